```python
import math
import jax, jax.numpy as jnp
from jax import lax
import numpy as np

D_MODEL = 1024
BATCH = 32
SEQ = 256
DEPTH = 4
DEC_BATCH = 8
DEC_SEQ = 1024
PAST_LEN = 512

GRID_W = 64
ROPE_BASE = 10000.0
Q_BLOCK = 128
EPS = 1e-6
N_EVEN = (DEPTH + 1) // 2
N_ODD = DEPTH // 2
H_A = 4
DK_A = 64
DV_A = 128
GLA_LR = 16
GLA_TAU = 16.0
GLA_CHUNK = 16
H_B = 4
Q_RANK = 256
KV_RANK = 256
NOPE_B = 128
ROPE_B = 64
V_B = 128
MLA_SCALE = (NOPE_B + ROPE_B) ** -0.5
H_C = 8
DH_C = 64
DIFF_SCALE = DH_C ** -0.5
D_FF = 2816
CONV_W = 3
EVEN_SIZES = (H_A * DK_A, H_A * DK_A, H_A * DV_A, H_A * DV_A, GLA_LR, GLA_LR, Q_RANK, KV_RANK, ROPE_B)
IN_EVEN = sum(EVEN_SIZES)
MIX_EVEN = H_A * DV_A + H_B * V_B
MIX_ODD = H_C * 2 * DH_C
IN_ODD = 3 * MIX_ODD

kernel_name = 'hybrid_flow_gla_mla_diffattn_step'


def rmsnorm(x, g):
    xf = x.astype(jnp.float32)
    y = xf * lax.rsqrt(jnp.mean(xf * xf, axis=-1, keepdims=True) + EPS)
    return (y * g.astype(jnp.float32)).astype(x.dtype)


def modulate(h, shift, scale):
    return h * (1.0 + scale) + shift


def rope_angles(rows, rot_dim):
    t = jnp.arange(rows * GRID_W)
    row = (t // GRID_W).astype(jnp.float32)
    col = (t % GRID_W).astype(jnp.float32)
    half = rot_dim // 2
    inv = ROPE_BASE ** (-jnp.arange(0, half, 2, dtype=jnp.float32) / half)
    return row[:, None] * inv, col[:, None] * inv


def _rot_half(x, ang):
    t, q = ang.shape
    shape = (t,) + (1,) * (x.ndim - 3) + (q,)
    cos = jnp.cos(ang).reshape(shape).astype(x.dtype)
    sin = jnp.sin(ang).reshape(shape).astype(x.dtype)
    x1, x2 = x[..., :q], x[..., q:]
    return jnp.concatenate([x1 * cos - x2 * sin, x2 * cos + x1 * sin], axis=-1)


def axial_rope(x, ang):
    ang_row, ang_col = ang
    half = x.shape[-1] // 2
    return jnp.concatenate([_rot_half(x[..., :half], ang_row), _rot_half(x[..., half:], ang_col)], axis=-1)


def map_query_blocks(fn, q):
    b, t = q.shape[:2]
    nb = t // Q_BLOCK
    qb = jnp.moveaxis(q.reshape((b, nb, Q_BLOCK) + q.shape[2:]), 1, 0)
    ob = lax.map(fn, qb)
    return jnp.moveaxis(ob, 0, 1).reshape((b, t) + ob.shape[3:])


def softmax_attend(q, k, v, scale):
    def blk(qb):
        s = jnp.einsum('bqhd,bkhd->bhqk', qb, k).astype(jnp.float32) * scale
        p = jax.nn.softmax(s, axis=-1).astype(v.dtype)
        return jnp.einsum('bhqk,bkhv->bqhv', p, v)
    return map_query_blocks(blk, q)


def diff_attend(q, k, v, lam, scale):
    def blk(qb):
        s = jnp.einsum('bqhnd,bkhnd->bnhqk', qb, k).astype(jnp.float32) * scale
        p = jax.nn.softmax(s, axis=-1)
        a = (p[:, 0] - lam * p[:, 1]).astype(v.dtype)
        return jnp.einsum('bhqk,bkhv->bqhv', a, v)
    return map_query_blocks(blk, q)


def gla_scan(q, k, v, log_a, s0):
    b, t, h, dk = q.shape
    dv = v.shape[-1]
    n = t // GLA_CHUNK
    f32 = jnp.float32

    def chunks(z):
        return z.reshape((b, n, GLA_CHUNK) + z.shape[2:]).astype(f32)

    qc, kc, vc = chunks(q), chunks(k), chunks(v)
    bcum = jnp.cumsum(chunks(log_a), axis=2)
    idx = jnp.arange(GLA_CHUNK)
    lower = (idx[:, None] >= idx[None, :])[None, None, :, :, None, None]
    rel = jnp.where(lower, bcum[:, :, :, None] - bcum[:, :, None, :], -jnp.inf)
    scores = jnp.einsum('bnthk,bnshk,bntshk->bntsh', qc, kc, jnp.exp(rel))
    o_intra = jnp.einsum('bntsh,bnshv->bnthv', scores, vc)
    b_last = bcum[:, :, -1]
    q_dec = qc * jnp.exp(bcum)
    k_dec = kc * jnp.exp(b_last[:, :, None] - bcum)

    def step(state, inp):
        qd, kd, vv, bl = inp
        o = jnp.einsum('bchk,bhkv->bchv', qd, state)
        state = jnp.exp(bl)[..., None] * state + jnp.einsum('bchk,bchv->bhkv', kd, vv)
        return state, o

    def lead(z):
        return jnp.moveaxis(z, 1, 0)

    s_fin, o_inter = lax.scan(step, s0.astype(f32), (lead(q_dec), lead(k_dec), lead(vc), lead(b_last)))
    o = o_intra + jnp.moveaxis(o_inter, 0, 1)
    return o.reshape(b, t, h, dv).astype(v.dtype), s_fin


def even_project(h, w_in, w_dec, b_dec, q_norm, kv_norm, w_uq):
    b, t = h.shape[:2]
    split_at = np.cumsum(EVEN_SIZES)[:-1].tolist()
    qa, ka, va, ga, lr_f, lr_b, cq, ckv, kr = jnp.split(h @ w_in, split_at, axis=-1)
    qa = qa.reshape(b, t, H_A, DK_A) * (DK_A ** -0.5)
    ka = ka.reshape(b, t, H_A, DK_A)
    va = va.reshape(b, t, H_A, DV_A)
    la_f = jax.nn.log_sigmoid((lr_f @ w_dec[0] + b_dec[0]).astype(jnp.float32)).reshape(b, t, H_A, DK_A) / GLA_TAU
    la_b = jax.nn.log_sigmoid((lr_b @ w_dec[1] + b_dec[1]).astype(jnp.float32)).reshape(b, t, H_A, DK_A) / GLA_TAU
    qb = (rmsnorm(cq, q_norm) @ w_uq).reshape(b, t, H_B, NOPE_B + ROPE_B)
    ckv = rmsnorm(ckv, kv_norm)
    return qa, ka, va, ga, la_f, la_b, qb, ckv, kr


def mla_expand(ckv, kr, w_ukv):
    b, s = ckv.shape[:2]
    kv = (ckv @ w_ukv).reshape(b, s, H_B, NOPE_B + V_B)
    k_nope, v = kv[..., :NOPE_B], kv[..., NOPE_B:]
    k = jnp.concatenate([k_nope, jnp.broadcast_to(kr[:, :, None, :], (b, s, H_B, ROPE_B))], axis=-1)
    return k, v


def gla_output(o_f, o_b_rev, ga, gla_gain):
    b, t = o_f.shape[:2]
    o = o_f + jnp.flip(o_b_rev, axis=1)
    o = rmsnorm(o, gla_gain) * jax.nn.silu(ga.reshape(b, t, H_A, DV_A))
    return o.reshape(b, t, H_A * DV_A)


def even_mixer_context(h, w_in, w_out, w_dec, b_dec, gla_gain, q_norm, kv_norm, w_uq, w_ukv):
    b, t = h.shape[:2]
    qa, ka, va, ga, la_f, la_b, qb, ckv, kr = even_project(h, w_in, w_dec, b_dec, q_norm, kv_norm, w_uq)
    s0 = jnp.zeros((b, H_A, DK_A, DV_A), jnp.float32)
    o_f, s_f = gla_scan(qa, ka, va, la_f, s0)
    fl = lambda z: jnp.flip(z, axis=1)
    o_b, s_b = gla_scan(fl(qa), fl(ka), fl(va), fl(la_b), s0)
    o_gla = gla_output(o_f, o_b, ga, gla_gain)
    k, v = mla_expand(ckv, kr, w_ukv)
    o_mla = softmax_attend(qb, k, v, MLA_SCALE).reshape(b, t, H_B * V_B)
    out = jnp.concatenate([o_gla, o_mla], axis=-1) @ w_out
    return out, jnp.stack([s_f, s_b], axis=1), ckv, kr


def even_mixer_latent(h, st_gla, ckv_ctx, kr_ctx, ang, w_in, w_out, w_dec, b_dec, gla_gain, q_norm, kv_norm, w_uq, w_ukv):
    b, t = h.shape[:2]
    qa, ka, va, ga, la_f, la_b, qb, ckv, kr = even_project(h, w_in, w_dec, b_dec, q_norm, kv_norm, w_uq)
    o_f, _ = gla_scan(qa, ka, va, la_f, st_gla[:, 0])
    fl = lambda z: jnp.flip(z, axis=1)
    o_b, _ = gla_scan(fl(qa), fl(ka), fl(va), fl(la_b), st_gla[:, 1])
    o_gla = gla_output(o_f, o_b, ga, gla_gain)
    qb = jnp.concatenate([qb[..., :NOPE_B], axial_rope(qb[..., NOPE_B:], ang)], axis=-1)
    k_lat, v_lat = mla_expand(ckv, axial_rope(kr, ang), w_ukv)
    k_ctx, v_ctx = mla_expand(ckv_ctx, kr_ctx, w_ukv)
    k = jnp.concatenate([k_ctx, k_lat], axis=1)
    v = jnp.concatenate([v_ctx, v_lat], axis=1)
    o_mla = softmax_attend(qb, k, v, MLA_SCALE).reshape(b, t, H_B * V_B)
    return jnp.concatenate([o_gla, o_mla], axis=-1) @ w_out


def diff_lambda_value(lam_p, lam_init):
    lp = lam_p.astype(jnp.float32)
    return jnp.exp(jnp.sum(lp[0] * lp[1])) - jnp.exp(jnp.sum(lp[2] * lp[3])) + lam_init


def diff_project(h, w_in):
    b, t = h.shape[:2]
    q, k, v = jnp.split(h @ w_in, 3, axis=-1)
    return (q.reshape(b, t, H_C, 2, DH_C), k.reshape(b, t, H_C, 2, DH_C), v.reshape(b, t, H_C, 2 * DH_C))


def diff_output(o, w_out, d_gain, lam_init):
    b, t = o.shape[:2]
    o = rmsnorm(o, d_gain) * (1.0 - lam_init)
    return o.reshape(b, t, MIX_ODD) @ w_out


def diff_mixer_context(h, w_in, w_out, d_gain, lam, lam_init):
    b, t = h.shape[:2]
    q, k, v = diff_project(h, w_in)
    o = diff_attend(q, k, v, lam, DIFF_SCALE)
    return diff_output(o, w_out, d_gain, lam_init), k.reshape(b, t, H_C, 2 * DH_C), v


def diff_mixer_latent(h, k_ctx, v_ctx, ang, w_in, w_out, d_gain, lam, lam_init):
    b = h.shape[0]
    s = k_ctx.shape[1]
    q, k, v = diff_project(h, w_in)
    q = axial_rope(q, ang)
    k = axial_rope(k, ang)
    k_all = jnp.concatenate([k_ctx.reshape(b, s, H_C, 2, DH_C), k], axis=1)
    v_all = jnp.concatenate([v_ctx, v], axis=1)
    o = diff_attend(q, k_all, v_all, lam, DIFF_SCALE)
    return diff_output(o, w_out, d_gain, lam_init)


def conv_ffn(h, w_up, conv_w, conv_b, w_down):
    u = h @ w_up
    p = jnp.pad(u, ((0, 0), (1, 1), (0, 0)))
    u = p[:, :-2] * conv_w[0] + p[:, 1:-1] * conv_w[1] + p[:, 2:] * conv_w[2] + conv_b
    a, g = jnp.split(u, 2, axis=-1)
    return (jax.nn.silu(a) * g) @ w_down


def setup_inputs(seed: int = 0) -> dict:
    key = jax.random.key(seed)
    ks = iter(jax.random.split(key, 32))

    def nrm(shape, scale):
        return jax.random.normal(next(ks), shape, jnp.float32) * scale

    def gain(shape):
        return 1.0 + nrm(shape, 0.05)

    return {
        'x_prompt': nrm((BATCH, SEQ, D_MODEL), 1.0),
        'x_sample': nrm((DEC_BATCH, DEC_SEQ, D_MODEL), 1.0),
        'state_gla': nrm((DEC_BATCH, N_EVEN, 2, H_A, DK_A, DV_A), 1.0),
        'cache_mla_ckv': nrm((DEC_BATCH, N_EVEN, PAST_LEN, KV_RANK), 1.0),
        'cache_mla_krope': nrm((DEC_BATCH, N_EVEN, PAST_LEN, ROPE_B), 1.0),
        'cache_diff_k': nrm((DEC_BATCH, N_ODD, PAST_LEN, H_C, 2 * DH_C), 1.0),
        'cache_diff_v': nrm((DEC_BATCH, N_ODD, PAST_LEN, H_C, 2 * DH_C), 1.0),
        'c': nrm((DEC_BATCH, D_MODEL), 1.0),
        'c_ctx': nrm((D_MODEL,), 1.0),
        'w_ada': nrm((DEPTH, D_MODEL, 6 * D_MODEL), 0.5 * D_MODEL ** -0.5),
        'b_ada': nrm((DEPTH, 6 * D_MODEL), 0.02),
        'norm_gain': gain((DEPTH, 2, D_MODEL)),
        'final_gain': gain((D_MODEL,)),
        'w_in_even': nrm((N_EVEN, D_MODEL, IN_EVEN), D_MODEL ** -0.5),
        'w_out_even': nrm((N_EVEN, MIX_EVEN, D_MODEL), MIX_EVEN ** -0.5),
        'gla_w_decay': nrm((N_EVEN, 2, GLA_LR, H_A * DK_A), GLA_LR ** -0.5),
        'gla_b_decay': nrm((N_EVEN, 2, H_A * DK_A), 0.1),
        'gla_norm': gain((N_EVEN, DV_A)),
        'mla_q_norm': gain((N_EVEN, Q_RANK)),
        'mla_kv_norm': gain((N_EVEN, KV_RANK)),
        'mla_w_uq': nrm((N_EVEN, Q_RANK, H_B * (NOPE_B + ROPE_B)), Q_RANK ** -0.5),
        'mla_w_ukv': nrm((N_EVEN, KV_RANK, H_B * (NOPE_B + V_B)), KV_RANK ** -0.5),
        'w_in_odd': nrm((N_ODD, D_MODEL, IN_ODD), D_MODEL ** -0.5),
        'w_out_odd': nrm((N_ODD, MIX_ODD, D_MODEL), MIX_ODD ** -0.5),
        'diff_lambda': nrm((N_ODD, 4, DH_C), 0.1),
        'diff_norm': gain((N_ODD, 2 * DH_C)),
        'ffn_w_up': nrm((DEPTH, D_MODEL, 2 * D_FF), D_MODEL ** -0.5),
        'ffn_conv_w': nrm((DEPTH, CONV_W, 2 * D_FF), CONV_W ** -0.5),
        'ffn_conv_b': nrm((DEPTH, 2 * D_FF), 0.02),
        'ffn_w_down': nrm((DEPTH, D_FF, D_MODEL), D_FF ** -0.5),
    }


def reference(x_prompt, x_sample, state_gla, cache_mla_ckv, cache_mla_krope, cache_diff_k, cache_diff_v, c,
              c_ctx, w_ada, b_ada, norm_gain, final_gain, w_in_even, w_out_even, gla_w_decay, gla_b_decay,
              gla_norm, mla_q_norm, mla_kv_norm, mla_w_uq, mla_w_ukv, w_in_odd, w_out_odd, diff_lambda,
              diff_norm, ffn_w_up, ffn_conv_w, ffn_conv_b, ffn_w_down):
    rows = x_sample.shape[1] // GRID_W
    ang_b = rope_angles(rows, ROPE_B)
    ang_c = rope_angles(rows, DH_C)
    s_ctx = jax.nn.silu(c_ctx)[None, :]
    s_lat = jax.nn.silu(c)
    xp, xs = x_prompt, x_sample
    gla_states, mla_ckvs, mla_krs, diff_ks, diff_vs = [], [], [], [], []
    for l in range(DEPTH):
        mp = jnp.split((s_ctx @ w_ada[l] + b_ada[l])[:, None, :], 6, axis=-1)
        ms = jnp.split((s_lat @ w_ada[l] + b_ada[l])[:, None, :], 6, axis=-1)
        hp = modulate(rmsnorm(xp, norm_gain[l, 0]), mp[0], mp[1])
        hs = modulate(rmsnorm(xs, norm_gain[l, 0]), ms[0], ms[1])
        i = l // 2
        if l % 2 == 0:
            ep = (w_in_even[i], w_out_even[i], gla_w_decay[i], gla_b_decay[i], gla_norm[i],
                  mla_q_norm[i], mla_kv_norm[i], mla_w_uq[i], mla_w_ukv[i])
            op, st, ckv, kr = even_mixer_context(hp, *ep)
            os_ = even_mixer_latent(hs, state_gla[:, i], cache_mla_ckv[:, i], cache_mla_krope[:, i], ang_b, *ep)
            gla_states.append(st)
            mla_ckvs.append(ckv)
            mla_krs.append(kr)
        else:
            lam_init = 0.8 - 0.6 * math.exp(-0.3 * l)
            lam = diff_lambda_value(diff_lambda[i], lam_init)
            op, k_new, v_new = diff_mixer_context(hp, w_in_odd[i], w_out_odd[i], diff_norm[i], lam, lam_init)
            os_ = diff_mixer_latent(hs, cache_diff_k[:, i], cache_diff_v[:, i], ang_c, w_in_odd[i], w_out_odd[i],
                                    diff_norm[i], lam, lam_init)
            diff_ks.append(k_new)
            diff_vs.append(v_new)
        xp = xp + mp[2] * op
        xs = xs + ms[2] * os_
        hp = modulate(rmsnorm(xp, norm_gain[l, 1]), mp[3], mp[4])
        hs = modulate(rmsnorm(xs, norm_gain[l, 1]), ms[3], ms[4])
        xp = xp + mp[5] * conv_ffn(hp, ffn_w_up[l], ffn_conv_w[l], ffn_conv_b[l], ffn_w_down[l])
        xs = xs + ms[5] * conv_ffn(hs, ffn_w_up[l], ffn_conv_w[l], ffn_conv_b[l], ffn_w_down[l])
    y_prompt = rmsnorm(xp, final_gain)
    y_sample = rmsnorm(xs, final_gain)
    return (y_prompt, y_sample, jnp.stack(gla_states, axis=1), jnp.stack(mla_ckvs, axis=1),
            jnp.stack(mla_krs, axis=1), jnp.stack(diff_ks, axis=1), jnp.stack(diff_vs, axis=1))
```

```python
import functools
import math

import numpy as np
import jax
import jax.numpy as jnp
from jax import lax
from jax.experimental import pallas as pl
from jax.experimental.pallas import tpu as pltpu

GRID_W = 64
ROPE_BASE = 10000.0
EPS = 1e-6
H_A, DK_A, DV_A = 4, 64, 128
GLA_LR = 16
GLA_TAU = 16.0
H_B, Q_RANK, KV_RANK, NOPE_B, ROPE_B, V_B = 4, 256, 256, 128, 64, 128
MLA_SCALE = (NOPE_B + ROPE_B) ** -0.5
H_C, DH_C = 8, 64
DIFF_SCALE = DH_C ** -0.5

LANES = 128
SUBLANES = 8
VMEM_LIMIT = 56 * 1024 * 1024

TM = 256
TQ = 256
GLA_C = 64
ADA_TN = 1536
FFN_TF = 256
ADA_ROWS = 16

F32 = jnp.float32
BF16 = jnp.bfloat16


def _dot(a, b):
    return jnp.dot(a, b, preferred_element_type=F32)


def _dot_nt(a, b):
    return lax.dot_general(a, b, (((1,), (1,)), ((), ())), preferred_element_type=F32)


def _silu(x):
    return x * (1.0 / (1.0 + jnp.exp(-x)))


def _rms(x, gain):
    return x * lax.rsqrt(jnp.mean(x * x, axis=-1, keepdims=True) + EPS) * gain


def _normmod(x, gain, shift, scale):
    return _rms(x, gain) * (1.0 + scale) + shift


def _rope(x, cos, sin_dn, sin_up):
    return (x * cos + pltpu.roll(x, 16, axis=1) * sin_dn + pltpu.roll(x, LANES - 16, axis=1) * sin_up)


def _cparams(sem, vmem=None):
    return pltpu.CompilerParams(dimension_semantics=sem, vmem_limit_bytes=vmem)


def _resident(shape):
    nd = len(shape)
    return pl.BlockSpec(shape, lambda *_: (0,) * nd, pipeline_mode=pl.Buffered(1))


def _ada_kernel(c_ref, w_ref, b_ref, o_ref):
    s = _silu(c_ref[...]).astype(BF16)
    o_ref[0] = _dot(s, w_ref[0].astype(BF16)) + b_ref[0]


def _ada_call(cond, w_ada, b_ada):
    depth, d, n = w_ada.shape
    return pl.pallas_call(
        _ada_kernel,
        grid=(depth, n // ADA_TN),
        in_specs=[
            pl.BlockSpec((ADA_ROWS, d), lambda l, j: (0, 0)),
            pl.BlockSpec((1, d, ADA_TN), lambda l, j: (l, 0, j)),
            pl.BlockSpec((1, 1, ADA_TN), lambda l, j: (l, 0, j)),
        ],
        out_specs=pl.BlockSpec((1, ADA_ROWS, ADA_TN), lambda l, j: (l, 0, j)),
        out_shape=jax.ShapeDtypeStruct((depth, ADA_ROWS, n), F32),
        name="ada",
        compiler_params=_cparams(("arbitrary", "arbitrary"), VMEM_LIMIT),
    )(cond, w_ada, b_ada.reshape(depth, 1, n))


class _Layout:
    def __init__(self, n_prompt, n_sample, seq_p, seq_s):
        assert n_prompt % TM == 0 and n_sample % TM == 0 and seq_p % TM == 0 and seq_s % TM == 0
        self.np, self.ns, self.nt = n_prompt, n_sample, n_prompt + n_sample
        self.seq_p, self.seq_s = seq_p, seq_s
        self.pt = n_prompt // TM
        self.st = n_sample // TM
        self.tiles = self.pt + self.st
        self.tiles_per_seq_s = seq_s // TM

    def group(self, i):
        return jnp.where(i < self.pt, 0, 1 + (i - self.pt) // self.tiles_per_seq_s)

    def pos_block(self, i):
        return jnp.where(i < self.pt, 0, self.tiles_per_seq_s + (i - self.pt) % self.tiles_per_seq_s)

    def p_block(self, i):
        return jnp.minimum(i, self.pt - 1)

    def s_block(self, i):
        return jnp.maximum(i - self.pt, 0)


def _mod_spec(lay, d):
    return pl.BlockSpec((1, 6, d), lambda i: (lay.group(i), 0, 0))


def _tab_spec(lay):
    return pl.BlockSpec((3, TM, LANES), lambda i: (0, lay.pos_block(i), 0))


def _even_proj_kernel(x_ref, mod_ref, gain_ref, w_ref, wd_ref, bd_ref, qn_ref, kvn_ref, wuq_ref, wukv_ref,
                      tab_ref, qk_ref, la_ref, v_ref, ga_ref, qb_ref, ckv_ref, kr_ref, kra_ref, kv_ref):
    mod = mod_ref[0]
    h = _normmod(x_ref[...], gain_ref[...], mod[0:1], mod[1:2]).astype(BF16)
    y = _dot(h, w_ref[...])
    qk_ref[...] = y[:, 0:512]
    v_ref[...] = y[:, 512:1024].astype(BF16)
    ga_ref[...] = y[:, 1024:1536]
    cq, ckv, g = y[:, 1536:1792], y[:, 1792:2048], y[:, 2048:2176]
    pre = _dot(g.astype(BF16), wd_ref[...]) + bd_ref[...]
    la_ref[...] = (jnp.minimum(pre, 0.0) - jnp.log(1.0 + jnp.exp(-jnp.abs(pre)))) * (1.0 / GLA_TAU)
    cos, sin_dn, sin_up = tab_ref[0], tab_ref[1], tab_ref[2]
    qb = _dot(_rms(cq, qn_ref[...]).astype(BF16), wuq_ref[...])
    for hh in range(H_B):
        lo = hh * 2 * LANES
        qb_ref[:, lo:lo + LANES] = qb[:, lo:lo + LANES].astype(BF16)
        qb_ref[:, lo + LANES:lo + 2 * LANES] = _rope(qb[:, lo + LANES:lo + 2 * LANES], cos, sin_dn, sin_up).astype(BF16)
    ckvn = _rms(ckv, kvn_ref[...])
    ckv_ref[...] = ckvn
    kv_ref[...] = _dot(ckvn.astype(BF16), wukv_ref[...]).astype(BF16)
    kr_ref[...] = g[:, :ROPE_B]
    lane = lax.broadcasted_iota(jnp.int32, g.shape, 1)
    kra_ref[...] = jnp.where(lane < ROPE_B, _rope(g, cos, sin_dn, sin_up), 0.0).astype(BF16)


def _even_proj_call(lay, x, mod, gain, w, wd, bd, qn, kvn, wuq, wukv, tab):
    d = x.shape[1]
    row = lambda n: pl.BlockSpec((TM, n), lambda i: (i, 0))
    outs = [(512, F32), (512, F32), (512, BF16), (512, F32), (1024, BF16), (256, F32), (ROPE_B, F32),
            (LANES, BF16), (1024, BF16)]
    return pl.pallas_call(
        _even_proj_kernel,
        grid=(lay.tiles,),
        in_specs=[row(d), _mod_spec(lay, d), _resident((1, d)), _resident(w.shape), _resident(wd.shape),
                  _resident(bd.shape), _resident(qn.shape), _resident(kvn.shape), _resident(wuq.shape),
                  _resident(wukv.shape), _tab_spec(lay)],
        out_specs=[row(n) for n, _ in outs],
        out_shape=[jax.ShapeDtypeStruct((lay.nt, n), dt) for n, dt in outs],
        name="even_proj",
        compiler_params=_cparams(("arbitrary",), VMEM_LIMIT),
    )(x, mod, gain, w, wd, bd, qn, kvn, wuq, wukv, tab)


def _gla_consts(c):
    levels = int(math.log2(c))
    idx = np.arange(c)
    stacks, masks = [], []
    for backward in (False, True):
        cum = (idx[None, :] >= idx[:, None]) if backward else (idx[None, :] <= idx[:, None])
        rows, mk = [cum], []
        for lev in range(levels):
            half = 1 << lev
            node = idx // (2 * half)
            mid = node * 2 * half + half
            upper = idx >= mid
            rows.append(cum[mid] if backward else cum[mid - 1])
            q_half, k_half = (~upper, upper) if backward else (upper, ~upper)
            mk.append((node[:, None] == node[None, :]) & q_half[:, None] & k_half[None, :])
        mk.append(np.eye(c, dtype=bool))
        stacks.append(np.concatenate(rows, 0))
        masks.append(np.stack(mk, 0))
    return (jnp.asarray(np.stack(stacks, 0), BF16), jnp.asarray(np.stack(masks, 0), F32), levels)


def _gla_chunk(qk, la, v, st, m_ref, mask_ref, d, levels):
    c = GLA_C
    q, k = qk[:, :DK_A], qk[:, DK_A:]
    hi = la.astype(BF16)
    r1 = la - hi.astype(F32)
    mid = r1.astype(BF16)
    lo = (r1 - mid.astype(F32)).astype(BF16)
    mm = m_ref[d]
    e = _dot(mm, hi) + _dot(mm, mid) + _dot(mm, lo)
    cum = e[0:c]
    tot = cum[0:1] if d == 1 else cum[c - 1:c]
    a = _dot_nt(q.astype(BF16), k.astype(BF16)) * mask_ref[d, levels]
    for lev in range(levels):
        g = e[(lev + 1) * c:(lev + 2) * c]
        ql = (q * jnp.exp(jnp.minimum(cum - g, 0.0))).astype(BF16)
        kl = (k * jnp.exp(jnp.minimum(g - cum, 0.0))).astype(BF16)
        a = a + _dot_nt(ql, kl) * mask_ref[d, lev]
    qd = (q * jnp.exp(cum)).astype(BF16)
    kd = (k * jnp.exp(jnp.minimum(tot - cum, 0.0))).astype(BF16)
    o = _dot(a.astype(BF16), v) + _dot_nt(qd, st.astype(BF16))
    vt = jnp.transpose(v.astype(F32)).astype(BF16)
    st_new = st * jnp.exp(tot) + _dot(vt, kd)
    return o, st_new


def _gla_kernel(*refs, t, has_s0, levels):
    if has_s0:
        qk_ref, la_ref, v_ref, ga_ref, gain_ref, m_ref, mask_ref, s0_ref, o_ref, st_ref, of_ref, ob_ref = refs
    else:
        qk_ref, la_ref, v_ref, ga_ref, gain_ref, m_ref, mask_ref, o_ref, st_ref, of_ref, ob_ref = refs
    c = GLA_C
    n = t // c
    if has_s0:
        sf0 = jnp.transpose(s0_ref[0, 0, 0])
        sb0 = jnp.transpose(s0_ref[0, 1, 0])
    else:
        sf0 = jnp.zeros((DV_A, DK_A), F32)
        sb0 = jnp.zeros((DV_A, DK_A), F32)

    def body(j, carry):
        sf, sb = carry
        of = pl.multiple_of(j * c, c)
        ob = pl.multiple_of((n - 1 - j) * c, c)
        o, sf = _gla_chunk(qk_ref[pl.ds(of, c), :], la_ref[pl.ds(of, c), :][:, :DK_A], v_ref[pl.ds(of, c), :],
                           sf, m_ref, mask_ref, 0, levels)
        of_ref[pl.ds(of, c), :] = o
        o, sb = _gla_chunk(qk_ref[pl.ds(ob, c), :], la_ref[pl.ds(ob, c), :][:, DK_A:], v_ref[pl.ds(ob, c), :],
                           sb, m_ref, mask_ref, 1, levels)
        ob_ref[pl.ds(ob, c), :] = o
        return sf, sb

    sf, sb = lax.fori_loop(0, n, body, (sf0, sb0))
    st_ref[0, 0, 0] = jnp.transpose(sf)
    st_ref[0, 1, 0] = jnp.transpose(sb)
    o = _rms(of_ref[...] + ob_ref[...], gain_ref[...]) * _silu(ga_ref[...])
    o_ref[...] = o.astype(BF16)


def _gla_call(qk, la, v, ga, gain, consts, s0, row0, nb, t):
    m, mask, levels = consts
    blk0 = row0 // t
    tok = pl.BlockSpec((t, LANES), lambda b, h: (blk0 + b, h))
    in_specs = [tok, tok, tok, tok, _resident(gain.shape), _resident(m.shape), _resident(mask.shape)]
    args = [qk, la, v, ga, gain, m, mask]
    st_spec = pl.BlockSpec((1, 2, 1, DK_A, DV_A), lambda b, h: (b, 0, h, 0, 0))
    if s0 is not None:
        in_specs.append(st_spec)
        args.append(s0)
    return pl.pallas_call(
        functools.partial(_gla_kernel, t=t, has_s0=s0 is not None, levels=levels),
        grid=(nb, H_A),
        in_specs=in_specs,
        out_specs=[pl.BlockSpec((t, LANES), lambda b, h: (b, h)), st_spec],
        out_shape=[jax.ShapeDtypeStruct((nb * t, H_A * DV_A), BF16),
                   jax.ShapeDtypeStruct((nb, 2, H_A, DK_A, DV_A), F32)],
        scratch_shapes=[pltpu.VMEM((t, DV_A), F32), pltpu.VMEM((t, DV_A), F32)],
        name="gla",
        compiler_params=_cparams(("arbitrary", "arbitrary"), VMEM_LIMIT),
    )(*args)


def _kv_expand_kernel(c_ref, w_ref, o_ref):
    o_ref[...] = _dot(c_ref[...].astype(BF16), w_ref[...]).astype(BF16)


def _kv_expand_call(ckv, wukv):
    n, r = ckv.shape
    tm = 512
    assert n % tm == 0
    return pl.pallas_call(
        _kv_expand_kernel,
        grid=(n // tm,),
        in_specs=[pl.BlockSpec((tm, r), lambda i: (i, 0)), _resident(wukv.shape)],
        out_specs=pl.BlockSpec((tm, wukv.shape[1]), lambda i: (i, 0)),
        out_shape=jax.ShapeDtypeStruct((n, wukv.shape[1]), BF16),
        name="kv_expand",
        compiler_params=_cparams(("arbitrary",), VMEM_LIMIT),
    )(ckv, wukv)


def _mla_attn_kernel(*refs, nseg):
    q_ref = refs[0]
    seg = [(refs[1 + 2 * s], refs[2 + 2 * s]) for s in range(nseg)]
    o_ref = refs[1 + 2 * nseg]
    for hh in range(H_B):
        q = q_ref[:, hh * 2 * LANES:(hh + 1) * 2 * LANES]
        scores = []
        for kv_ref, kr_ref in seg:
            kcat = jnp.concatenate([kv_ref[:, hh * 2 * LANES:hh * 2 * LANES + LANES], kr_ref[...]], axis=1)
            scores.append(_dot_nt(q, kcat) * MLA_SCALE)
        mx = functools.reduce(jnp.maximum, [jnp.max(s, axis=-1, keepdims=True) for s in scores])
        ps = [jnp.exp(s - mx) for s in scores]
        den = functools.reduce(lambda a, b: a + b, [jnp.sum(p, axis=-1, keepdims=True) for p in ps])
        acc = functools.reduce(lambda a, b: a + b, [
            _dot(p.astype(BF16), kv_ref[:, hh * 2 * LANES + LANES:(hh + 1) * 2 * LANES])
            for p, (kv_ref, _) in zip(ps, seg)])
        o_ref[:, hh * LANES:(hh + 1) * LANES] = (acc * (1.0 / den)).astype(BF16)


def _mla_attn_call(qb, segs, row0, nb, t):
    q0 = row0 // TQ
    per = t // TQ
    in_specs = [pl.BlockSpec((TQ, qb.shape[1]), lambda b, j: (q0 + b * per + j, 0))]
    args = [qb]
    for kv, kr, r0, s in segs:
        b0 = r0 // s
        in_specs.append(pl.BlockSpec((s, kv.shape[1]), lambda b, j, b0=b0: (b0 + b, 0)))
        in_specs.append(pl.BlockSpec((s, LANES), lambda b, j, b0=b0: (b0 + b, 0)))
        args += [kv, kr]
    return pl.pallas_call(
        functools.partial(_mla_attn_kernel, nseg=len(segs)),
        grid=(nb, per),
        in_specs=in_specs,
        out_specs=pl.BlockSpec((TQ, H_B * V_B), lambda b, j: (b * per + j, 0)),
        out_shape=jax.ShapeDtypeStruct((nb * t, H_B * V_B), BF16),
        name="mla_attn",
        compiler_params=_cparams(("arbitrary", "arbitrary"), VMEM_LIMIT),
    )(*args)


def _odd_proj_kernel(x_ref, mod_ref, gain_ref, w_ref, tab_ref, q_ref, kf_ref, kb_ref, vf_ref, vb_ref):
    mod = mod_ref[0]
    h = _normmod(x_ref[...], gain_ref[...], mod[0:1], mod[1:2]).astype(BF16)
    y = _dot(h, w_ref[...])
    n = q_ref.shape[1]
    cos, sin_dn, sin_up = tab_ref[0], tab_ref[1], tab_ref[2]
    kf_ref[...] = y[:, n:2 * n]
    vf_ref[...] = y[:, 2 * n:3 * n]
    vb_ref[...] = y[:, 2 * n:3 * n].astype(BF16)
    for g in range(n // LANES):
        lo = g * LANES
        q_ref[:, lo:lo + LANES] = _rope(y[:, lo:lo + LANES], cos, sin_dn, sin_up).astype(BF16)
        kb_ref[:, lo:lo + LANES] = _rope(y[:, n + lo:n + lo + LANES], cos, sin_dn, sin_up).astype(BF16)


def _odd_proj_call(lay, x, mod, gain, w, tab):
    d = x.shape[1]
    n = w.shape[1] // 3
    row = lambda m: pl.BlockSpec((TM, m), lambda i: (i, 0))
    outs = [BF16, F32, BF16, F32, BF16]
    return pl.pallas_call(
        _odd_proj_kernel,
        grid=(lay.tiles,),
        in_specs=[row(d), _mod_spec(lay, d), _resident((1, d)), _resident(w.shape), _tab_spec(lay)],
        out_specs=[row(n) for _ in outs],
        out_shape=[jax.ShapeDtypeStruct((lay.nt, n), dt) for dt in outs],
        name="odd_proj",
        compiler_params=_cparams(("arbitrary",), VMEM_LIMIT),
    )(x, mod, gain, w, tab)


def _diff_attn_kernel(*refs, nseg, lam_init):
    q_ref, dl_ref, gain_ref = refs[0], refs[1], refs[2]
    seg = [(refs[3 + 2 * s], refs[4 + 2 * s]) for s in range(nseg)]
    o_ref = refs[3 + 2 * nseg]
    dl = dl_ref[...]
    lam = (jnp.exp(jnp.sum(dl[0:1] * dl[1:2], axis=-1, keepdims=True))
           - jnp.exp(jnp.sum(dl[2:3] * dl[3:4], axis=-1, keepdims=True)) + lam_init)
    tq = q_ref.shape[0]
    lane = lax.broadcasted_iota(jnp.int32, (1, LANES), 1)
    keep1 = jnp.where(lane < DH_C, 1.0, 0.0).astype(BF16)
    keep2 = jnp.where(lane < DH_C, 0.0, 1.0).astype(BF16)
    for hh in range(H_C):
        q = q_ref[:, hh * LANES:(hh + 1) * LANES]
        qq = jnp.concatenate([q * keep1, q * keep2], axis=0)
        scores = [_dot_nt(qq, k_ref[:, hh * LANES:(hh + 1) * LANES]) for k_ref, _ in seg]
        mx = functools.reduce(jnp.maximum, [jnp.max(s, axis=-1, keepdims=True) for s in scores])
        ps = [jnp.exp(s - mx) for s in scores]
        den = functools.reduce(lambda a, b: a + b, [jnp.sum(p, axis=-1, keepdims=True) for p in ps])
        inv = 1.0 / den
        w1, w2 = inv[:tq], inv[tq:] * lam
        acc = functools.reduce(lambda a, b: a + b, [
            _dot((p[:tq] * w1 - p[tq:] * w2).astype(BF16), v_ref[:, hh * LANES:(hh + 1) * LANES])
            for p, (_, v_ref) in zip(ps, seg)])
        o_ref[:, hh * LANES:(hh + 1) * LANES] = (_rms(acc, gain_ref[...]) * (1.0 - lam_init)).astype(BF16)


def _diff_attn_call(q, dl, gain, segs, row0, nb, t, lam_init):
    q0 = row0 // TQ
    per = t // TQ
    n = q.shape[1]
    in_specs = [pl.BlockSpec((TQ, n), lambda b, j: (q0 + b * per + j, 0)), _resident(dl.shape), _resident(gain.shape)]
    args = [q, dl, gain]
    for k, v, r0, s in segs:
        b0 = r0 // s
        in_specs.append(pl.BlockSpec((s, n), lambda b, j, b0=b0: (b0 + b, 0)))
        in_specs.append(pl.BlockSpec((s, n), lambda b, j, b0=b0: (b0 + b, 0)))
        args += [k, v]
    return pl.pallas_call(
        functools.partial(_diff_attn_kernel, nseg=len(segs), lam_init=lam_init),
        grid=(nb, per),
        in_specs=in_specs,
        out_specs=pl.BlockSpec((TQ, n), lambda b, j: (b * per + j, 0)),
        out_shape=jax.ShapeDtypeStruct((nb * t, n), BF16),
        name="diff_attn",
        compiler_params=_cparams(("arbitrary", "arbitrary"), VMEM_LIMIT),
    )(*args)


def _out_proj_kernel(*refs, npairs, prompt_tiles):
    x_ref, mod_ref = refs[0], refs[1]
    mix = refs[2:2 + 2 * npairs]
    w_refs = refs[2 + 2 * npairs:2 + 3 * npairs]
    o_ref = refs[2 + 3 * npairs]
    is_prompt = pl.program_id(0) < prompt_tiles
    acc = None
    for p in range(npairs):
        a = jnp.where(is_prompt, mix[2 * p][...], mix[2 * p + 1][...])
        part = _dot(a, w_refs[p][...])
        acc = part if acc is None else acc + part
    o_ref[...] = x_ref[...] + mod_ref[0][2:3] * acc


def _out_proj_call(lay, x, mod, pairs, ws):
    d = x.shape[1]
    in_specs = [pl.BlockSpec((TM, d), lambda i: (i, 0)), _mod_spec(lay, d)]
    args = [x, mod]
    for a_p, a_s in pairs:
        in_specs.append(pl.BlockSpec((TM, a_p.shape[1]), lambda i: (lay.p_block(i), 0)))
        in_specs.append(pl.BlockSpec((TM, a_s.shape[1]), lambda i: (lay.s_block(i), 0)))
        args += [a_p, a_s]
    for w in ws:
        in_specs.append(_resident(w.shape))
        args.append(w)
    return pl.pallas_call(
        functools.partial(_out_proj_kernel, npairs=len(pairs), prompt_tiles=lay.pt),
        name="out_proj",
        grid=(lay.tiles,),
        in_specs=in_specs,
        out_specs=pl.BlockSpec((TM, d), lambda i: (i, 0)),
        out_shape=jax.ShapeDtypeStruct((lay.nt, d), F32),
        compiler_params=_cparams(("arbitrary",), VMEM_LIMIT),
    )(*args)


def _ffn_kernel(x_ref, xp_ref, xn_ref, mod_ref, gain_ref, wup_ref, cw_ref, cb_ref, wdn_ref, fg_ref, o_ref, *,
                lay, d_ff, final):
    i = pl.program_id(0)
    row = i * TM
    seq = jnp.where(i < lay.pt, lay.seq_p, lay.seq_s)
    first = (row & (seq - 1)) == 0
    last = ((row + TM) & (seq - 1)) == 0
    mod = mod_ref[0]
    gain, shift, scale = gain_ref[...], mod[3:4], mod[4:5]
    x = x_ref[...]
    hp = jnp.where(first, 0.0, _normmod(xp_ref[...], gain, shift, scale))
    hn = jnp.where(last, 0.0, _normmod(xn_ref[...], gain, shift, scale))
    h = jnp.concatenate([hp, _normmod(x, gain, shift, scale), hn], axis=0).astype(BF16)
    acc = jnp.zeros((TM, x.shape[1]), F32)
    for f in range(d_ff // FFN_TF):
        halves = []
        for base in (0, d_ff):
            lo = base + f * FFN_TF
            u = _dot(h, wup_ref[:, lo:lo + FFN_TF])
            cw = cw_ref[:, lo:lo + FFN_TF]
            halves.append(u[SUBLANES - 1:SUBLANES - 1 + TM] * cw[0:1] + u[SUBLANES:SUBLANES + TM] * cw[1:2]
                          + u[SUBLANES + 1:SUBLANES + 1 + TM] * cw[2:3] + cb_ref[:, lo:lo + FFN_TF])
        act = (_silu(halves[0]) * halves[1]).astype(BF16)
        acc = acc + _dot(act, wdn_ref[f * FFN_TF:(f + 1) * FFN_TF, :])
    y = x + mod[5:6] * acc
    o_ref[...] = _rms(y, fg_ref[...]) if final else y


def _ffn_call(lay, x, mod, gain, wup, cw, cb, wdn, fgain, final):
    d = x.shape[1]
    d_ff = wdn.shape[0]
    assert d_ff % FFN_TF == 0
    halo = TM // SUBLANES
    last_blk = lay.nt // SUBLANES - 1
    return pl.pallas_call(
        functools.partial(_ffn_kernel, lay=lay, d_ff=d_ff, final=final),
        name="ffn",
        grid=(lay.tiles,),
        in_specs=[
            pl.BlockSpec((TM, d), lambda i: (i, 0)),
            pl.BlockSpec((SUBLANES, d), lambda i: (jnp.maximum(i * halo - 1, 0), 0)),
            pl.BlockSpec((SUBLANES, d), lambda i: (jnp.minimum((i + 1) * halo, last_blk), 0)),
            _mod_spec(lay, d), _resident((1, d)), _resident(wup.shape), _resident(cw.shape), _resident(cb.shape),
            _resident(wdn.shape), _resident((1, d)),
        ],
        out_specs=pl.BlockSpec((TM, d), lambda i: (i, 0)),
        out_shape=jax.ShapeDtypeStruct((lay.nt, d), F32),
        compiler_params=_cparams(("arbitrary",), VMEM_LIMIT),
    )(x, x, x, mod, gain, wup, cw, cb, wdn, fgain)


def _rope_tables(seq_s, both_halves):
    t = jnp.arange(seq_s)
    rowp = (t // GRID_W).astype(F32)
    colp = (t % GRID_W).astype(F32)
    half = ROPE_B // 2
    inv = ROPE_BASE ** (-jnp.arange(0, half, 2, dtype=F32) / half)
    ar, ac = rowp[:, None] * inv, colp[:, None] * inv
    zero = jnp.zeros_like(ar)
    cos = jnp.concatenate([jnp.cos(ar), jnp.cos(ar), jnp.cos(ac), jnp.cos(ac)], -1)
    sin_dn = jnp.concatenate([zero, jnp.sin(ar), zero, jnp.sin(ac)], -1)
    sin_up = jnp.concatenate([-jnp.sin(ar), zero, -jnp.sin(ac), zero], -1)
    if both_halves:
        parts = [jnp.concatenate([p, p], -1) for p in (cos, sin_dn, sin_up)]
    else:
        one, zz = jnp.ones_like(cos), jnp.zeros_like(cos)
        parts = [jnp.concatenate([cos, one], -1), jnp.concatenate([sin_dn, zz], -1), jnp.concatenate([sin_up, zz], -1)]
    ident = [jnp.ones((seq_s, LANES), F32), jnp.zeros((seq_s, LANES), F32), jnp.zeros((seq_s, LANES), F32)]
    return jnp.stack([jnp.concatenate([i_, p], 0) for i_, p in zip(ident, parts)], 0)


def _even_weights(w_in, w_dec, b_dec, w_uq):
    d = w_in.shape[0]
    sizes = (H_A * DK_A, H_A * DK_A, H_A * DV_A, H_A * DV_A, GLA_LR, GLA_LR, Q_RANK, KV_RANK, ROPE_B)
    offs = np.concatenate([[0], np.cumsum(sizes)])
    qa, ka, va, ga, lrf, lrb, cq, ckv, kr = [w_in[:, offs[j]:offs[j + 1]] for j in range(9)]
    qa = qa * (DK_A ** -0.5)
    qk = jnp.concatenate([qa.reshape(d, H_A, DK_A), ka.reshape(d, H_A, DK_A)], -1).reshape(d, 2 * H_A * DK_A)
    pad = jnp.zeros((d, LANES - ROPE_B - 2 * GLA_LR), w_in.dtype)
    w = jnp.concatenate([qk, va, ga, cq, ckv, kr, lrf, lrb, pad], -1).astype(BF16)
    wd = jnp.zeros((LANES, H_A, 2, DK_A), F32)
    wd = wd.at[ROPE_B:ROPE_B + GLA_LR, :, 0, :].set(w_dec[0].reshape(GLA_LR, H_A, DK_A))
    wd = wd.at[ROPE_B + GLA_LR:ROPE_B + 2 * GLA_LR, :, 1, :].set(w_dec[1].reshape(GLA_LR, H_A, DK_A))
    wd = wd.reshape(LANES, 2 * H_A * DK_A).astype(BF16)
    bd = jnp.stack([b_dec[0].reshape(H_A, DK_A), b_dec[1].reshape(H_A, DK_A)], 1).reshape(1, 2 * H_A * DK_A)
    wq = w_uq.reshape(Q_RANK, H_B, NOPE_B + ROPE_B)
    wq = jnp.concatenate([wq, jnp.zeros((Q_RANK, H_B, 2 * LANES - NOPE_B - ROPE_B), w_uq.dtype)], -1)
    wq = wq.reshape(Q_RANK, H_B * 2 * LANES).astype(BF16)
    return w, wd, bd, wq


def kernel(x_prompt, x_sample, state_gla, cache_mla_ckv, cache_mla_krope, cache_diff_k, cache_diff_v, c, c_ctx, w_ada, b_ada, norm_gain, final_gain, w_in_even, w_out_even, gla_w_decay, gla_b_decay, gla_norm, mla_q_norm, mla_kv_norm, mla_w_uq, mla_w_ukv, w_in_odd, w_out_odd, diff_lambda, diff_norm, ffn_w_up, ffn_conv_w, ffn_conv_b, ffn_w_down):
    nb_p, seq_p, d = x_prompt.shape
    nb_s, seq_s, _ = x_sample.shape
    past = cache_mla_ckv.shape[2]
    depth = w_ada.shape[0]
    d_ff = ffn_w_down.shape[1]
    n_p, n_s = nb_p * seq_p, nb_s * seq_s
    lay = _Layout(n_p, n_s, seq_p, seq_s)
    assert seq_s % GRID_W == 0 and 1 + nb_s <= ADA_ROWS
    assert n_p % seq_s == 0 and seq_p & (seq_p - 1) == 0 and seq_s & (seq_s - 1) == 0
    assert seq_p % GLA_C == 0 and seq_s % GLA_C == 0 and seq_p % TQ == 0 and seq_s % TQ == 0

    x = jnp.concatenate([x_prompt.reshape(n_p, d), x_sample.reshape(n_s, d)], 0)
    cond = jnp.concatenate([c_ctx[None, :], c, jnp.zeros((ADA_ROWS - 1 - nb_s, d), F32)], 0)
    mods = _ada_call(cond, w_ada, b_ada).reshape(depth, ADA_ROWS, 6, d)

    tab_mla = _rope_tables(seq_s, both_halves=False)
    tab_diff = _rope_tables(seq_s, both_halves=True)
    gla_consts = _gla_consts(GLA_C)

    gla_states, mla_ckvs, mla_krs, diff_ks, diff_vs = [], [], [], [], []
    for l in range(depth):
        i = l // 2
        mod = mods[l]
        gain0 = norm_gain[l, 0].reshape(1, d)
        gain1 = norm_gain[l, 1].reshape(1, d)
        if l % 2 == 0:
            w, wd, bd, wq = _even_weights(w_in_even[i], gla_w_decay[i], gla_b_decay[i], mla_w_uq[i])
            wukv = mla_w_ukv[i].astype(BF16)
            qk, la, v, ga, qb, ckv, kr, kra, kv = _even_proj_call(
                lay, x, mod, gain0, w, wd, bd, mla_q_norm[i].reshape(1, -1), mla_kv_norm[i].reshape(1, -1),
                wq, wukv, tab_mla)
            ggain = gla_norm[i].reshape(1, DV_A)
            og_p, st = _gla_call(qk, la, v, ga, ggain, gla_consts, None, 0, nb_p, seq_p)
            og_s, _ = _gla_call(qk, la, v, ga, ggain, gla_consts, state_gla[:, i], n_p, nb_s, seq_s)
            kv_ctx = _kv_expand_call(cache_mla_ckv[:, i].reshape(nb_s * past, KV_RANK), wukv)
            kr_ctx = jnp.pad(cache_mla_krope[:, i].reshape(nb_s * past, ROPE_B), ((0, 0), (0, LANES - ROPE_B))).astype(BF16)
            om_p = _mla_attn_call(qb, [(kv, kra, 0, seq_p)], 0, nb_p, seq_p)
            om_s = _mla_attn_call(qb, [(kv_ctx, kr_ctx, 0, past), (kv, kra, n_p, seq_s)], n_p, nb_s, seq_s)
            wo = w_out_even[i].astype(BF16)
            x = _out_proj_call(lay, x, mod, [(og_p, og_s), (om_p, om_s)], [wo[:H_A * DV_A], wo[H_A * DV_A:]])
            gla_states.append(st)
            mla_ckvs.append(ckv[:n_p].reshape(nb_p, seq_p, KV_RANK))
            mla_krs.append(kr[:n_p].reshape(nb_p, seq_p, ROPE_B))
        else:
            lam_init = 0.8 - 0.6 * math.exp(-0.3 * l)
            n = H_C * 2 * DH_C
            wi = w_in_odd[i]
            wi = jnp.concatenate([wi[:, :n] * DIFF_SCALE, wi[:, n:]], -1).astype(BF16)
            q, kf, kb, vf, vb = _odd_proj_call(lay, x, mod, gain0, wi, tab_diff)
            dgain = diff_norm[i].reshape(1, 2 * DH_C)
            k_ctx = cache_diff_k[:, i].reshape(nb_s * past, n).astype(BF16)
            v_ctx = cache_diff_v[:, i].reshape(nb_s * past, n).astype(BF16)
            od_p = _diff_attn_call(q, diff_lambda[i], dgain, [(kb, vb, 0, seq_p)], 0, nb_p, seq_p, lam_init)
            od_s = _diff_attn_call(q, diff_lambda[i], dgain, [(k_ctx, v_ctx, 0, past), (kb, vb, n_p, seq_s)],
                                   n_p, nb_s, seq_s, lam_init)
            x = _out_proj_call(lay, x, mod, [(od_p, od_s)], [w_out_odd[i].astype(BF16)])
            diff_ks.append(kf[:n_p].reshape(nb_p, seq_p, H_C, 2 * DH_C))
            diff_vs.append(vf[:n_p].reshape(nb_p, seq_p, H_C, 2 * DH_C))
        x = _ffn_call(lay, x, mod, gain1, ffn_w_up[l].astype(BF16), ffn_conv_w[l], ffn_conv_b[l].reshape(1, -1),
                      ffn_w_down[l].astype(BF16), final_gain.reshape(1, d), final=(l == depth - 1))
    y_prompt = x[:n_p].reshape(nb_p, seq_p, d)
    y_sample = x[n_p:].reshape(nb_s, seq_s, d)
    return (y_prompt, y_sample, jnp.stack(gla_states, axis=1), jnp.stack(mla_ckvs, axis=1),
            jnp.stack(mla_krs, axis=1), jnp.stack(diff_ks, axis=1), jnp.stack(diff_vs, axis=1))
```

```python
import functools
import math

import numpy as np
import jax
import jax.numpy as jnp
from jax import lax
from jax.experimental import pallas as pl
from jax.experimental.pallas import tpu as pltpu

GRID_W = 64
ROPE_BASE = 10000.0
EPS = 1e-6
H_A, DK_A, DV_A = 4, 64, 128
GLA_LR = 16
GLA_TAU = 16.0
H_B, Q_RANK, KV_RANK, NOPE_B, ROPE_B, V_B = 4, 256, 256, 128, 64, 128
MLA_SCALE = (NOPE_B + ROPE_B) ** -0.5
H_C, DH_C = 8, 64
DIFF_SCALE = DH_C ** -0.5
LOG2E = math.log2(math.e)

LANES = 128
SUBLANES = 8
VMEM_LIMIT = 56 * 1024 * 1024

TM = 256
TQ = 256
GLA_C = 64
ADA_TN = 1536
FFN_TF = 256
FFN_AHEAD = 2
ADA_ROWS = 16

F32 = jnp.float32
BF16 = jnp.bfloat16


def _dot(a, b):
    return jnp.dot(a, b, preferred_element_type=F32)


def _dot_nt(a, b):
    return lax.dot_general(a, b, (((1,), (1,)), ((), ())), preferred_element_type=F32)


def _silu(x):
    return x * (1.0 / (1.0 + jnp.exp(-x)))


def _rms(x, gain):
    return x * lax.rsqrt(jnp.mean(x * x, axis=-1, keepdims=True) + EPS) * gain


def _normmod(x, gain, shift, scale):
    return _rms(x, gain) * (1.0 + scale) + shift


def _rope(x, cos, sin_dn, sin_up):
    return (x * cos + pltpu.roll(x, 16, axis=1) * sin_dn + pltpu.roll(x, LANES - 16, axis=1) * sin_up)


def _cparams(sem, vmem=VMEM_LIMIT):
    return pltpu.CompilerParams(dimension_semantics=sem, vmem_limit_bytes=vmem)


def _resident(shape):
    nd = len(shape)
    return pl.BlockSpec(shape, lambda *_: (0,) * nd, pipeline_mode=pl.Buffered(1))


def _layer_spec(shape, l):
    nd = len(shape)
    return pl.BlockSpec((1,) + tuple(shape[1:]), lambda *_: (l,) + (0,) * (nd - 1), pipeline_mode=pl.Buffered(1))


def _ada_kernel(c_ref, w_ref, b_ref, o_ref):
    s = _silu(c_ref[...]).astype(BF16)
    o_ref[0] = _dot(s, w_ref[0].astype(BF16)) + b_ref[0]


def _ada_call(cond, w_ada, b_ada):
    depth, d, n = w_ada.shape
    return pl.pallas_call(
        _ada_kernel,
        grid=(depth, n // ADA_TN),
        in_specs=[
            pl.BlockSpec((ADA_ROWS, d), lambda l, j: (0, 0)),
            pl.BlockSpec((1, d, ADA_TN), lambda l, j: (l, 0, j)),
            pl.BlockSpec((1, 1, ADA_TN), lambda l, j: (l, 0, j)),
        ],
        out_specs=pl.BlockSpec((1, ADA_ROWS, ADA_TN), lambda l, j: (l, 0, j)),
        out_shape=jax.ShapeDtypeStruct((depth, ADA_ROWS, n), F32),
        name="ada",
        compiler_params=_cparams(("arbitrary", "arbitrary")),
    )(cond, w_ada, b_ada.reshape(depth, 1, n))


class _Layout:
    def __init__(self, n_prompt, n_sample, seq_p, seq_s):
        assert seq_p == TM and n_sample % TM == 0 and seq_s % TM == 0
        self.np, self.ns, self.nt = n_prompt, n_sample, n_prompt + n_sample
        self.seq_p, self.seq_s = seq_p, seq_s
        self.pt = n_prompt // TM
        self.st = n_sample // TM
        self.tiles = self.pt + self.st
        self.tiles_per_seq_s = seq_s // TM

    def group(self, i):
        return jnp.where(i < self.pt, 0, 1 + (i - self.pt) // self.tiles_per_seq_s)

    def pos_block(self, i):
        return jnp.where(i < self.pt, 0, self.tiles_per_seq_s + (i - self.pt) % self.tiles_per_seq_s)

    def p_block(self, i):
        return jnp.minimum(i, self.pt - 1)

    def s_block(self, i):
        return jnp.maximum(i - self.pt, 0)


def _mod_spec(lay, l, d):
    return pl.BlockSpec((1, 1, 6, d), lambda i: (l, lay.group(i), 0, 0))


def _tab_spec(lay):
    return pl.BlockSpec((3, TM, LANES), lambda i: (0, lay.pos_block(i), 0))


def _x_specs(lay, x):
    if isinstance(x, tuple):
        d = x[0].shape[1]
        return [pl.BlockSpec((TM, d), lambda i: (lay.p_block(i), 0)),
                pl.BlockSpec((TM, d), lambda i: (lay.s_block(i), 0))], list(x)
    return [pl.BlockSpec((TM, x.shape[1]), lambda i: (i, 0))], [x]


def _read_x(x_refs, prompt_tiles):
    if len(x_refs) == 1:
        return x_refs[0][...]
    return jnp.where(pl.program_id(0) < prompt_tiles, x_refs[0][...], x_refs[1][...])


class _Slot:
    def __init__(self, prev, slot, n_slots):
        self.prev, self.slot, self.n_slots = prev, slot, n_slots

    @property
    def first(self):
        return self.prev is None

    def spec(self, tail, index):
        zeros = (0,) * len(tail)
        if self.first:
            return pl.BlockSpec((1, self.n_slots) + tail, lambda *g: (index(*g), 0) + zeros)
        s = self.slot
        return pl.BlockSpec((1, 1) + tail, lambda *g: (index(*g), s) + zeros)

    def shape(self, nb, tail):
        return jax.ShapeDtypeStruct((nb, self.n_slots) + tail, F32)

    def at(self):
        return self.slot if self.first else 0

    def zero_others(self, ref):
        if self.first:
            for s in range(self.n_slots):
                if s != self.slot:
                    ref[0, s] = jnp.zeros(ref.shape[2:], F32)


def _even_proj_kernel(*refs, nx, prompt_tiles, slot):
    x_refs = refs[:nx]
    (mod_ref, gain_ref, w_ref, wd_ref, bd_ref, qn_ref, kvn_ref, wuq_ref, wukv_ref, tab_ref) = refs[nx:nx + 10]
    outs = refs[len(refs) - 9:]
    qk_ref, la_ref, v_ref, ga_ref, qb_ref, kra_ref, kv_ref, ckv_ref, kr_ref = outs
    mod = mod_ref[0, 0]
    h = _normmod(_read_x(x_refs, prompt_tiles), gain_ref[0, 0:1], mod[0:1], mod[1:2]).astype(BF16)
    y = _dot(h, w_ref[...])
    qk_ref[...] = y[:, 0:512]
    v_ref[...] = y[:, 512:1024].astype(BF16)
    ga_ref[...] = y[:, 1024:1536]
    cq, ckv, g = y[:, 1536:1792], y[:, 1792:2048], y[:, 2048:2176]
    pre = _dot(g.astype(BF16), wd_ref[...]) + bd_ref[...]
    la_ref[...] = (jnp.minimum(pre, 0.0) - jnp.log(1.0 + jnp.exp(-jnp.abs(pre)))) * (1.0 / GLA_TAU)
    cos, sin_dn, sin_up = tab_ref[0], tab_ref[1], tab_ref[2]
    qb = _dot(_rms(cq, qn_ref[...]).astype(BF16), wuq_ref[...])
    for hh in range(H_B):
        lo = hh * 2 * LANES
        qb_ref[:, lo:lo + LANES] = qb[:, lo:lo + LANES].astype(BF16)
        qb_ref[:, lo + LANES:lo + 2 * LANES] = _rope(qb[:, lo + LANES:lo + 2 * LANES], cos, sin_dn, sin_up).astype(BF16)
    ckvn = _rms(ckv, kvn_ref[...])
    kv_ref[...] = _dot(ckvn.astype(BF16), wukv_ref[...]).astype(BF16)
    lane = lax.broadcasted_iota(jnp.int32, g.shape, 1)
    kra_ref[...] = jnp.where(lane < ROPE_B, _rope(g, cos, sin_dn, sin_up), 0.0).astype(BF16)

    @pl.when(pl.program_id(0) < prompt_tiles)
    def _():
        ckv_ref[0, slot.at()] = ckvn
        kr_ref[0, slot.at()] = g[:, :ROPE_B]
        slot.zero_others(ckv_ref)
        slot.zero_others(kr_ref)


def _even_proj_call(lay, x, mods, l, gains, w, wd, bd, qn, kvn, wuq, wukv, tab, slot, prev):
    d = w.shape[0]
    x_specs, x_args = _x_specs(lay, x)
    row = lambda n: pl.BlockSpec((TM, n), lambda i: (i, 0))
    outs = [(512, F32), (512, F32), (512, BF16), (512, F32), (1024, BF16), (LANES, BF16), (1024, BF16)]
    in_specs = x_specs + [_mod_spec(lay, l, d), _layer_spec(gains.shape, l), _resident(w.shape), _resident(wd.shape),
                          _resident(bd.shape), _resident(qn.shape), _resident(kvn.shape), _resident(wuq.shape),
                          _resident(wukv.shape), _tab_spec(lay)]
    args = x_args + [mods, gains, w, wd, bd, qn, kvn, wuq, wukv, tab]
    aliases = {}
    if not slot.first:
        aliases = {len(args): len(outs), len(args) + 1: len(outs) + 1}
        in_specs += [pl.BlockSpec(memory_space=pl.ANY), pl.BlockSpec(memory_space=pl.ANY)]
        args += list(prev)
    nb_p = lay.pt
    return pl.pallas_call(
        functools.partial(_even_proj_kernel, nx=len(x_args), prompt_tiles=lay.pt, slot=slot),
        grid=(lay.tiles,),
        in_specs=in_specs,
        out_specs=[row(n) for n, _ in outs] + [slot.spec((TM, KV_RANK), lay.p_block), slot.spec((TM, ROPE_B), lay.p_block)],
        out_shape=[jax.ShapeDtypeStruct((lay.nt, n), dt) for n, dt in outs]
        + [slot.shape(nb_p, (TM, KV_RANK)), slot.shape(nb_p, (TM, ROPE_B))],
        input_output_aliases=aliases,
        name="even_proj",
        compiler_params=_cparams(("arbitrary",)),
    )(*args)


def _gla_consts(c):
    levels = int(math.log2(c))
    idx = np.arange(c)
    cum = idx[None, :] <= idx[:, None]
    rows, sgn, masks = [cum], [], []
    for lev in range(levels):
        half = 1 << lev
        node = idx // (2 * half)
        mid = node * 2 * half + half
        upper = idx >= mid
        rows.append(cum[mid - 1])
        sgn.append(np.repeat(np.where(upper, 1.0, -1.0)[:, None], LANES, 1))
        same = node[:, None] == node[None, :]
        masks.append(np.concatenate([same & upper[:, None] & ~upper[None, :],
                                     same & ~upper[:, None] & upper[None, :]], 1))
    eye = np.eye(c, dtype=bool)
    masks.append(np.concatenate([eye, eye], 1))
    m = np.concatenate(rows, 0)
    return (jnp.asarray(np.concatenate([m, m, m], 1), BF16), jnp.asarray(np.stack(sgn, 0), F32),
            jnp.asarray(np.stack(masks, 0), F32), levels)


def _gla_kernel(*refs, t, has_s0, slot, levels):
    qk_ref, la_ref, v_ref, ga_ref, gain_ref, m3_ref, sg_ref, mask_ref = refs[:8]
    s0_ref = refs[8] if has_s0 else None
    n_out = 1 if slot is None else 2
    o_ref = refs[len(refs) - 3 - n_out]
    st_ref = None if slot is None else refs[len(refs) - 4]
    of_ref, ob_ref, st_scr = refs[len(refs) - 3:]
    c = GLA_C
    n = t // c
    heads = range(H_A)
    lane = lax.broadcasted_iota(jnp.int32, (1, LANES), 1)
    is_fwd = lane < DK_A
    keep_f = jnp.where(is_fwd, 1.0, 0.0).astype(BF16)
    keep_b = jnp.where(is_fwd, 0.0, 1.0).astype(BF16)

    def split(x):
        return jnp.concatenate([x * keep_f, x * keep_b], axis=0)

    for h in heads:
        if has_s0:
            st_scr[h] = jnp.concatenate([jnp.transpose(s0_ref[0, 0, 0, h]), jnp.transpose(s0_ref[0, 0, 1, h])], axis=1)
        else:
            st_scr[h] = jnp.zeros((DV_A, LANES), F32)

    def body(j, carry):
        rf = pl.multiple_of(j * c, c)
        rb = pl.multiple_of((n - 1 - j) * c, c)
        sl = [slice(h * LANES, (h + 1) * LANES) for h in heads]
        qq, kk, e, cx = [], [], [], []
        for h in heads:
            qkf, qkb = qk_ref[pl.ds(rf, c), sl[h]], qk_ref[pl.ds(rb, c), sl[h]]
            qq.append(jnp.where(is_fwd, qkf, pltpu.roll(qkb, DK_A, axis=1)))
            kk.append(jnp.where(is_fwd, pltpu.roll(qkf, DK_A, axis=1), qkb))
            ll = jnp.where(is_fwd, la_ref[pl.ds(rf, c), sl[h]], la_ref[pl.ds(rb, c), sl[h]])
            hi = ll.astype(BF16)
            r1 = ll - hi.astype(F32)
            mid = r1.astype(BF16)
            lo = (r1 - mid.astype(F32)).astype(BF16)
            e.append(_dot(m3_ref[...], jnp.concatenate([hi, mid, lo], axis=0)))
            cx.append(jnp.where(is_fwd, e[h][0:c], e[h][0:c] - ll))
        a = [_dot_nt(qq[h].astype(BF16), split(kk[h].astype(BF16))) * mask_ref[levels] for h in heads]
        for lev in range(levels):
            for h in heads:
                ex = jnp.exp(sg_ref[lev] * (cx[h] - e[h][(lev + 1) * c:(lev + 2) * c]))
                a[h] = a[h] + _dot_nt((qq[h] * ex).astype(BF16), split((kk[h] * ex).astype(BF16))) * mask_ref[lev]
        for h in heads:
            tot = e[h][c - 1:c]
            u = jnp.exp(cx[h])
            w = jnp.exp(tot - cx[h])
            qd = (qq[h] * jnp.where(is_fwd, u, w)).astype(BF16)
            kd = (kk[h] * jnp.where(is_fwd, w, u)).astype(BF16)
            vs = jnp.concatenate([v_ref[pl.ds(rf, c), sl[h]], v_ref[pl.ds(rb, c), sl[h]]], axis=0)
            st = st_scr[h]
            o = _dot(split(a[h].astype(BF16)), vs) + _dot_nt(split(qd), st.astype(BF16))
            of_ref[pl.ds(rf, c), sl[h]] = o[:c]
            ob_ref[pl.ds(rb, c), sl[h]] = o[c:]
            vt = jnp.transpose(vs.astype(F32)).astype(BF16)
            st_scr[h] = st * jnp.exp(tot) + _dot(vt, split(kd))
        return carry

    lax.fori_loop(0, n, body, 0)
    for h in heads:
        sl = slice(h * LANES, (h + 1) * LANES)
        o = _rms(of_ref[:, sl] + ob_ref[:, sl], gain_ref[...]) * _silu(ga_ref[:, sl])
        o_ref[:, sl] = o.astype(BF16)
        if slot is not None:
            st = st_scr[h]
            st_ref[0, slot.at(), 0, h] = jnp.transpose(st[:, :DK_A])
            st_ref[0, slot.at(), 1, h] = jnp.transpose(st[:, DK_A:])
    if slot is not None:
        slot.zero_others(st_ref)


def _gla_call(qk, la, v, ga, gain, consts, s0, s0_slot, row0, nb, t, slot, prev):
    m3, sg, mask, levels = consts
    blk0 = row0 // t
    width = H_A * LANES
    tok = pl.BlockSpec((t, width), lambda b: (blk0 + b, 0))
    in_specs = [tok, tok, tok, tok, _resident(gain.shape), _resident(m3.shape), _resident(sg.shape), _resident(mask.shape)]
    args = [qk, la, v, ga, gain, m3, sg, mask]
    tail = (2, H_A, DK_A, DV_A)
    if s0 is not None:
        in_specs.append(pl.BlockSpec((1, 1) + tail, lambda b: (b, s0_slot, 0, 0, 0, 0)))
        args.append(s0)
    out_specs = [pl.BlockSpec((t, width), lambda b: (b, 0))]
    out_shape = [jax.ShapeDtypeStruct((nb * t, width), BF16)]
    aliases = {}
    if slot is not None:
        out_specs.append(slot.spec(tail, lambda b: b))
        out_shape.append(slot.shape(nb, tail))
        if not slot.first:
            aliases = {len(args): 1}
            in_specs.append(pl.BlockSpec(memory_space=pl.ANY))
            args.append(prev)
    return pl.pallas_call(
        functools.partial(_gla_kernel, t=t, has_s0=s0 is not None, slot=slot, levels=levels),
        grid=(nb,),
        in_specs=in_specs,
        out_specs=out_specs,
        out_shape=out_shape,
        scratch_shapes=[pltpu.VMEM((t, width), F32), pltpu.VMEM((t, width), F32), pltpu.VMEM((H_A, DV_A, LANES), F32)],
        input_output_aliases=aliases,
        name="gla",
        compiler_params=_cparams(("arbitrary",)),
    )(*args)


def _kv_expand_kernel(c_ref, r_ref, w_ref, o_ref, ro_ref):
    o_ref[...] = _dot(c_ref[0, 0].astype(BF16), w_ref[...]).astype(BF16)
    kr = r_ref[0, 0]
    ro_ref[...] = jnp.concatenate([kr, jnp.zeros((kr.shape[0], LANES - ROPE_B), F32)], axis=1).astype(BF16)


def _kv_expand_call(cache_ckv, cache_kr, slot, wukv):
    nb, _, past, r = cache_ckv.shape
    return pl.pallas_call(
        _kv_expand_kernel,
        grid=(nb,),
        in_specs=[pl.BlockSpec((1, 1, past, r), lambda b: (b, slot, 0, 0)),
                  pl.BlockSpec((1, 1, past, ROPE_B), lambda b: (b, slot, 0, 0)), _resident(wukv.shape)],
        out_specs=[pl.BlockSpec((past, wukv.shape[1]), lambda b: (b, 0)), pl.BlockSpec((past, LANES), lambda b: (b, 0))],
        out_shape=[jax.ShapeDtypeStruct((nb * past, wukv.shape[1]), BF16), jax.ShapeDtypeStruct((nb * past, LANES), BF16)],
        name="kv_expand",
        compiler_params=_cparams(("arbitrary",)),
    )(cache_ckv, cache_kr, wukv)


def _mla_attn_kernel(*refs, nseg):
    q_ref = refs[0]
    seg = [(refs[1 + 2 * s], refs[2 + 2 * s]) for s in range(nseg)]
    o_ref = refs[1 + 2 * nseg]
    def qk(hh):
        q = q_ref[:, hh * 2 * LANES:(hh + 1) * 2 * LANES]
        scores = []
        for kv_ref, kr_ref in seg:
            kcat = jnp.concatenate([kv_ref[:, hh * 2 * LANES:hh * 2 * LANES + LANES], kr_ref[...]], axis=1)
            scores.append(_dot_nt(q, kcat) * (MLA_SCALE * LOG2E))
        return scores

    nxt = qk(0)
    for hh in range(H_B):
        scores = nxt
        if hh + 1 < H_B:
            nxt = qk(hh + 1)
        mx = functools.reduce(jnp.maximum, [jnp.max(s, axis=-1, keepdims=True) for s in scores])
        ps = [jnp.exp2(s - mx) for s in scores]
        den = functools.reduce(lambda a, b: a + b, [jnp.sum(p, axis=-1, keepdims=True) for p in ps])
        acc = functools.reduce(lambda a, b: a + b, [
            _dot(p.astype(BF16), kv_ref[:, hh * 2 * LANES + LANES:(hh + 1) * 2 * LANES])
            for p, (kv_ref, _) in zip(ps, seg)])
        o_ref[:, hh * LANES:(hh + 1) * LANES] = (acc * (1.0 / den)).astype(BF16)


def _mla_attn_call(qb, segs, row0, nb, t):
    q0 = row0 // TQ
    per = t // TQ
    in_specs = [pl.BlockSpec((TQ, qb.shape[1]), lambda b, j: (q0 + b * per + j, 0))]
    args = [qb]
    for kv, kr, r0, s in segs:
        b0 = r0 // s
        in_specs.append(pl.BlockSpec((s, kv.shape[1]), lambda b, j, b0=b0: (b0 + b, 0)))
        in_specs.append(pl.BlockSpec((s, LANES), lambda b, j, b0=b0: (b0 + b, 0)))
        args += [kv, kr]
    return pl.pallas_call(
        functools.partial(_mla_attn_kernel, nseg=len(segs)),
        grid=(nb, per),
        in_specs=in_specs,
        out_specs=pl.BlockSpec((TQ, H_B * V_B), lambda b, j: (b * per + j, 0)),
        out_shape=jax.ShapeDtypeStruct((nb * t, H_B * V_B), BF16),
        name="mla_attn",
        compiler_params=_cparams(("arbitrary", "arbitrary")),
    )(*args)


def _odd_proj_kernel(*refs, prompt_tiles, slot):
    x_ref, mod_ref, gain_ref, w_ref, tab_ref = refs[:5]
    q_ref, kb_ref, vb_ref, kf_ref, vf_ref = refs[len(refs) - 5:]
    mod = mod_ref[0, 0]
    h = _normmod(x_ref[...], gain_ref[0, 0:1], mod[0:1], mod[1:2]).astype(BF16)
    y = _dot(h, w_ref[...])
    n = q_ref.shape[1]
    cos, sin_dn, sin_up = tab_ref[0], tab_ref[1], tab_ref[2]
    vb_ref[...] = y[:, 2 * n:3 * n].astype(BF16)
    for g in range(n // LANES):
        lo = g * LANES
        q_ref[:, lo:lo + LANES] = _rope(y[:, lo:lo + LANES], cos, sin_dn, sin_up).astype(BF16)
        kb_ref[:, lo:lo + LANES] = _rope(y[:, n + lo:n + lo + LANES], cos, sin_dn, sin_up).astype(BF16)

    @pl.when(pl.program_id(0) < prompt_tiles)
    def _():
        for g in range(n // LANES):
            lo = g * LANES
            kf_ref[0, slot.at(), :, g, :] = y[:, n + lo:n + lo + LANES]
            vf_ref[0, slot.at(), :, g, :] = y[:, 2 * n + lo:2 * n + lo + LANES]
        slot.zero_others(kf_ref)
        slot.zero_others(vf_ref)


def _odd_proj_call(lay, x, mods, l, gains, w, tab, slot, prev):
    d = x.shape[1]
    n = w.shape[1] // 3
    row = lambda m: pl.BlockSpec((TM, m), lambda i: (i, 0))
    in_specs = [row(d), _mod_spec(lay, l, d), _layer_spec(gains.shape, l), _resident(w.shape), _tab_spec(lay)]
    args = [x, mods, gains, w, tab]
    aliases = {}
    if not slot.first:
        aliases = {len(args): 3, len(args) + 1: 4}
        in_specs += [pl.BlockSpec(memory_space=pl.ANY), pl.BlockSpec(memory_space=pl.ANY)]
        args += list(prev)
    tail = (TM, H_C, 2 * DH_C)
    return pl.pallas_call(
        functools.partial(_odd_proj_kernel, prompt_tiles=lay.pt, slot=slot),
        grid=(lay.tiles,),
        in_specs=in_specs,
        out_specs=[row(n), row(n), row(n), slot.spec(tail, lay.p_block), slot.spec(tail, lay.p_block)],
        out_shape=[jax.ShapeDtypeStruct((lay.nt, n), BF16)] * 3 + [slot.shape(lay.pt, tail)] * 2,
        input_output_aliases=aliases,
        name="odd_proj",
        compiler_params=_cparams(("arbitrary",)),
    )(*args)


def _diff_attn_kernel(*refs, nseg, lam_init):
    q_ref, dl_ref, gain_ref = refs[0], refs[1], refs[2]
    seg = [(refs[3 + 2 * s], refs[4 + 2 * s]) for s in range(nseg)]
    o_ref = refs[3 + 2 * nseg]
    dl = dl_ref[0]
    lam = (jnp.exp(jnp.sum(dl[0:1] * dl[1:2], axis=-1, keepdims=True))
           - jnp.exp(jnp.sum(dl[2:3] * dl[3:4], axis=-1, keepdims=True)) + lam_init)
    tq = q_ref.shape[0]
    lane = lax.broadcasted_iota(jnp.int32, (1, LANES), 1)
    keep1 = jnp.where(lane < DH_C, 1.0, 0.0).astype(BF16)
    keep2 = jnp.where(lane < DH_C, 0.0, 1.0).astype(BF16)
    def qk(hh):
        q = q_ref[:, hh * LANES:(hh + 1) * LANES]
        qq = jnp.concatenate([q * keep1, q * keep2], axis=0)
        return [_dot_nt(qq, k_ref[:, hh * LANES:(hh + 1) * LANES]) for k_ref, _ in seg]

    nxt = qk(0)
    for hh in range(H_C):
        scores = nxt
        if hh + 1 < H_C:
            nxt = qk(hh + 1)
        mx = functools.reduce(jnp.maximum, [jnp.max(s, axis=-1, keepdims=True) for s in scores])
        ps = [jnp.exp2(s - mx) for s in scores]
        den = functools.reduce(lambda a, b: a + b, [jnp.sum(p, axis=-1, keepdims=True) for p in ps])
        acc = functools.reduce(lambda a, b: a + b, [
            _dot(p.astype(BF16), v_ref[:, hh * LANES:(hh + 1) * LANES]) for p, (_, v_ref) in zip(ps, seg)])
        acc = acc * (1.0 / den)
        o = acc[:tq] - lam * acc[tq:]
        o_ref[:, hh * LANES:(hh + 1) * LANES] = (_rms(o, gain_ref[0]) * (1.0 - lam_init)).astype(BF16)


def _diff_attn_call(q, dls, gains, slot, segs, row0, nb, t, lam_init):
    q0 = row0 // TQ
    per = t // TQ
    n = q.shape[1]
    in_specs = [pl.BlockSpec((TQ, n), lambda b, j: (q0 + b * per + j, 0)),
                pl.BlockSpec((1,) + dls.shape[1:], lambda b, j: (slot, 0, 0)),
                pl.BlockSpec((1, 1, gains.shape[2]), lambda b, j: (slot, 0, 0))]
    args = [q, dls, gains]
    for k, v, r0, s in segs:
        b0 = r0 // s
        in_specs.append(pl.BlockSpec((s, n), lambda b, j, b0=b0: (b0 + b, 0)))
        in_specs.append(pl.BlockSpec((s, n), lambda b, j, b0=b0: (b0 + b, 0)))
        args += [k, v]
    return pl.pallas_call(
        functools.partial(_diff_attn_kernel, nseg=len(segs), lam_init=lam_init),
        grid=(nb, per),
        in_specs=in_specs,
        out_specs=pl.BlockSpec((TQ, n), lambda b, j: (b * per + j, 0)),
        out_shape=jax.ShapeDtypeStruct((nb * t, n), BF16),
        name="diff_attn",
        compiler_params=_cparams(("arbitrary", "arbitrary")),
    )(*args)


def _out_proj_kernel(*refs, nx, npairs, prompt_tiles):
    x_refs = refs[:nx]
    mod_ref = refs[nx]
    mix = refs[nx + 1:nx + 1 + 2 * npairs]
    w_ref = refs[nx + 1 + 2 * npairs]
    o_ref = refs[nx + 2 + 2 * npairs]
    is_prompt = pl.program_id(0) < prompt_tiles
    acc = None
    k0 = 0
    for p in range(npairs):
        a = jnp.where(is_prompt, mix[2 * p][...], mix[2 * p + 1][...])
        part = _dot(a, w_ref[0, k0:k0 + a.shape[1], :])
        k0 += a.shape[1]
        acc = part if acc is None else acc + part
    o_ref[...] = _read_x(x_refs, prompt_tiles) + mod_ref[0, 0][2:3] * acc


def _out_proj_call(lay, x, mods, l, pairs, ws, slot):
    d = ws.shape[2]
    x_specs, x_args = _x_specs(lay, x)
    in_specs = x_specs + [_mod_spec(lay, l, d)]
    args = x_args + [mods]
    for a_p, a_s in pairs:
        in_specs.append(pl.BlockSpec((TM, a_p.shape[1]), lambda i: (lay.p_block(i), 0)))
        in_specs.append(pl.BlockSpec((TM, a_s.shape[1]), lambda i: (lay.s_block(i), 0)))
        args += [a_p, a_s]
    in_specs.append(_layer_spec(ws.shape, slot))
    args.append(ws)
    return pl.pallas_call(
        functools.partial(_out_proj_kernel, nx=len(x_args), npairs=len(pairs), prompt_tiles=lay.pt),
        name="out_proj",
        grid=(lay.tiles,),
        in_specs=in_specs,
        out_specs=pl.BlockSpec((TM, d), lambda i: (i, 0)),
        out_shape=jax.ShapeDtypeStruct((lay.nt, d), F32),
        compiler_params=_cparams(("arbitrary",)),
    )(*args)


def _ffn_kernel(x_ref, xp_ref, xn_ref, mod_ref, gain_ref, wup_ref, cw_ref, cb_ref, wdn_ref, fg_ref, *rest,
                lay, d_ff, final):
    u_ref = rest[-1]
    outs = rest[:-1]
    i = pl.program_id(0)
    row = i * TM
    seq = jnp.where(i < lay.pt, lay.seq_p, lay.seq_s)
    first = (row & (seq - 1)) == 0
    last = ((row + TM) & (seq - 1)) == 0
    mod = mod_ref[0, 0]
    gain, shift, scale = gain_ref[0, 1:2], mod[3:4], mod[4:5]
    x = x_ref[...]
    hp = jnp.where(first, 0.0, _normmod(xp_ref[...], gain, shift, scale))
    hn = jnp.where(last, 0.0, _normmod(xn_ref[...], gain, shift, scale))
    h = jnp.concatenate([hp, _normmod(x, gain, shift, scale), hn], axis=0).astype(BF16)
    acc = jnp.zeros((TM, x.shape[1]), F32)
    n_f = d_ff // FFN_TF

    def up(f):
        for base in (0, d_ff):
            lo = base + f * FFN_TF
            u_ref[:, lo:lo + FFN_TF] = _dot(h, wup_ref[0, :, lo:lo + FFN_TF])

    def conv(lo):
        cw = cw_ref[0, :, lo:lo + FFN_TF]
        return (u_ref[SUBLANES - 1:SUBLANES - 1 + TM, lo:lo + FFN_TF] * cw[0:1]
                + u_ref[SUBLANES:SUBLANES + TM, lo:lo + FFN_TF] * cw[1:2]
                + u_ref[SUBLANES + 1:SUBLANES + 1 + TM, lo:lo + FFN_TF] * cw[2:3] + cb_ref[0, :, lo:lo + FFN_TF])

    for f in range(min(FFN_AHEAD, n_f)):
        up(f)
    for f in range(n_f):
        if f + FFN_AHEAD < n_f:
            up(f + FFN_AHEAD)
        act = (_silu(conv(f * FFN_TF)) * conv(d_ff + f * FFN_TF)).astype(BF16)
        acc = acc + _dot(act, wdn_ref[0, f * FFN_TF:(f + 1) * FFN_TF, :])
    y = x + mod[5:6] * acc
    if not final:
        outs[0][...] = y
    else:
        yn = _rms(y, fg_ref[...])

        @pl.when(i < lay.pt)
        def _():
            outs[0][...] = yn

        @pl.when(i >= lay.pt)
        def _():
            outs[1][...] = yn


def _ffn_call(lay, x, mods, l, gains, wup, cw, cb, wdn, fgain, final):
    d = x.shape[1]
    d_ff = wdn.shape[1]
    assert d_ff % FFN_TF == 0
    halo = TM // SUBLANES
    last_blk = lay.nt // SUBLANES - 1
    if final:
        out_specs = [pl.BlockSpec((TM, d), lambda i: (lay.p_block(i), 0)), pl.BlockSpec((TM, d), lambda i: (lay.s_block(i), 0))]
        out_shape = [jax.ShapeDtypeStruct((lay.np, d), F32), jax.ShapeDtypeStruct((lay.ns, d), F32)]
    else:
        out_specs = [pl.BlockSpec((TM, d), lambda i: (i, 0))]
        out_shape = [jax.ShapeDtypeStruct((lay.nt, d), F32)]
    return pl.pallas_call(
        functools.partial(_ffn_kernel, lay=lay, d_ff=d_ff, final=final),
        name="ffn",
        grid=(lay.tiles,),
        in_specs=[
            pl.BlockSpec((TM, d), lambda i: (i, 0)),
            pl.BlockSpec((SUBLANES, d), lambda i: (jnp.maximum(i * halo - 1, 0), 0)),
            pl.BlockSpec((SUBLANES, d), lambda i: (jnp.minimum((i + 1) * halo, last_blk), 0)),
            _mod_spec(lay, l, d), _layer_spec(gains.shape, l), _layer_spec(wup.shape, l), _layer_spec(cw.shape, l),
            _layer_spec(cb.shape, l), _layer_spec(wdn.shape, l), _resident((1, d)),
        ],
        out_specs=out_specs,
        out_shape=out_shape,
        scratch_shapes=[pltpu.VMEM((TM + 2 * SUBLANES, 2 * d_ff), F32)],
        compiler_params=_cparams(("arbitrary",)),
    )(x, x, x, mods, gains, wup, cw, cb, wdn, fgain)


def _rope_tables(seq_s, both_halves):
    t = jnp.arange(seq_s)
    rowp = (t // GRID_W).astype(F32)
    colp = (t % GRID_W).astype(F32)
    half = ROPE_B // 2
    inv = ROPE_BASE ** (-jnp.arange(0, half, 2, dtype=F32) / half)
    ar, ac = rowp[:, None] * inv, colp[:, None] * inv
    zero = jnp.zeros_like(ar)
    cos = jnp.concatenate([jnp.cos(ar), jnp.cos(ar), jnp.cos(ac), jnp.cos(ac)], -1)
    sin_dn = jnp.concatenate([zero, jnp.sin(ar), zero, jnp.sin(ac)], -1)
    sin_up = jnp.concatenate([-jnp.sin(ar), zero, -jnp.sin(ac), zero], -1)
    if both_halves:
        parts = [jnp.concatenate([p, p], -1) for p in (cos, sin_dn, sin_up)]
    else:
        one, zz = jnp.ones_like(cos), jnp.zeros_like(cos)
        parts = [jnp.concatenate([cos, one], -1), jnp.concatenate([sin_dn, zz], -1), jnp.concatenate([sin_up, zz], -1)]
    ident = [jnp.ones((seq_s, LANES), F32), jnp.zeros((seq_s, LANES), F32), jnp.zeros((seq_s, LANES), F32)]
    return jnp.stack([jnp.concatenate([i_, p], 0) for i_, p in zip(ident, parts)], 0)


def _even_weights(w_in, w_dec, b_dec, w_uq):
    d = w_in.shape[0]
    sizes = (H_A * DK_A, H_A * DK_A, H_A * DV_A, H_A * DV_A, GLA_LR, GLA_LR, Q_RANK, KV_RANK, ROPE_B)
    offs = np.concatenate([[0], np.cumsum(sizes)])
    qa, ka, va, ga, lrf, lrb, cq, ckv, kr = [w_in[:, offs[j]:offs[j + 1]] for j in range(9)]
    qa = qa * (DK_A ** -0.5)
    qk = jnp.concatenate([qa.reshape(d, H_A, DK_A), ka.reshape(d, H_A, DK_A)], -1).reshape(d, 2 * H_A * DK_A)
    pad = jnp.zeros((d, LANES - ROPE_B - 2 * GLA_LR), w_in.dtype)
    w = jnp.concatenate([qk, va, ga, cq, ckv, kr, lrf, lrb, pad], -1).astype(BF16)
    wd = jnp.zeros((LANES, H_A, 2, DK_A), F32)
    wd = wd.at[ROPE_B:ROPE_B + GLA_LR, :, 0, :].set(w_dec[0].reshape(GLA_LR, H_A, DK_A))
    wd = wd.at[ROPE_B + GLA_LR:ROPE_B + 2 * GLA_LR, :, 1, :].set(w_dec[1].reshape(GLA_LR, H_A, DK_A))
    wd = wd.reshape(LANES, 2 * H_A * DK_A).astype(BF16)
    bd = jnp.stack([b_dec[0].reshape(H_A, DK_A), b_dec[1].reshape(H_A, DK_A)], 1).reshape(1, 2 * H_A * DK_A)
    wq = w_uq.reshape(Q_RANK, H_B, NOPE_B + ROPE_B)
    wq = jnp.concatenate([wq, jnp.zeros((Q_RANK, H_B, 2 * LANES - NOPE_B - ROPE_B), w_uq.dtype)], -1)
    wq = wq.reshape(Q_RANK, H_B * 2 * LANES).astype(BF16)
    return w, wd, bd, wq


def kernel(x_prompt, x_sample, state_gla, cache_mla_ckv, cache_mla_krope, cache_diff_k, cache_diff_v, c, c_ctx, w_ada, b_ada, norm_gain, final_gain, w_in_even, w_out_even, gla_w_decay, gla_b_decay, gla_norm, mla_q_norm, mla_kv_norm, mla_w_uq, mla_w_ukv, w_in_odd, w_out_odd, diff_lambda, diff_norm, ffn_w_up, ffn_conv_w, ffn_conv_b, ffn_w_down):
    nb_p, seq_p, d = x_prompt.shape
    nb_s, seq_s, _ = x_sample.shape
    past = cache_mla_ckv.shape[2]
    depth = w_ada.shape[0]
    n_even, n_odd = w_in_even.shape[0], w_in_odd.shape[0]
    n_p, n_s = nb_p * seq_p, nb_s * seq_s
    lay = _Layout(n_p, n_s, seq_p, seq_s)
    assert seq_s % GRID_W == 0 and 1 + nb_s <= ADA_ROWS
    assert n_p % seq_s == 0 and seq_p & (seq_p - 1) == 0 and seq_s & (seq_s - 1) == 0
    assert seq_p % GLA_C == 0 and seq_s % GLA_C == 0 and seq_p % TQ == 0 and seq_s % TQ == 0

    cond = jnp.concatenate([c_ctx[None, :], c, jnp.zeros((ADA_ROWS - 1 - nb_s, d), F32)], 0)
    mods = _ada_call(cond, w_ada, b_ada).reshape(depth, ADA_ROWS, 6, d)

    tab_mla = _rope_tables(seq_s, both_halves=False)
    tab_diff = _rope_tables(seq_s, both_halves=True)
    gla_consts = _gla_consts(GLA_C)
    wup_all = ffn_w_up.astype(BF16)
    wdn_all = ffn_w_down.astype(BF16)
    cb_all = ffn_conv_b.reshape(depth, 1, -1)
    wo_even = w_out_even.astype(BF16)
    wo_odd = w_out_odd.astype(BF16)
    fgain = final_gain.reshape(1, d)
    dgains = diff_norm.reshape(n_odd, 1, 2 * DH_C)

    x = (x_prompt.reshape(n_p, d), x_sample.reshape(n_s, d))
    st_new = caches_even = caches_odd = None
    for l in range(depth):
        i = l // 2
        if l % 2 == 0:
            slot = _Slot(caches_even, i, n_even)
            w, wd, bd, wq = _even_weights(w_in_even[i], gla_w_decay[i], gla_b_decay[i], mla_w_uq[i])
            wukv = mla_w_ukv[i].astype(BF16)
            qk, la, v, ga, qb, kra, kv, ckv_new, kr_new = _even_proj_call(
                lay, x, mods, l, norm_gain, w, wd, bd, mla_q_norm[i].reshape(1, -1), mla_kv_norm[i].reshape(1, -1),
                wq, wukv, tab_mla, slot, caches_even)
            caches_even = (ckv_new, kr_new)
            ggain = gla_norm[i].reshape(1, DV_A)
            og_p, st_new = _gla_call(qk, la, v, ga, ggain, gla_consts, None, 0, 0, nb_p, seq_p,
                                     _Slot(st_new, i, n_even), st_new)
            og_s, = _gla_call(qk, la, v, ga, ggain, gla_consts, state_gla, i, n_p, nb_s, seq_s, None, None)
            kv_ctx, kr_ctx = _kv_expand_call(cache_mla_ckv, cache_mla_krope, i, wukv)
            om_p = _mla_attn_call(qb, [(kv, kra, 0, seq_p)], 0, nb_p, seq_p)
            om_s = _mla_attn_call(qb, [(kv_ctx, kr_ctx, 0, past), (kv, kra, n_p, seq_s)], n_p, nb_s, seq_s)
            x = _out_proj_call(lay, x, mods, l, [(og_p, og_s), (om_p, om_s)], wo_even, i)
        else:
            slot = _Slot(caches_odd, i, n_odd)
            lam_init = 0.8 - 0.6 * math.exp(-0.3 * l)
            n = H_C * 2 * DH_C
            wi = w_in_odd[i]
            wi = jnp.concatenate([wi[:, :n] * (DIFF_SCALE * LOG2E), wi[:, n:]], -1).astype(BF16)
            q, kb, vb, kf_new, vf_new = _odd_proj_call(lay, x, mods, l, norm_gain, wi, tab_diff, slot, caches_odd)
            caches_odd = (kf_new, vf_new)
            k_ctx = cache_diff_k[:, i].reshape(nb_s * past, n).astype(BF16)
            v_ctx = cache_diff_v[:, i].reshape(nb_s * past, n).astype(BF16)
            od_p = _diff_attn_call(q, diff_lambda, dgains, i, [(kb, vb, 0, seq_p)], 0, nb_p, seq_p, lam_init)
            od_s = _diff_attn_call(q, diff_lambda, dgains, i, [(k_ctx, v_ctx, 0, past), (kb, vb, n_p, seq_s)],
                                   n_p, nb_s, seq_s, lam_init)
            x = _out_proj_call(lay, x, mods, l, [(od_p, od_s)], wo_odd, i)
        x = _ffn_call(lay, x, mods, l, norm_gain, wup_all, ffn_conv_w, cb_all, wdn_all, fgain, final=(l == depth - 1))
        if l < depth - 1:
            x = x[0]
    y_prompt = x[0].reshape(nb_p, seq_p, d)
    y_sample = x[1].reshape(nb_s, seq_s, d)
    return (y_prompt, y_sample, st_new, caches_even[0], caches_even[1], caches_odd[0], caches_odd[1])
```

```python
import functools
import math

import numpy as np
import jax
import jax.numpy as jnp
from jax import lax
from jax.experimental import pallas as pl
from jax.experimental.pallas import tpu as pltpu

GRID_W = 64
ROPE_BASE = 10000.0
EPS = 1e-6
H_A, DK_A, DV_A = 4, 64, 128
GLA_LR = 16
GLA_TAU = 16.0
H_B, Q_RANK, KV_RANK, NOPE_B, ROPE_B, V_B = 4, 256, 256, 128, 64, 128
MLA_SCALE = (NOPE_B + ROPE_B) ** -0.5
H_C, DH_C = 8, 64
DIFF_SCALE = DH_C ** -0.5
LOG2E = math.log2(math.e)

LANES = 128
SUBLANES = 8
HALO = 16
VMEM_LIMIT = 56 * 1024 * 1024

TM = 256
TQ = 256
GLA_C = 64
ADA_TN = 1536
FFN_TM = 512
FFN_TF = 256
FFN_AHEAD = 2
ADA_ROWS = 16

F32 = jnp.float32
BF16 = jnp.bfloat16


def _dot(a, b):
    return jnp.dot(a, b, preferred_element_type=F32)


def _dot_nt(a, b):
    return lax.dot_general(a, b, (((1,), (1,)), ((), ())), preferred_element_type=F32)


def _silu(x):
    return x * (1.0 / (1.0 + jnp.exp(-x)))


def _rms(x, gain):
    return x * lax.rsqrt(jnp.mean(x * x, axis=-1, keepdims=True) + EPS) * gain


def _normmod(x, gain, shift, scale):
    return _rms(x, gain) * (1.0 + scale) + shift


def _rope(x, cos, sin_dn, sin_up):
    return (x * cos + pltpu.roll(x, 16, axis=1) * sin_dn + pltpu.roll(x, LANES - 16, axis=1) * sin_up)


def _cparams(sem, vmem=VMEM_LIMIT):
    return pltpu.CompilerParams(dimension_semantics=sem, vmem_limit_bytes=vmem)


def _resident(shape):
    nd = len(shape)
    return pl.BlockSpec(shape, lambda *_: (0,) * nd, pipeline_mode=pl.Buffered(1))


def _layer_spec(shape, l):
    nd = len(shape)
    return pl.BlockSpec((1,) + tuple(shape[1:]), lambda *_: (l,) + (0,) * (nd - 1), pipeline_mode=pl.Buffered(1))


def _ada_kernel(c_ref, w_ref, b_ref, o_ref):
    s = _silu(c_ref[...]).astype(BF16)
    o_ref[0] = _dot(s, w_ref[0].astype(BF16)) + b_ref[0]


def _ada_call(cond, w_ada, b_ada):
    depth, d, n = w_ada.shape
    return pl.pallas_call(
        _ada_kernel,
        grid=(depth, n // ADA_TN),
        in_specs=[
            pl.BlockSpec((ADA_ROWS, d), lambda l, j: (0, 0)),
            pl.BlockSpec((1, d, ADA_TN), lambda l, j: (l, 0, j)),
            pl.BlockSpec((1, 1, ADA_TN), lambda l, j: (l, 0, j)),
        ],
        out_specs=pl.BlockSpec((1, ADA_ROWS, ADA_TN), lambda l, j: (l, 0, j)),
        out_shape=jax.ShapeDtypeStruct((depth, ADA_ROWS, n), F32),
        name="ada",
        compiler_params=_cparams(("arbitrary", "arbitrary")),
    )(cond, w_ada, b_ada.reshape(depth, 1, n))


class _Layout:
    def __init__(self, n_prompt, n_sample, seq_p, seq_s, tm):
        assert tm % seq_p == 0 and n_prompt % tm == 0 and n_sample % tm == 0 and seq_s % tm == 0
        self.tm = tm
        self.np, self.ns, self.nt = n_prompt, n_sample, n_prompt + n_sample
        self.seq_p, self.seq_s = seq_p, seq_s
        self.pt = n_prompt // tm
        self.st = n_sample // tm
        self.tiles = self.pt + self.st
        self.tiles_per_seq_s = seq_s // tm

    def group(self, i):
        return jnp.where(i < self.pt, 0, 1 + (i - self.pt) // self.tiles_per_seq_s)

    def pos_block(self, i):
        return jnp.where(i < self.pt, 0, self.tiles_per_seq_s + (i - self.pt) % self.tiles_per_seq_s)

    def p_block(self, i):
        return jnp.minimum(i, self.pt - 1)

    def s_block(self, i):
        return jnp.maximum(i - self.pt, 0)


def _mod_spec(lay, l, d):
    return pl.BlockSpec((1, 1, 6, d), lambda i: (l, lay.group(i), 0, 0))


def _tab_spec(lay):
    return pl.BlockSpec((3, lay.tm, LANES), lambda i: (0, lay.pos_block(i), 0))


def _x_specs(lay, x):
    if isinstance(x, tuple):
        d = x[0].shape[1]
        return [pl.BlockSpec((lay.tm, d), lambda i: (lay.p_block(i), 0)),
                pl.BlockSpec((lay.tm, d), lambda i: (lay.s_block(i), 0))], list(x)
    return [pl.BlockSpec((lay.tm, x.shape[1]), lambda i: (i, 0))], [x]


def _read_x(x_refs, prompt_tiles):
    if len(x_refs) == 1:
        return x_refs[0][...]
    return jnp.where(pl.program_id(0) < prompt_tiles, x_refs[0][...], x_refs[1][...])


class _Slot:
    def __init__(self, prev, slot, n_slots):
        self.prev, self.slot, self.n_slots = prev, slot, n_slots

    @property
    def first(self):
        return self.prev is None

    def spec(self, tail, index):
        zeros = (0,) * len(tail)
        if self.first:
            return pl.BlockSpec((1, self.n_slots) + tail, lambda *g: (index(*g), 0) + zeros)
        s = self.slot
        return pl.BlockSpec((1, 1) + tail, lambda *g: (index(*g), s) + zeros)

    def shape(self, nb, tail):
        return jax.ShapeDtypeStruct((nb, self.n_slots) + tail, F32)

    def at(self):
        return self.slot if self.first else 0

    def zero_others(self, ref):
        if self.first:
            for s in range(self.n_slots):
                if s != self.slot:
                    ref[0, s] = jnp.zeros(ref.shape[2:], F32)


def _even_proj_kernel(*refs, nx, prompt_tiles, slot):
    x_refs = refs[:nx]
    (mod_ref, gain_ref, w_ref, wd_ref, bd_ref, qn_ref, kvn_ref, wuq_ref, wukv_ref, tab_ref) = refs[nx:nx + 10]
    outs = refs[len(refs) - 9:]
    qk_ref, la_ref, v_ref, ga_ref, qb_ref, kra_ref, kv_ref, ckv_ref, kr_ref = outs
    mod = mod_ref[0, 0]
    h = _normmod(_read_x(x_refs, prompt_tiles), gain_ref[0, 0:1], mod[0:1], mod[1:2]).astype(BF16)
    y = _dot(h, w_ref[...])
    qk_ref[...] = y[:, 0:512]
    v_ref[...] = y[:, 512:1024].astype(BF16)
    ga_ref[...] = y[:, 1024:1536]
    cq, ckv, g = y[:, 1536:1792], y[:, 1792:2048], y[:, 2048:2176]
    pre = _dot(g.astype(BF16), wd_ref[...]) + bd_ref[...]
    la_ref[...] = (jnp.minimum(pre, 0.0) - jnp.log(1.0 + jnp.exp(-jnp.abs(pre)))) * (LOG2E / GLA_TAU)
    cos, sin_dn, sin_up = tab_ref[0], tab_ref[1], tab_ref[2]
    qb = _dot(_rms(cq, qn_ref[...]).astype(BF16), wuq_ref[...])
    for hh in range(H_B):
        lo = hh * 2 * LANES
        qb_ref[:, lo:lo + LANES] = qb[:, lo:lo + LANES].astype(BF16)
        qb_ref[:, lo + LANES:lo + 2 * LANES] = _rope(qb[:, lo + LANES:lo + 2 * LANES], cos, sin_dn, sin_up).astype(BF16)
    ckvn = _rms(ckv, kvn_ref[...])
    kv_ref[...] = _dot(ckvn.astype(BF16), wukv_ref[...]).astype(BF16)
    lane = lax.broadcasted_iota(jnp.int32, g.shape, 1)
    kra_ref[...] = jnp.where(lane < ROPE_B, _rope(g, cos, sin_dn, sin_up), 0.0).astype(BF16)

    @pl.when(pl.program_id(0) < prompt_tiles)
    def _():
        ckv_ref[0, slot.at()] = ckvn
        kr_ref[0, slot.at()] = g[:, :ROPE_B]
        slot.zero_others(ckv_ref)
        slot.zero_others(kr_ref)


def _even_proj_call(lay, x, mods, l, gains, w, wd, bd, qn, kvn, wuq, wukv, tab, slot, prev):
    d = w.shape[0]
    x_specs, x_args = _x_specs(lay, x)
    row = lambda n: pl.BlockSpec((TM, n), lambda i: (i, 0))
    outs = [(512, F32), (512, F32), (512, BF16), (512, F32), (1024, BF16), (LANES, BF16), (1024, BF16)]
    in_specs = x_specs + [_mod_spec(lay, l, d), _layer_spec(gains.shape, l), _resident(w.shape), _resident(wd.shape),
                          _resident(bd.shape), _resident(qn.shape), _resident(kvn.shape), _resident(wuq.shape),
                          _resident(wukv.shape), _tab_spec(lay)]
    args = x_args + [mods, gains, w, wd, bd, qn, kvn, wuq, wukv, tab]
    aliases = {}
    if not slot.first:
        aliases = {len(args): len(outs), len(args) + 1: len(outs) + 1}
        in_specs += [pl.BlockSpec(memory_space=pl.ANY), pl.BlockSpec(memory_space=pl.ANY)]
        args += list(prev)
    nb_p = lay.pt
    return pl.pallas_call(
        functools.partial(_even_proj_kernel, nx=len(x_args), prompt_tiles=lay.pt, slot=slot),
        grid=(lay.tiles,),
        in_specs=in_specs,
        out_specs=[row(n) for n, _ in outs] + [slot.spec((TM, KV_RANK), lay.p_block), slot.spec((TM, ROPE_B), lay.p_block)],
        out_shape=[jax.ShapeDtypeStruct((lay.nt, n), dt) for n, dt in outs]
        + [slot.shape(nb_p, (TM, KV_RANK)), slot.shape(nb_p, (TM, ROPE_B))],
        input_output_aliases=aliases,
        name="even_proj",
        compiler_params=_cparams(("arbitrary",)),
    )(*args)


def _gla_consts(c):
    levels = int(math.log2(c))
    idx = np.arange(c)
    cum = idx[None, :] <= idx[:, None]
    rows, sgn, masks = [cum], [], []
    for lev in range(levels):
        half = 1 << lev
        node = idx // (2 * half)
        mid = node * 2 * half + half
        upper = idx >= mid
        rows.append(cum[mid - 1])
        sgn.append(np.repeat(np.where(upper, 1.0, -1.0)[:, None], LANES, 1))
        same = node[:, None] == node[None, :]
        masks.append(np.concatenate([same & upper[:, None] & ~upper[None, :],
                                     same & ~upper[:, None] & upper[None, :]], 1))
    eye = np.eye(c, dtype=bool)
    masks.append(np.concatenate([eye, eye], 1))
    m = np.concatenate(rows, 0)
    return (jnp.asarray(np.concatenate([m, m, m], 1), BF16), jnp.asarray(np.stack(sgn, 0), F32),
            jnp.asarray(np.stack(masks, 0), F32), levels)


def _gla_kernel(*refs, t, has_s0, slot, levels):
    qk_ref, la_ref, v_ref, ga_ref, gain_ref, m3_ref, sg_ref, mask_ref = refs[:8]
    s0_ref = refs[8] if has_s0 else None
    n_out = 1 if slot is None else 2
    o_ref = refs[len(refs) - 3 - n_out]
    st_ref = None if slot is None else refs[len(refs) - 4]
    of_ref, ob_ref, st_scr = refs[len(refs) - 3:]
    c = GLA_C
    n = t // c
    heads = range(H_A)
    lane = lax.broadcasted_iota(jnp.int32, (1, LANES), 1)
    is_fwd = lane < DK_A
    keep_f = jnp.where(is_fwd, 1.0, 0.0).astype(BF16)
    keep_b = jnp.where(is_fwd, 0.0, 1.0).astype(BF16)

    def split(x):
        return jnp.concatenate([x * keep_f, x * keep_b], axis=0)

    for h in heads:
        if has_s0:
            st_scr[h] = jnp.concatenate([jnp.transpose(s0_ref[0, 0, 0, h]), jnp.transpose(s0_ref[0, 0, 1, h])], axis=1)
        else:
            st_scr[h] = jnp.zeros((DV_A, LANES), F32)

    def body(j, carry):
        rf = pl.multiple_of(j * c, c)
        rb = pl.multiple_of((n - 1 - j) * c, c)
        sl = [slice(h * LANES, (h + 1) * LANES) for h in heads]
        qq, kk, ll, split3 = [], [], [], []
        for h in heads:
            qkf, qkb = qk_ref[pl.ds(rf, c), sl[h]], qk_ref[pl.ds(rb, c), sl[h]]
            qq.append(jnp.where(is_fwd, qkf, pltpu.roll(qkb, DK_A, axis=1)))
            kk.append(jnp.where(is_fwd, pltpu.roll(qkf, DK_A, axis=1), qkb))
            ll.append(jnp.where(is_fwd, la_ref[pl.ds(rf, c), sl[h]], la_ref[pl.ds(rb, c), sl[h]]))
            hi = ll[h].astype(BF16)
            r1 = ll[h] - hi.astype(F32)
            mid = r1.astype(BF16)
            lo = (r1 - mid.astype(F32)).astype(BF16)
            split3.append(jnp.concatenate([hi, mid, lo], axis=0))
        e_all = _dot(m3_ref[...], jnp.concatenate(split3, axis=1))
        e = [e_all[:, sl[h]] for h in heads]
        cx = [jnp.where(is_fwd, e[h][0:c], e[h][0:c] - ll[h]) for h in heads]
        a = [_dot_nt(qq[h].astype(BF16), split(kk[h].astype(BF16))) * mask_ref[levels] for h in heads]
        for lev in range(levels):
            for h in heads:
                ex = jnp.exp2(sg_ref[lev] * (cx[h] - e[h][(lev + 1) * c:(lev + 2) * c]))
                a[h] = a[h] + _dot_nt((qq[h] * ex).astype(BF16), split((kk[h] * ex).astype(BF16))) * mask_ref[lev]
        for h in heads:
            tot = e[h][c - 1:c]
            u = jnp.exp2(cx[h])
            w = jnp.exp2(tot - cx[h])
            qd = (qq[h] * jnp.where(is_fwd, u, w)).astype(BF16)
            kd = (kk[h] * jnp.where(is_fwd, w, u)).astype(BF16)
            vs = jnp.concatenate([v_ref[pl.ds(rf, c), sl[h]], v_ref[pl.ds(rb, c), sl[h]]], axis=0)
            st = st_scr[h]
            o = _dot(split(a[h].astype(BF16)), vs) + _dot_nt(split(qd), st.astype(BF16))
            of_ref[pl.ds(rf, c), sl[h]] = o[:c]
            ob_ref[pl.ds(rb, c), sl[h]] = o[c:]
            vt = jnp.transpose(vs.astype(F32)).astype(BF16)
            st_scr[h] = st * jnp.exp2(tot) + _dot(vt, split(kd))
        return carry

    lax.fori_loop(0, n, body, 0, unroll=2)
    for h in heads:
        sl = slice(h * LANES, (h + 1) * LANES)
        o = _rms(of_ref[:, sl] + ob_ref[:, sl], gain_ref[...]) * _silu(ga_ref[:, sl])
        o_ref[:, sl] = o.astype(BF16)
        if slot is not None:
            st = st_scr[h]
            st_ref[0, slot.at(), 0, h] = jnp.transpose(st[:, :DK_A])
            st_ref[0, slot.at(), 1, h] = jnp.transpose(st[:, DK_A:])
    if slot is not None:
        slot.zero_others(st_ref)


def _gla_call(qk, la, v, ga, gain, consts, s0, s0_slot, row0, nb, t, slot, prev):
    m3, sg, mask, levels = consts
    blk0 = row0 // t
    width = H_A * LANES
    tok = pl.BlockSpec((t, width), lambda b: (blk0 + b, 0))
    in_specs = [tok, tok, tok, tok, _resident(gain.shape), _resident(m3.shape), _resident(sg.shape), _resident(mask.shape)]
    args = [qk, la, v, ga, gain, m3, sg, mask]
    tail = (2, H_A, DK_A, DV_A)
    if s0 is not None:
        in_specs.append(pl.BlockSpec((1, 1) + tail, lambda b: (b, s0_slot, 0, 0, 0, 0)))
        args.append(s0)
    out_specs = [pl.BlockSpec((t, width), lambda b: (b, 0))]
    out_shape = [jax.ShapeDtypeStruct((nb * t, width), BF16)]
    aliases = {}
    if slot is not None:
        out_specs.append(slot.spec(tail, lambda b: b))
        out_shape.append(slot.shape(nb, tail))
        if not slot.first:
            aliases = {len(args): 1}
            in_specs.append(pl.BlockSpec(memory_space=pl.ANY))
            args.append(prev)
    return pl.pallas_call(
        functools.partial(_gla_kernel, t=t, has_s0=s0 is not None, slot=slot, levels=levels),
        grid=(nb,),
        in_specs=in_specs,
        out_specs=out_specs,
        out_shape=out_shape,
        scratch_shapes=[pltpu.VMEM((t, width), F32), pltpu.VMEM((t, width), F32), pltpu.VMEM((H_A, DV_A, LANES), F32)],
        input_output_aliases=aliases,
        name="gla",
        compiler_params=_cparams(("arbitrary",)),
    )(*args)


def _kv_expand_kernel(c_ref, r_ref, w_ref, o_ref, ro_ref):
    o_ref[...] = _dot(c_ref[0, 0].astype(BF16), w_ref[...]).astype(BF16)
    kr = r_ref[0, 0]
    ro_ref[...] = jnp.concatenate([kr, jnp.zeros((kr.shape[0], LANES - ROPE_B), F32)], axis=1).astype(BF16)


def _kv_expand_call(cache_ckv, cache_kr, slot, wukv):
    nb, _, past, r = cache_ckv.shape
    return pl.pallas_call(
        _kv_expand_kernel,
        grid=(nb,),
        in_specs=[pl.BlockSpec((1, 1, past, r), lambda b: (b, slot, 0, 0)),
                  pl.BlockSpec((1, 1, past, ROPE_B), lambda b: (b, slot, 0, 0)), _resident(wukv.shape)],
        out_specs=[pl.BlockSpec((past, wukv.shape[1]), lambda b: (b, 0)), pl.BlockSpec((past, LANES), lambda b: (b, 0))],
        out_shape=[jax.ShapeDtypeStruct((nb * past, wukv.shape[1]), BF16), jax.ShapeDtypeStruct((nb * past, LANES), BF16)],
        name="kv_expand",
        compiler_params=_cparams(("arbitrary",)),
    )(cache_ckv, cache_kr, wukv)


def _mla_attn_kernel(*refs, nseg):
    q_ref = refs[0]
    seg = [(refs[1 + 2 * s], refs[2 + 2 * s]) for s in range(nseg)]
    o_ref = refs[1 + 2 * nseg]
    def qk(hh):
        q = q_ref[:, hh * 2 * LANES:(hh + 1) * 2 * LANES]
        scores = []
        for kv_ref, kr_ref in seg:
            kcat = jnp.concatenate([kv_ref[:, hh * 2 * LANES:hh * 2 * LANES + LANES], kr_ref[...]], axis=1)
            scores.append(_dot_nt(q, kcat) * (MLA_SCALE * LOG2E))
        return scores

    nxt = qk(0)
    for hh in range(H_B):
        scores = nxt
        if hh + 1 < H_B:
            nxt = qk(hh + 1)
        mx = functools.reduce(jnp.maximum, [jnp.max(s, axis=-1, keepdims=True) for s in scores])
        ps = [jnp.exp2(s - mx) for s in scores]
        den = functools.reduce(lambda a, b: a + b, [jnp.sum(p, axis=-1, keepdims=True) for p in ps])
        acc = functools.reduce(lambda a, b: a + b, [
            _dot(p.astype(BF16), kv_ref[:, hh * 2 * LANES + LANES:(hh + 1) * 2 * LANES])
            for p, (kv_ref, _) in zip(ps, seg)])
        o_ref[:, hh * LANES:(hh + 1) * LANES] = (acc * (1.0 / den)).astype(BF16)


def _mla_attn_call(qb, segs, row0, nb, t):
    q0 = row0 // TQ
    per = t // TQ
    in_specs = [pl.BlockSpec((TQ, qb.shape[1]), lambda b, j: (q0 + b * per + j, 0))]
    args = [qb]
    for kv, kr, r0, s in segs:
        b0 = r0 // s
        in_specs.append(pl.BlockSpec((s, kv.shape[1]), lambda b, j, b0=b0: (b0 + b, 0)))
        in_specs.append(pl.BlockSpec((s, LANES), lambda b, j, b0=b0: (b0 + b, 0)))
        args += [kv, kr]
    return pl.pallas_call(
        functools.partial(_mla_attn_kernel, nseg=len(segs)),
        grid=(nb, per),
        in_specs=in_specs,
        out_specs=pl.BlockSpec((TQ, H_B * V_B), lambda b, j: (b * per + j, 0)),
        out_shape=jax.ShapeDtypeStruct((nb * t, H_B * V_B), BF16),
        name="mla_attn",
        compiler_params=_cparams(("arbitrary", "arbitrary")),
    )(*args)


def _odd_proj_kernel(*refs, prompt_tiles, slot):
    x_ref, mod_ref, gain_ref, w_ref, tab_ref = refs[:5]
    q_ref, kb_ref, vb_ref, kf_ref, vf_ref = refs[len(refs) - 5:]
    mod = mod_ref[0, 0]
    h = _normmod(x_ref[...], gain_ref[0, 0:1], mod[0:1], mod[1:2]).astype(BF16)
    y = _dot(h, w_ref[...])
    n = q_ref.shape[1]
    cos, sin_dn, sin_up = tab_ref[0], tab_ref[1], tab_ref[2]
    vb_ref[...] = y[:, 2 * n:3 * n].astype(BF16)
    for g in range(n // LANES):
        lo = g * LANES
        q_ref[:, lo:lo + LANES] = _rope(y[:, lo:lo + LANES], cos, sin_dn, sin_up).astype(BF16)
        kb_ref[:, lo:lo + LANES] = _rope(y[:, n + lo:n + lo + LANES], cos, sin_dn, sin_up).astype(BF16)

    @pl.when(pl.program_id(0) < prompt_tiles)
    def _():
        heads = n // LANES
        for g in range(heads):
            lo = g * LANES
            kf_ref[0, slot.at(), pl.ds(g, TM, stride=heads), :] = y[:, n + lo:n + lo + LANES]
            vf_ref[0, slot.at(), pl.ds(g, TM, stride=heads), :] = y[:, 2 * n + lo:2 * n + lo + LANES]
        slot.zero_others(kf_ref)
        slot.zero_others(vf_ref)


def _odd_proj_call(lay, x, mods, l, gains, w, tab, slot, prev):
    d = x.shape[1]
    n = w.shape[1] // 3
    row = lambda m: pl.BlockSpec((TM, m), lambda i: (i, 0))
    in_specs = [row(d), _mod_spec(lay, l, d), _layer_spec(gains.shape, l), _resident(w.shape), _tab_spec(lay)]
    args = [x, mods, gains, w, tab]
    aliases = {}
    if not slot.first:
        aliases = {len(args): 3, len(args) + 1: 4}
        in_specs += [pl.BlockSpec(memory_space=pl.ANY), pl.BlockSpec(memory_space=pl.ANY)]
        args += list(prev)
    tail = (TM * H_C, 2 * DH_C)
    return pl.pallas_call(
        functools.partial(_odd_proj_kernel, prompt_tiles=lay.pt, slot=slot),
        grid=(lay.tiles,),
        in_specs=in_specs,
        out_specs=[row(n), row(n), row(n), slot.spec(tail, lay.p_block), slot.spec(tail, lay.p_block)],
        out_shape=[jax.ShapeDtypeStruct((lay.nt, n), BF16)] * 3 + [slot.shape(lay.pt, tail)] * 2,
        input_output_aliases=aliases,
        name="odd_proj",
        compiler_params=_cparams(("arbitrary",)),
    )(*args)


def _diff_attn_kernel(*refs, nseg, has_ctx, lam_init):
    q_ref, dl_ref, gain_ref = refs[0], refs[1], refs[2]
    seg = [(refs[3 + 2 * s], refs[4 + 2 * s]) for s in range(nseg)]
    o_ref = refs[3 + 2 * nseg]
    if has_ctx:
        (kc_ref, vc_ref), (kc_scr, vc_scr) = seg[0], refs[4 + 2 * nseg:6 + 2 * nseg]

        @pl.when(pl.program_id(1) == 0)
        def _():
            for hh in range(H_C):
                rows = pl.ds(hh, kc_scr.shape[0], stride=H_C)
                kc_scr[:, hh * LANES:(hh + 1) * LANES] = kc_ref[0, 0, rows, :].astype(BF16)
                vc_scr[:, hh * LANES:(hh + 1) * LANES] = vc_ref[0, 0, rows, :].astype(BF16)

        seg[0] = (kc_scr, vc_scr)
    dl = dl_ref[0]
    lam = (jnp.exp(jnp.sum(dl[0:1] * dl[1:2], axis=-1, keepdims=True))
           - jnp.exp(jnp.sum(dl[2:3] * dl[3:4], axis=-1, keepdims=True)) + lam_init)
    tq = q_ref.shape[0]
    lane = lax.broadcasted_iota(jnp.int32, (1, LANES), 1)
    keep1 = jnp.where(lane < DH_C, 1.0, 0.0).astype(BF16)
    keep2 = jnp.where(lane < DH_C, 0.0, 1.0).astype(BF16)
    def qk(hh):
        q = q_ref[:, hh * LANES:(hh + 1) * LANES]
        qq = jnp.concatenate([q * keep1, q * keep2], axis=0)
        return [_dot_nt(qq, k_ref[:, hh * LANES:(hh + 1) * LANES]) for k_ref, _ in seg]

    nxt = qk(0)
    for hh in range(H_C):
        scores = nxt
        if hh + 1 < H_C:
            nxt = qk(hh + 1)
        mx = functools.reduce(jnp.maximum, [jnp.max(s, axis=-1, keepdims=True) for s in scores])
        ps = [jnp.exp2(s - mx) for s in scores]
        den = functools.reduce(lambda a, b: a + b, [jnp.sum(p, axis=-1, keepdims=True) for p in ps])
        acc = functools.reduce(lambda a, b: a + b, [
            _dot(p.astype(BF16), v_ref[:, hh * LANES:(hh + 1) * LANES]) for p, (_, v_ref) in zip(ps, seg)])
        acc = acc * (1.0 / den)
        o = acc[:tq] - lam * acc[tq:]
        o_ref[:, hh * LANES:(hh + 1) * LANES] = (_rms(o, gain_ref[0]) * (1.0 - lam_init)).astype(BF16)


def _diff_attn_call(q, dls, gains, slot, ctx, segs, row0, nb, t, lam_init):
    q0 = row0 // TQ
    per = t // TQ
    n = q.shape[1]
    in_specs = [pl.BlockSpec((TQ, n), lambda b, j: (q0 + b * per + j, 0)),
                pl.BlockSpec((1,) + dls.shape[1:], lambda b, j: (slot, 0, 0)),
                pl.BlockSpec((1, 1, gains.shape[2]), lambda b, j: (slot, 0, 0))]
    args = [q, dls, gains]
    scratch = []
    if ctx is not None:
        nb_c, n_l, past, heads, dh = ctx[0].shape
        for cache in ctx:
            in_specs.append(pl.BlockSpec((1, 1, past * heads, dh), lambda b, j: (b, slot, 0, 0)))
            args.append(cache.reshape(nb_c, n_l, past * heads, dh))
        scratch = [pltpu.VMEM((past, n), BF16), pltpu.VMEM((past, n), BF16)]
    for k, v, r0, s in segs:
        b0 = r0 // s
        in_specs.append(pl.BlockSpec((s, n), lambda b, j, b0=b0: (b0 + b, 0)))
        in_specs.append(pl.BlockSpec((s, n), lambda b, j, b0=b0: (b0 + b, 0)))
        args += [k, v]
    return pl.pallas_call(
        functools.partial(_diff_attn_kernel, nseg=len(segs) + (ctx is not None), has_ctx=ctx is not None,
                          lam_init=lam_init),
        grid=(nb, per),
        in_specs=in_specs,
        out_specs=pl.BlockSpec((TQ, n), lambda b, j: (b * per + j, 0)),
        out_shape=jax.ShapeDtypeStruct((nb * t, n), BF16),
        scratch_shapes=scratch,
        name="diff_attn",
        compiler_params=_cparams(("arbitrary", "arbitrary")),
    )(*args)


def _halo_specs(rows, width, tm, tile):
    per = tm // HALO
    last = rows // HALO - 1
    return [pl.BlockSpec((HALO, width), lambda i: (jnp.maximum(tile(i) * per - 1, 0), 0)),
            pl.BlockSpec((tm, width), lambda i: (tile(i), 0)),
            pl.BlockSpec((HALO, width), lambda i: (jnp.minimum((tile(i) + 1) * per, last), 0))]


def _ffn_kernel(*refs, lay, d_ff, final, nx, widths):
    pos = 3 * nx + 6 * len(widths)
    trips = [refs[3 * t:3 * t + 3] for t in range(pos // 3)]
    mod_ref, gain_ref, wo_ref, wup_ref, cw_ref, cb_ref, wdn_ref, fg_ref = refs[pos:pos + 8]
    outs = refs[pos + 8:len(refs) - 1]
    u_ref = refs[len(refs) - 1]
    tm = lay.tm
    i = pl.program_id(0)
    row = i * tm
    is_prompt = i < lay.pt
    seq = jnp.where(is_prompt, lay.seq_p, lay.seq_s)
    first = (row & (seq - 1)) == 0
    last = ((row + tm) & (seq - 1)) == 0

    def ext(trip_p, trip_s):
        if trip_s is None:
            return jnp.concatenate([r[...] for r in trip_p], axis=0)
        return jnp.concatenate([jnp.where(is_prompt, a[...], b[...]) for a, b in zip(trip_p, trip_s)], axis=0)

    mod = mod_ref[0, 0]
    x_ext = ext(trips[0], trips[1] if nx == 2 else None)
    proj = None
    k0 = 0
    for p, w in enumerate(widths):
        part = _dot(ext(trips[nx + 2 * p], trips[nx + 2 * p + 1]), wo_ref[0, k0:k0 + w, :])
        k0 += w
        proj = part if proj is None else proj + part
    xm = x_ext + mod[2:3] * proj
    gain, shift, scale = gain_ref[0, 1:2], mod[3:4], mod[4:5]
    hh = _normmod(xm, gain, shift, scale)
    h = jnp.concatenate([jnp.where(first, 0.0, hh[HALO - SUBLANES:HALO]), hh[HALO:HALO + tm],
                         jnp.where(last, 0.0, hh[HALO + tm:HALO + tm + SUBLANES])], axis=0).astype(BF16)
    x = xm[HALO:HALO + tm]
    acc = jnp.zeros((tm, x.shape[1]), F32)
    n_f = d_ff // FFN_TF
    n_slots = u_ref.shape[0]
    sub = lax.broadcasted_iota(jnp.int32, (SUBLANES, FFN_TF), 0)
    cut_prev = is_prompt & (sub == 0)
    cut_next = is_prompt & (sub == SUBLANES - 1)
    seams = range(lay.seq_p, tm, lay.seq_p)

    def up(f):
        for half, base in enumerate((0, d_ff)):
            lo = base + f * FFN_TF
            u_ref[f % n_slots, :, half * FFN_TF:(half + 1) * FFN_TF] = _dot(h, wup_ref[0, :, lo:lo + FFN_TF])

    def conv(f, half):
        lo = half * d_ff + f * FFN_TF
        cols = slice(half * FFN_TF, (half + 1) * FFN_TF)
        cw = cw_ref[0, :, lo:lo + FFN_TF]
        prev = u_ref[f % n_slots, SUBLANES - 1:SUBLANES - 1 + tm, cols]
        here = u_ref[f % n_slots, SUBLANES:SUBLANES + tm, cols]
        nxt = u_ref[f % n_slots, SUBLANES + 1:SUBLANES + 1 + tm, cols]
        for r in seams:
            prev = jnp.concatenate([prev[:r], jnp.where(cut_prev, 0.0, prev[r:r + SUBLANES]), prev[r + SUBLANES:]], axis=0)
            nxt = jnp.concatenate([nxt[:r - SUBLANES], jnp.where(cut_next, 0.0, nxt[r - SUBLANES:r]), nxt[r:]], axis=0)
        return prev * cw[0:1] + here * cw[1:2] + nxt * cw[2:3] + cb_ref[0, :, lo:lo + FFN_TF]

    for f in range(min(FFN_AHEAD, n_f)):
        up(f)
    for f in range(n_f):
        if f + FFN_AHEAD < n_f:
            up(f + FFN_AHEAD)
        act = (_silu(conv(f, 0)) * conv(f, 1)).astype(BF16)
        acc = acc + _dot(act, wdn_ref[0, f * FFN_TF:(f + 1) * FFN_TF, :])
    y = x + mod[5:6] * acc
    if not final:
        outs[0][...] = y
    else:
        yn = _rms(y, fg_ref[...])

        @pl.when(i < lay.pt)
        def _():
            outs[0][...] = yn

        @pl.when(i >= lay.pt)
        def _():
            outs[1][...] = yn


def _ffn_call(lay, x, pairs, mods, l, gains, wo, slot, wup, cw, cb, wdn, fgain, final):
    d = wo.shape[2]
    d_ff = wdn.shape[1]
    assert d_ff % FFN_TF == 0
    tm = lay.tm
    in_specs, args = [], []
    if isinstance(x, tuple):
        streams = [(x[0], lay.p_block), (x[1], lay.s_block)]
    else:
        streams = [(x, lambda i: i)]
    for a_p, a_s in pairs:
        streams += [(a_p, lay.p_block), (a_s, lay.s_block)]
    for arr, tile in streams:
        in_specs += _halo_specs(arr.shape[0], arr.shape[1], tm, tile)
        args += [arr, arr, arr]
    in_specs += [_mod_spec(lay, l, d), _layer_spec(gains.shape, l), _layer_spec(wo.shape, slot), _layer_spec(wup.shape, l),
                 _layer_spec(cw.shape, l), _layer_spec(cb.shape, l), _layer_spec(wdn.shape, l), _resident((1, d))]
    args += [mods, gains, wo, wup, cw, cb, wdn, fgain]
    if final:
        out_specs = [pl.BlockSpec((tm, d), lambda i: (lay.p_block(i), 0)), pl.BlockSpec((tm, d), lambda i: (lay.s_block(i), 0))]
        out_shape = [jax.ShapeDtypeStruct((lay.np, d), F32), jax.ShapeDtypeStruct((lay.ns, d), F32)]
    else:
        out_specs = [pl.BlockSpec((tm, d), lambda i: (i, 0))]
        out_shape = [jax.ShapeDtypeStruct((lay.nt, d), F32)]
    return pl.pallas_call(
        functools.partial(_ffn_kernel, lay=lay, d_ff=d_ff, final=final, nx=2 if isinstance(x, tuple) else 1,
                          widths=[a_p.shape[1] for a_p, _ in pairs]),
        name="ffn",
        grid=(lay.tiles,),
        in_specs=in_specs,
        out_specs=out_specs,
        out_shape=out_shape,
        scratch_shapes=[pltpu.VMEM((FFN_AHEAD + 1, tm + 2 * SUBLANES, 2 * FFN_TF), F32)],
        compiler_params=_cparams(("arbitrary",)),
    )(*args)


def _rope_tables(seq_s, both_halves):
    t = jnp.arange(seq_s)
    rowp = (t // GRID_W).astype(F32)
    colp = (t % GRID_W).astype(F32)
    half = ROPE_B // 2
    inv = ROPE_BASE ** (-jnp.arange(0, half, 2, dtype=F32) / half)
    ar, ac = rowp[:, None] * inv, colp[:, None] * inv
    zero = jnp.zeros_like(ar)
    cos = jnp.concatenate([jnp.cos(ar), jnp.cos(ar), jnp.cos(ac), jnp.cos(ac)], -1)
    sin_dn = jnp.concatenate([zero, jnp.sin(ar), zero, jnp.sin(ac)], -1)
    sin_up = jnp.concatenate([-jnp.sin(ar), zero, -jnp.sin(ac), zero], -1)
    if both_halves:
        parts = [jnp.concatenate([p, p], -1) for p in (cos, sin_dn, sin_up)]
    else:
        one, zz = jnp.ones_like(cos), jnp.zeros_like(cos)
        parts = [jnp.concatenate([cos, one], -1), jnp.concatenate([sin_dn, zz], -1), jnp.concatenate([sin_up, zz], -1)]
    ident = [jnp.ones((seq_s, LANES), F32), jnp.zeros((seq_s, LANES), F32), jnp.zeros((seq_s, LANES), F32)]
    return jnp.stack([jnp.concatenate([i_, p], 0) for i_, p in zip(ident, parts)], 0)


def _even_weights(w_in, w_dec, b_dec, w_uq):
    d = w_in.shape[0]
    sizes = (H_A * DK_A, H_A * DK_A, H_A * DV_A, H_A * DV_A, GLA_LR, GLA_LR, Q_RANK, KV_RANK, ROPE_B)
    offs = np.concatenate([[0], np.cumsum(sizes)])
    qa, ka, va, ga, lrf, lrb, cq, ckv, kr = [w_in[:, offs[j]:offs[j + 1]] for j in range(9)]
    qa = qa * (DK_A ** -0.5)
    qk = jnp.concatenate([qa.reshape(d, H_A, DK_A), ka.reshape(d, H_A, DK_A)], -1).reshape(d, 2 * H_A * DK_A)
    pad = jnp.zeros((d, LANES - ROPE_B - 2 * GLA_LR), w_in.dtype)
    w = jnp.concatenate([qk, va, ga, cq, ckv, kr, lrf, lrb, pad], -1).astype(BF16)
    wd = jnp.zeros((LANES, H_A, 2, DK_A), F32)
    wd = wd.at[ROPE_B:ROPE_B + GLA_LR, :, 0, :].set(w_dec[0].reshape(GLA_LR, H_A, DK_A))
    wd = wd.at[ROPE_B + GLA_LR:ROPE_B + 2 * GLA_LR, :, 1, :].set(w_dec[1].reshape(GLA_LR, H_A, DK_A))
    wd = wd.reshape(LANES, 2 * H_A * DK_A).astype(BF16)
    bd = jnp.stack([b_dec[0].reshape(H_A, DK_A), b_dec[1].reshape(H_A, DK_A)], 1).reshape(1, 2 * H_A * DK_A)
    wq = w_uq.reshape(Q_RANK, H_B, NOPE_B + ROPE_B)
    wq = jnp.concatenate([wq, jnp.zeros((Q_RANK, H_B, 2 * LANES - NOPE_B - ROPE_B), w_uq.dtype)], -1)
    wq = wq.reshape(Q_RANK, H_B * 2 * LANES).astype(BF16)
    return w, wd, bd, wq


def kernel(x_prompt, x_sample, state_gla, cache_mla_ckv, cache_mla_krope, cache_diff_k, cache_diff_v, c, c_ctx, w_ada, b_ada, norm_gain, final_gain, w_in_even, w_out_even, gla_w_decay, gla_b_decay, gla_norm, mla_q_norm, mla_kv_norm, mla_w_uq, mla_w_ukv, w_in_odd, w_out_odd, diff_lambda, diff_norm, ffn_w_up, ffn_conv_w, ffn_conv_b, ffn_w_down):
    nb_p, seq_p, d = x_prompt.shape
    nb_s, seq_s, _ = x_sample.shape
    past = cache_mla_ckv.shape[2]
    depth = w_ada.shape[0]
    n_even, n_odd = w_in_even.shape[0], w_in_odd.shape[0]
    n_p, n_s = nb_p * seq_p, nb_s * seq_s
    lay = _Layout(n_p, n_s, seq_p, seq_s, TM)
    lay_ffn = _Layout(n_p, n_s, seq_p, seq_s, FFN_TM)
    assert seq_p == TM and seq_s % GRID_W == 0 and 1 + nb_s <= ADA_ROWS
    assert n_p % seq_s == 0 and seq_p & (seq_p - 1) == 0 and seq_s & (seq_s - 1) == 0
    assert seq_p % GLA_C == 0 and seq_s % GLA_C == 0 and seq_p % TQ == 0 and seq_s % TQ == 0

    cond = jnp.concatenate([c_ctx[None, :], c, jnp.zeros((ADA_ROWS - 1 - nb_s, d), F32)], 0)
    mods = _ada_call(cond, w_ada, b_ada).reshape(depth, ADA_ROWS, 6, d)

    tab_mla = _rope_tables(seq_s, both_halves=False)
    tab_diff = _rope_tables(seq_s, both_halves=True)
    gla_consts = _gla_consts(GLA_C)
    wup_all = ffn_w_up.astype(BF16)
    wdn_all = ffn_w_down.astype(BF16)
    cb_all = ffn_conv_b.reshape(depth, 1, -1)
    wo_even = w_out_even.astype(BF16)
    wo_odd = w_out_odd.astype(BF16)
    fgain = final_gain.reshape(1, d)
    dgains = diff_norm.reshape(n_odd, 1, 2 * DH_C)

    x = (x_prompt.reshape(n_p, d), x_sample.reshape(n_s, d))
    st_new = caches_even = caches_odd = None
    for l in range(depth):
        i = l // 2
        if l % 2 == 0:
            slot = _Slot(caches_even, i, n_even)
            w, wd, bd, wq = _even_weights(w_in_even[i], gla_w_decay[i], gla_b_decay[i], mla_w_uq[i])
            wukv = mla_w_ukv[i].astype(BF16)
            qk, la, v, ga, qb, kra, kv, ckv_new, kr_new = _even_proj_call(
                lay, x, mods, l, norm_gain, w, wd, bd, mla_q_norm[i].reshape(1, -1), mla_kv_norm[i].reshape(1, -1),
                wq, wukv, tab_mla, slot, caches_even)
            caches_even = (ckv_new, kr_new)
            ggain = gla_norm[i].reshape(1, DV_A)
            og_p, st_new = _gla_call(qk, la, v, ga, ggain, gla_consts, None, 0, 0, nb_p, seq_p,
                                     _Slot(st_new, i, n_even), st_new)
            og_s, = _gla_call(qk, la, v, ga, ggain, gla_consts, state_gla, i, n_p, nb_s, seq_s, None, None)
            kv_ctx, kr_ctx = _kv_expand_call(cache_mla_ckv, cache_mla_krope, i, wukv)
            om_p = _mla_attn_call(qb, [(kv, kra, 0, seq_p)], 0, nb_p, seq_p)
            om_s = _mla_attn_call(qb, [(kv_ctx, kr_ctx, 0, past), (kv, kra, n_p, seq_s)], n_p, nb_s, seq_s)
            pairs, wo = [(og_p, og_s), (om_p, om_s)], wo_even
        else:
            slot = _Slot(caches_odd, i, n_odd)
            lam_init = 0.8 - 0.6 * math.exp(-0.3 * l)
            n = H_C * 2 * DH_C
            col_scale = np.concatenate([np.full(n, DIFF_SCALE * LOG2E, np.float32), np.ones(2 * n, np.float32)])
            wi = (w_in_odd[i] * col_scale).astype(BF16)
            q, kb, vb, kf_new, vf_new = _odd_proj_call(lay, x, mods, l, norm_gain, wi, tab_diff, slot, caches_odd)
            caches_odd = (kf_new, vf_new)
            od_p = _diff_attn_call(q, diff_lambda, dgains, i, None, [(kb, vb, 0, seq_p)], 0, nb_p, seq_p, lam_init)
            od_s = _diff_attn_call(q, diff_lambda, dgains, i, (cache_diff_k, cache_diff_v), [(kb, vb, n_p, seq_s)],
                                   n_p, nb_s, seq_s, lam_init)
            pairs, wo = [(od_p, od_s)], wo_odd
        x = _ffn_call(lay_ffn, x, pairs, mods, l, norm_gain, wo, i, wup_all, ffn_conv_w, cb_all, wdn_all, fgain,
                      final=(l == depth - 1))
        if l < depth - 1:
            x = x[0]
    y_prompt = x[0].reshape(nb_p, seq_p, d)
    y_sample = x[1].reshape(nb_s, seq_s, d)
    diff_shape = (nb_p, n_odd, seq_p, H_C, 2 * DH_C)
    return (y_prompt, y_sample, st_new, caches_even[0], caches_even[1],
            caches_odd[0].reshape(diff_shape), caches_odd[1].reshape(diff_shape))
```

```python
import functools
import math

import numpy as np
import jax
import jax.numpy as jnp
from jax import lax
from jax.experimental import pallas as pl
from jax.experimental.pallas import tpu as pltpu

GRID_W = 64
ROPE_BASE = 10000.0
EPS = 1e-6
H_A, DK_A, DV_A = 4, 64, 128
GLA_LR = 16
GLA_TAU = 16.0
H_B, Q_RANK, KV_RANK, NOPE_B, ROPE_B, V_B = 4, 256, 256, 128, 64, 128
MLA_SCALE = (NOPE_B + ROPE_B) ** -0.5
H_C, DH_C = 8, 64
DIFF_SCALE = DH_C ** -0.5
LOG2E = math.log2(math.e)

LANES = 128
SUBLANES = 8
HALO = 16
VMEM_LIMIT = 56 * 1024 * 1024

TM = 256
TQ = 256
GLA_C = 64
GLA_STEPS = 2
ADA_TN = 1536
FFN_TM = 512
FFN_TF = 256
FFN_AHEAD = 2
FFN_DOWN = 4
ADA_ROWS = 16

F32 = jnp.float32
BF16 = jnp.bfloat16


def _dot(a, b):
    return jnp.dot(a, b, preferred_element_type=F32)


def _dot_nt(a, b):
    return lax.dot_general(a, b, (((1,), (1,)), ((), ())), preferred_element_type=F32)


def _silu(x):
    return x * (1.0 / (1.0 + jnp.exp(-x)))


def _rms(x, gain):
    return x * lax.rsqrt(jnp.mean(x * x, axis=-1, keepdims=True) + EPS) * gain


def _normmod(x, gain, shift, scale):
    return _rms(x, gain) * (1.0 + scale) + shift


def _rope(x, cos, sin_dn, sin_up):
    return (x * cos + pltpu.roll(x, 16, axis=1) * sin_dn + pltpu.roll(x, LANES - 16, axis=1) * sin_up)


def _cparams(sem, vmem=VMEM_LIMIT):
    return pltpu.CompilerParams(dimension_semantics=sem, vmem_limit_bytes=vmem)


def _resident(shape):
    nd = len(shape)
    return pl.BlockSpec(shape, lambda *_: (0,) * nd, pipeline_mode=pl.Buffered(1))


def _layer_spec(shape, l):
    nd = len(shape)
    return pl.BlockSpec((1,) + tuple(shape[1:]), lambda *_: (l,) + (0,) * (nd - 1), pipeline_mode=pl.Buffered(1))


def _ada_kernel(c_ref, w_ref, b_ref, o_ref):
    s = _silu(c_ref[...]).astype(BF16)
    o_ref[0] = _dot(s, w_ref[0].astype(BF16)) + b_ref[0]


def _ada_call(cond, w_ada, b_ada):
    depth, d, n = w_ada.shape
    return pl.pallas_call(
        _ada_kernel,
        grid=(depth, n // ADA_TN),
        in_specs=[
            pl.BlockSpec((ADA_ROWS, d), lambda l, j: (0, 0)),
            pl.BlockSpec((1, d, ADA_TN), lambda l, j: (l, 0, j)),
            pl.BlockSpec((1, 1, ADA_TN), lambda l, j: (l, 0, j)),
        ],
        out_specs=pl.BlockSpec((1, ADA_ROWS, ADA_TN), lambda l, j: (l, 0, j)),
        out_shape=jax.ShapeDtypeStruct((depth, ADA_ROWS, n), F32),
        name="ada",
        compiler_params=_cparams(("arbitrary", "arbitrary")),
    )(cond, w_ada, b_ada.reshape(depth, 1, n))


class _Layout:
    def __init__(self, n_prompt, n_sample, seq_p, seq_s, tm):
        assert tm % seq_p == 0 and n_prompt % tm == 0 and n_sample % tm == 0 and seq_s % tm == 0
        self.tm = tm
        self.np, self.ns, self.nt = n_prompt, n_sample, n_prompt + n_sample
        self.seq_p, self.seq_s = seq_p, seq_s
        self.pt = n_prompt // tm
        self.st = n_sample // tm
        self.tiles = self.pt + self.st
        self.tiles_per_seq_s = seq_s // tm

    def group(self, i):
        return jnp.where(i < self.pt, 0, 1 + (i - self.pt) // self.tiles_per_seq_s)

    def pos_block(self, i):
        return jnp.where(i < self.pt, 0, self.tiles_per_seq_s + (i - self.pt) % self.tiles_per_seq_s)

    def p_block(self, i):
        return jnp.minimum(i, self.pt - 1)

    def s_block(self, i):
        return jnp.maximum(i - self.pt, 0)


def _mod_spec(lay, l, d):
    return pl.BlockSpec((1, 1, 6, d), lambda i: (l, lay.group(i), 0, 0))


def _tab_spec(lay):
    return pl.BlockSpec((3, lay.tm, LANES), lambda i: (0, lay.pos_block(i), 0))


def _x_specs(lay, x):
    if isinstance(x, tuple):
        d = x[0].shape[1]
        return [pl.BlockSpec((lay.tm, d), lambda i: (lay.p_block(i), 0)),
                pl.BlockSpec((lay.tm, d), lambda i: (lay.s_block(i), 0))], list(x)
    return [pl.BlockSpec((lay.tm, x.shape[1]), lambda i: (i, 0))], [x]


def _read_x(x_refs, prompt_tiles):
    if len(x_refs) == 1:
        return x_refs[0][...]
    return jnp.where(pl.program_id(0) < prompt_tiles, x_refs[0][...], x_refs[1][...])


class _Slot:
    def __init__(self, prev, slot, n_slots):
        self.prev, self.slot, self.n_slots = prev, slot, n_slots

    @property
    def first(self):
        return self.prev is None

    def spec(self, tail, index):
        zeros = (0,) * len(tail)
        if self.first:
            return pl.BlockSpec((1, self.n_slots) + tail, lambda *g: (index(*g), 0) + zeros)
        s = self.slot
        return pl.BlockSpec((1, 1) + tail, lambda *g: (index(*g), s) + zeros)

    def shape(self, nb, tail):
        return jax.ShapeDtypeStruct((nb, self.n_slots) + tail, F32)

    def at(self):
        return self.slot if self.first else 0

    def zero_others(self, ref):
        if self.first:
            for s in range(self.n_slots):
                if s != self.slot:
                    ref[0, s] = jnp.zeros(ref.shape[2:], F32)


def _even_proj_kernel(*refs, nx, prompt_tiles, slot):
    x_refs = refs[:nx]
    (mod_ref, gain_ref, w_ref, wd_ref, bd_ref, qn_ref, kvn_ref, wuq_ref, wukv_ref, tab_ref) = refs[nx:nx + 10]
    outs = refs[len(refs) - 9:]
    qk_ref, la_ref, v_ref, ga_ref, qb_ref, kra_ref, kv_ref, ckv_ref, kr_ref = outs
    mod = mod_ref[0, 0]
    h = _normmod(_read_x(x_refs, prompt_tiles), gain_ref[0, 0:1], mod[0:1], mod[1:2]).astype(BF16)
    y = _dot(h, w_ref[...])
    qk_ref[...] = y[:, 0:512]
    v_ref[...] = y[:, 512:1024].astype(BF16)
    ga_ref[...] = y[:, 1024:1536]
    cq, ckv, g = y[:, 1536:1792], y[:, 1792:2048], y[:, 2048:2176]
    pre = _dot(g.astype(BF16), wd_ref[...]) + bd_ref[...]
    la_ref[...] = (jnp.minimum(pre, 0.0) - jnp.log(1.0 + jnp.exp(-jnp.abs(pre)))) * (LOG2E / GLA_TAU)
    cos, sin_dn, sin_up = tab_ref[0], tab_ref[1], tab_ref[2]
    qb = _dot(_rms(cq, qn_ref[...]).astype(BF16), wuq_ref[...])
    for hh in range(H_B):
        lo = hh * 2 * LANES
        qb_ref[:, lo:lo + LANES] = qb[:, lo:lo + LANES].astype(BF16)
        qb_ref[:, lo + LANES:lo + 2 * LANES] = _rope(qb[:, lo + LANES:lo + 2 * LANES], cos, sin_dn, sin_up).astype(BF16)
    ckvn = _rms(ckv, kvn_ref[...])
    kv_ref[...] = _dot(ckvn.astype(BF16), wukv_ref[...]).astype(BF16)
    lane = lax.broadcasted_iota(jnp.int32, g.shape, 1)
    kra_ref[...] = jnp.where(lane < ROPE_B, _rope(g, cos, sin_dn, sin_up), 0.0).astype(BF16)

    @pl.when(pl.program_id(0) < prompt_tiles)
    def _():
        ckv_ref[0, slot.at()] = ckvn
        kr_ref[0, slot.at()] = g[:, :ROPE_B]
        slot.zero_others(ckv_ref)
        slot.zero_others(kr_ref)


def _even_proj_call(lay, x, mods, l, gains, w, wd, bd, qn, kvn, wuq, wukv, tab, slot, prev):
    d = w.shape[0]
    x_specs, x_args = _x_specs(lay, x)
    row = lambda n: pl.BlockSpec((TM, n), lambda i: (i, 0))
    outs = [(512, F32), (512, F32), (512, BF16), (512, F32), (1024, BF16), (LANES, BF16), (1024, BF16)]
    in_specs = x_specs + [_mod_spec(lay, l, d), _layer_spec(gains.shape, l), _resident(w.shape), _resident(wd.shape),
                          _resident(bd.shape), _resident(qn.shape), _resident(kvn.shape), _resident(wuq.shape),
                          _resident(wukv.shape), _tab_spec(lay)]
    args = x_args + [mods, gains, w, wd, bd, qn, kvn, wuq, wukv, tab]
    aliases = {}
    if not slot.first:
        aliases = {len(args): len(outs), len(args) + 1: len(outs) + 1}
        in_specs += [pl.BlockSpec(memory_space=pl.ANY), pl.BlockSpec(memory_space=pl.ANY)]
        args += list(prev)
    nb_p = lay.pt
    return pl.pallas_call(
        functools.partial(_even_proj_kernel, nx=len(x_args), prompt_tiles=lay.pt, slot=slot),
        grid=(lay.tiles,),
        in_specs=in_specs,
        out_specs=[row(n) for n, _ in outs] + [slot.spec((TM, KV_RANK), lay.p_block), slot.spec((TM, ROPE_B), lay.p_block)],
        out_shape=[jax.ShapeDtypeStruct((lay.nt, n), dt) for n, dt in outs]
        + [slot.shape(nb_p, (TM, KV_RANK)), slot.shape(nb_p, (TM, ROPE_B))],
        input_output_aliases=aliases,
        name="even_proj",
        compiler_params=_cparams(("arbitrary",)),
    )(*args)


def _gla_consts(c):
    levels = int(math.log2(c))
    idx = np.arange(c)
    cum = idx[None, :] <= idx[:, None]
    rows, sgn, masks = [cum], [], []
    for lev in range(levels):
        half = 1 << lev
        node = idx // (2 * half)
        mid = node * 2 * half + half
        upper = idx >= mid
        rows.append(cum[mid - 1])
        sgn.append(np.repeat(np.where(upper, 1.0, -1.0)[:, None], LANES, 1))
        same = node[:, None] == node[None, :]
        masks.append(np.concatenate([same & upper[:, None] & ~upper[None, :],
                                     same & ~upper[:, None] & upper[None, :]], 1))
    eye = np.eye(c, dtype=bool)
    masks.append(np.concatenate([eye, eye], 1))
    m = np.concatenate(rows, 0)
    return (jnp.asarray(np.concatenate([m, m, m], 1), BF16), jnp.asarray(np.stack(sgn, 0), F32),
            jnp.asarray(np.stack(masks, 0), F32), levels)


def _gla_kernel(*refs, t, has_s0, slot, levels):
    qk_ref, la_ref, v_ref, ga_ref, gain_ref, m3_ref, sg_ref, mask_ref = refs[:8]
    s0_ref = refs[8] if has_s0 else None
    n_out = 1 if slot is None else 2
    o_ref = refs[len(refs) - 3 - n_out]
    st_ref = None if slot is None else refs[len(refs) - 4]
    of_ref, ob_ref, st_scr = refs[len(refs) - 3:]
    c = GLA_C
    n = t // c
    heads = range(H_A)
    lane = lax.broadcasted_iota(jnp.int32, (1, LANES), 1)
    is_fwd = lane < DK_A
    keep_f = jnp.where(is_fwd, 1.0, 0.0).astype(BF16)
    keep_b = jnp.where(is_fwd, 0.0, 1.0).astype(BF16)

    def split(x):
        return jnp.concatenate([x * keep_f, x * keep_b], axis=0)

    for h in heads:
        if has_s0:
            st_scr[h] = jnp.concatenate([jnp.transpose(s0_ref[0, 0, 0, h]), jnp.transpose(s0_ref[0, 0, 1, h])], axis=1)
        else:
            st_scr[h] = jnp.zeros((DV_A, LANES), F32)

    steps = GLA_STEPS if n % GLA_STEPS == 0 else 1

    def body(j, carry):
        units = [(s, h) for s in range(steps) for h in heads]
        rf = [pl.multiple_of((j * steps + s) * c, c) for s in range(steps)]
        rb = [pl.multiple_of((n - 1 - j * steps - s) * c, c) for s in range(steps)]
        sl = [slice(h * LANES, (h + 1) * LANES) for h in heads]
        qq, kk, ll, split3 = [], [], [], []
        for s, h in units:
            qkf, qkb = qk_ref[pl.ds(rf[s], c), sl[h]], qk_ref[pl.ds(rb[s], c), sl[h]]
            qq.append(jnp.where(is_fwd, qkf, pltpu.roll(qkb, DK_A, axis=1)))
            kk.append(jnp.where(is_fwd, pltpu.roll(qkf, DK_A, axis=1), qkb))
            ll.append(jnp.where(is_fwd, la_ref[pl.ds(rf[s], c), sl[h]], la_ref[pl.ds(rb[s], c), sl[h]]))
            hi = ll[-1].astype(BF16)
            r1 = ll[-1] - hi.astype(F32)
            mid = r1.astype(BF16)
            lo = (r1 - mid.astype(F32)).astype(BF16)
            split3.append(jnp.concatenate([hi, mid, lo], axis=0))
        e_all = _dot(m3_ref[...], jnp.concatenate(split3, axis=1))
        e = [e_all[:, k * LANES:(k + 1) * LANES] for k in range(len(units))]
        cx = [jnp.where(is_fwd, e[k][0:c], e[k][0:c] - ll[k]) for k in range(len(units))]
        a = [_dot_nt(qq[k].astype(BF16), split(kk[k].astype(BF16))) * mask_ref[levels] for k in range(len(units))]
        for lev in range(levels):
            for k in range(len(units)):
                ex = jnp.exp2(sg_ref[lev] * (cx[k] - e[k][(lev + 1) * c:(lev + 2) * c]))
                a[k] = a[k] + _dot_nt((qq[k] * ex).astype(BF16), split((kk[k] * ex).astype(BF16))) * mask_ref[lev]
        for k, (s, h) in enumerate(units):
            tot = e[k][c - 1:c]
            u = jnp.exp2(cx[k])
            w = jnp.exp2(tot - cx[k])
            qd = (qq[k] * jnp.where(is_fwd, u, w)).astype(BF16)
            kd = (kk[k] * jnp.where(is_fwd, w, u)).astype(BF16)
            vs = jnp.concatenate([v_ref[pl.ds(rf[s], c), sl[h]], v_ref[pl.ds(rb[s], c), sl[h]]], axis=0)
            st = st_scr[h]
            o = _dot(split(a[k].astype(BF16)), vs) + _dot_nt(split(qd), st.astype(BF16))
            of_ref[pl.ds(rf[s], c), sl[h]] = o[:c]
            ob_ref[pl.ds(rb[s], c), sl[h]] = o[c:]
            vt = jnp.transpose(vs.astype(F32)).astype(BF16)
            st_scr[h] = st * jnp.exp2(tot) + _dot(vt, split(kd))
        return carry

    lax.fori_loop(0, n // steps, body, 0)
    for h in heads:
        sl = slice(h * LANES, (h + 1) * LANES)
        o = _rms(of_ref[:, sl] + ob_ref[:, sl], gain_ref[...]) * _silu(ga_ref[:, sl])
        o_ref[:, sl] = o.astype(BF16)
        if slot is not None:
            st = st_scr[h]
            st_ref[0, slot.at(), 0, h] = jnp.transpose(st[:, :DK_A])
            st_ref[0, slot.at(), 1, h] = jnp.transpose(st[:, DK_A:])
    if slot is not None:
        slot.zero_others(st_ref)


def _gla_call(qk, la, v, ga, gain, consts, s0, s0_slot, row0, nb, t, slot, prev):
    m3, sg, mask, levels = consts
    blk0 = row0 // t
    width = H_A * LANES
    tok = pl.BlockSpec((t, width), lambda b: (blk0 + b, 0))
    in_specs = [tok, tok, tok, tok, _resident(gain.shape), _resident(m3.shape), _resident(sg.shape), _resident(mask.shape)]
    args = [qk, la, v, ga, gain, m3, sg, mask]
    tail = (2, H_A, DK_A, DV_A)
    if s0 is not None:
        in_specs.append(pl.BlockSpec((1, 1) + tail, lambda b: (b, s0_slot, 0, 0, 0, 0)))
        args.append(s0)
    out_specs = [pl.BlockSpec((t, width), lambda b: (b, 0))]
    out_shape = [jax.ShapeDtypeStruct((nb * t, width), BF16)]
    aliases = {}
    if slot is not None:
        out_specs.append(slot.spec(tail, lambda b: b))
        out_shape.append(slot.shape(nb, tail))
        if not slot.first:
            aliases = {len(args): 1}
            in_specs.append(pl.BlockSpec(memory_space=pl.ANY))
            args.append(prev)
    return pl.pallas_call(
        functools.partial(_gla_kernel, t=t, has_s0=s0 is not None, slot=slot, levels=levels),
        grid=(nb,),
        in_specs=in_specs,
        out_specs=out_specs,
        out_shape=out_shape,
        scratch_shapes=[pltpu.VMEM((t, width), F32), pltpu.VMEM((t, width), F32), pltpu.VMEM((H_A, DV_A, LANES), F32)],
        input_output_aliases=aliases,
        name="gla",
        compiler_params=_cparams(("arbitrary",)),
    )(*args)


def _kv_expand_kernel(c_ref, r_ref, w_ref, o_ref, ro_ref):
    o_ref[...] = _dot(c_ref[0, 0].astype(BF16), w_ref[...]).astype(BF16)
    kr = r_ref[0, 0]
    ro_ref[...] = jnp.concatenate([kr, jnp.zeros((kr.shape[0], LANES - ROPE_B), F32)], axis=1).astype(BF16)


def _kv_expand_call(cache_ckv, cache_kr, slot, wukv):
    nb, _, past, r = cache_ckv.shape
    return pl.pallas_call(
        _kv_expand_kernel,
        grid=(nb,),
        in_specs=[pl.BlockSpec((1, 1, past, r), lambda b: (b, slot, 0, 0)),
                  pl.BlockSpec((1, 1, past, ROPE_B), lambda b: (b, slot, 0, 0)), _resident(wukv.shape)],
        out_specs=[pl.BlockSpec((past, wukv.shape[1]), lambda b: (b, 0)), pl.BlockSpec((past, LANES), lambda b: (b, 0))],
        out_shape=[jax.ShapeDtypeStruct((nb * past, wukv.shape[1]), BF16), jax.ShapeDtypeStruct((nb * past, LANES), BF16)],
        name="kv_expand",
        compiler_params=_cparams(("arbitrary",)),
    )(cache_ckv, cache_kr, wukv)


def _mla_attn_kernel(*refs, nseg):
    q_ref = refs[0]
    seg = [(refs[1 + 2 * s], refs[2 + 2 * s]) for s in range(nseg)]
    o_ref = refs[1 + 2 * nseg]
    def qk(hh):
        q = q_ref[:, hh * 2 * LANES:(hh + 1) * 2 * LANES]
        scores = []
        for kv_ref, kr_ref in seg:
            kcat = jnp.concatenate([kv_ref[:, hh * 2 * LANES:hh * 2 * LANES + LANES], kr_ref[...]], axis=1)
            scores.append(_dot_nt(q, kcat) * (MLA_SCALE * LOG2E))
        return scores

    nxt = qk(0)
    for hh in range(H_B):
        scores = nxt
        if hh + 1 < H_B:
            nxt = qk(hh + 1)
        mx = functools.reduce(jnp.maximum, [jnp.max(s, axis=-1, keepdims=True) for s in scores])
        ps = [jnp.exp2(s - mx) for s in scores]
        den = functools.reduce(lambda a, b: a + b, [jnp.sum(p, axis=-1, keepdims=True) for p in ps])
        acc = functools.reduce(lambda a, b: a + b, [
            _dot(p.astype(BF16), kv_ref[:, hh * 2 * LANES + LANES:(hh + 1) * 2 * LANES])
            for p, (kv_ref, _) in zip(ps, seg)])
        o_ref[:, hh * LANES:(hh + 1) * LANES] = (acc * (1.0 / den)).astype(BF16)


def _mla_attn_call(qb, segs, row0, nb, t):
    q0 = row0 // TQ
    per = t // TQ
    in_specs = [pl.BlockSpec((TQ, qb.shape[1]), lambda b, j: (q0 + b * per + j, 0))]
    args = [qb]
    for kv, kr, r0, s in segs:
        b0 = r0 // s
        in_specs.append(pl.BlockSpec((s, kv.shape[1]), lambda b, j, b0=b0: (b0 + b, 0)))
        in_specs.append(pl.BlockSpec((s, LANES), lambda b, j, b0=b0: (b0 + b, 0)))
        args += [kv, kr]
    return pl.pallas_call(
        functools.partial(_mla_attn_kernel, nseg=len(segs)),
        grid=(nb, per),
        in_specs=in_specs,
        out_specs=pl.BlockSpec((TQ, H_B * V_B), lambda b, j: (b * per + j, 0)),
        out_shape=jax.ShapeDtypeStruct((nb * t, H_B * V_B), BF16),
        name="mla_attn",
        compiler_params=_cparams(("arbitrary", "arbitrary")),
    )(*args)


def _odd_proj_kernel(*refs, prompt_tiles, slot):
    x_ref, mod_ref, gain_ref, w_ref, tab_ref = refs[:5]
    q_ref, kb_ref, vb_ref, kf_ref, vf_ref = refs[len(refs) - 5:]
    mod = mod_ref[0, 0]
    h = _normmod(x_ref[...], gain_ref[0, 0:1], mod[0:1], mod[1:2]).astype(BF16)
    y = _dot(h, w_ref[...])
    n = q_ref.shape[1]
    cos, sin_dn, sin_up = tab_ref[0], tab_ref[1], tab_ref[2]
    vb_ref[...] = y[:, 2 * n:3 * n].astype(BF16)
    for g in range(n // LANES):
        lo = g * LANES
        q_ref[:, lo:lo + LANES] = _rope(y[:, lo:lo + LANES], cos, sin_dn, sin_up).astype(BF16)
        kb_ref[:, lo:lo + LANES] = _rope(y[:, n + lo:n + lo + LANES], cos, sin_dn, sin_up).astype(BF16)

    @pl.when(pl.program_id(0) < prompt_tiles)
    def _():
        heads = n // LANES
        for g in range(heads):
            lo = g * LANES
            kf_ref[0, slot.at(), pl.ds(g, TM, stride=heads), :] = y[:, n + lo:n + lo + LANES]
            vf_ref[0, slot.at(), pl.ds(g, TM, stride=heads), :] = y[:, 2 * n + lo:2 * n + lo + LANES]
        slot.zero_others(kf_ref)
        slot.zero_others(vf_ref)


def _odd_proj_call(lay, x, mods, l, gains, w, tab, slot, prev):
    d = x.shape[1]
    n = w.shape[1] // 3
    row = lambda m: pl.BlockSpec((TM, m), lambda i: (i, 0))
    in_specs = [row(d), _mod_spec(lay, l, d), _layer_spec(gains.shape, l), _resident(w.shape), _tab_spec(lay)]
    args = [x, mods, gains, w, tab]
    aliases = {}
    if not slot.first:
        aliases = {len(args): 3, len(args) + 1: 4}
        in_specs += [pl.BlockSpec(memory_space=pl.ANY), pl.BlockSpec(memory_space=pl.ANY)]
        args += list(prev)
    tail = (TM * H_C, 2 * DH_C)
    return pl.pallas_call(
        functools.partial(_odd_proj_kernel, prompt_tiles=lay.pt, slot=slot),
        grid=(lay.tiles,),
        in_specs=in_specs,
        out_specs=[row(n), row(n), row(n), slot.spec(tail, lay.p_block), slot.spec(tail, lay.p_block)],
        out_shape=[jax.ShapeDtypeStruct((lay.nt, n), BF16)] * 3 + [slot.shape(lay.pt, tail)] * 2,
        input_output_aliases=aliases,
        name="odd_proj",
        compiler_params=_cparams(("arbitrary",)),
    )(*args)


def _diff_attn_kernel(*refs, nseg, has_ctx, lam_init):
    q_ref, dl_ref, gain_ref = refs[0], refs[1], refs[2]
    seg = [(refs[3 + 2 * s], refs[4 + 2 * s]) for s in range(nseg)]
    o_ref = refs[3 + 2 * nseg]
    if has_ctx:
        (kc_ref, vc_ref), (kc_scr, vc_scr) = seg[0], refs[4 + 2 * nseg:6 + 2 * nseg]

        @pl.when(pl.program_id(1) == 0)
        def _():
            for hh in range(H_C):
                rows = pl.ds(hh, kc_scr.shape[0], stride=H_C)
                kc_scr[:, hh * LANES:(hh + 1) * LANES] = kc_ref[0, 0, rows, :].astype(BF16)
                vc_scr[:, hh * LANES:(hh + 1) * LANES] = vc_ref[0, 0, rows, :].astype(BF16)

        seg[0] = (kc_scr, vc_scr)
    dl = dl_ref[0]
    lam = (jnp.exp(jnp.sum(dl[0:1] * dl[1:2], axis=-1, keepdims=True))
           - jnp.exp(jnp.sum(dl[2:3] * dl[3:4], axis=-1, keepdims=True)) + lam_init)
    tq = q_ref.shape[0]
    lane = lax.broadcasted_iota(jnp.int32, (1, LANES), 1)
    keep1 = jnp.where(lane < DH_C, 1.0, 0.0).astype(BF16)
    keep2 = jnp.where(lane < DH_C, 0.0, 1.0).astype(BF16)
    def qk(hh):
        q = q_ref[:, hh * LANES:(hh + 1) * LANES]
        qq = jnp.concatenate([q * keep1, q * keep2], axis=0)
        return [_dot_nt(qq, k_ref[:, hh * LANES:(hh + 1) * LANES]) for k_ref, _ in seg]

    nxt = qk(0)
    for hh in range(H_C):
        scores = nxt
        if hh + 1 < H_C:
            nxt = qk(hh + 1)
        mx = functools.reduce(jnp.maximum, [jnp.max(s, axis=-1, keepdims=True) for s in scores])
        ps = [jnp.exp2(s - mx) for s in scores]
        den = functools.reduce(lambda a, b: a + b, [jnp.sum(p, axis=-1, keepdims=True) for p in ps])
        acc = functools.reduce(lambda a, b: a + b, [
            _dot(p.astype(BF16), v_ref[:, hh * LANES:(hh + 1) * LANES]) for p, (_, v_ref) in zip(ps, seg)])
        acc = acc * (1.0 / den)
        o = acc[:tq] - lam * acc[tq:]
        o_ref[:, hh * LANES:(hh + 1) * LANES] = (_rms(o, gain_ref[0]) * (1.0 - lam_init)).astype(BF16)


def _diff_attn_call(q, dls, gains, slot, ctx, segs, row0, nb, t, lam_init):
    q0 = row0 // TQ
    per = t // TQ
    n = q.shape[1]
    in_specs = [pl.BlockSpec((TQ, n), lambda b, j: (q0 + b * per + j, 0)),
                pl.BlockSpec((1,) + dls.shape[1:], lambda b, j: (slot, 0, 0)),
                pl.BlockSpec((1, 1, gains.shape[2]), lambda b, j: (slot, 0, 0))]
    args = [q, dls, gains]
    scratch = []
    if ctx is not None:
        nb_c, n_l, past, heads, dh = ctx[0].shape
        for cache in ctx:
            in_specs.append(pl.BlockSpec((1, 1, past * heads, dh), lambda b, j: (b, slot, 0, 0)))
            args.append(cache.reshape(nb_c, n_l, past * heads, dh))
        scratch = [pltpu.VMEM((past, n), BF16), pltpu.VMEM((past, n), BF16)]
    for k, v, r0, s in segs:
        b0 = r0 // s
        in_specs.append(pl.BlockSpec((s, n), lambda b, j, b0=b0: (b0 + b, 0)))
        in_specs.append(pl.BlockSpec((s, n), lambda b, j, b0=b0: (b0 + b, 0)))
        args += [k, v]
    return pl.pallas_call(
        functools.partial(_diff_attn_kernel, nseg=len(segs) + (ctx is not None), has_ctx=ctx is not None,
                          lam_init=lam_init),
        grid=(nb, per),
        in_specs=in_specs,
        out_specs=pl.BlockSpec((TQ, n), lambda b, j: (b * per + j, 0)),
        out_shape=jax.ShapeDtypeStruct((nb * t, n), BF16),
        scratch_shapes=scratch,
        name="diff_attn",
        compiler_params=_cparams(("arbitrary", "arbitrary")),
    )(*args)


def _halo_specs(rows, width, tm, tile):
    per = tm // HALO
    last = rows // HALO - 1
    return [pl.BlockSpec((HALO, width), lambda i: (jnp.maximum(tile(i) * per - 1, 0), 0)),
            pl.BlockSpec((tm, width), lambda i: (tile(i), 0)),
            pl.BlockSpec((HALO, width), lambda i: (jnp.minimum((tile(i) + 1) * per, last), 0))]


def _ffn_kernel(*refs, lay, d_ff, final, nx, widths):
    pos = 3 * nx + 6 * len(widths)
    trips = [refs[3 * t:3 * t + 3] for t in range(pos // 3)]
    mod_ref, gain_ref, wo_ref, wup_ref, cw_ref, cb_ref, wdn_ref, fg_ref = refs[pos:pos + 8]
    outs = refs[pos + 8:len(refs) - 2]
    u_ref, act_ref = refs[len(refs) - 2:]
    tm = lay.tm
    i = pl.program_id(0)
    row = i * tm
    is_prompt = i < lay.pt
    seq = jnp.where(is_prompt, lay.seq_p, lay.seq_s)
    first = (row & (seq - 1)) == 0
    last = ((row + tm) & (seq - 1)) == 0

    def ext(trip_p, trip_s):
        if trip_s is None:
            return jnp.concatenate([r[...] for r in trip_p], axis=0)
        return jnp.concatenate([jnp.where(is_prompt, a[...], b[...]) for a, b in zip(trip_p, trip_s)], axis=0)

    mod = mod_ref[0, 0]
    x_ext = ext(trips[0], trips[1] if nx == 2 else None)
    proj = None
    k0 = 0
    for p, w in enumerate(widths):
        part = _dot(ext(trips[nx + 2 * p], trips[nx + 2 * p + 1]), wo_ref[0, k0:k0 + w, :])
        k0 += w
        proj = part if proj is None else proj + part
    xm = x_ext + mod[2:3] * proj
    gain, shift, scale = gain_ref[0, 1:2], mod[3:4], mod[4:5]
    hh = _normmod(xm, gain, shift, scale)
    h = jnp.concatenate([jnp.where(first, 0.0, hh[HALO - SUBLANES:HALO]), hh[HALO:HALO + tm],
                         jnp.where(last, 0.0, hh[HALO + tm:HALO + tm + SUBLANES])], axis=0).astype(BF16)
    x = xm[HALO:HALO + tm]
    acc = jnp.zeros((tm, x.shape[1]), F32)
    n_f = d_ff // FFN_TF
    n_slots = u_ref.shape[0]
    sub = lax.broadcasted_iota(jnp.int32, (SUBLANES, FFN_TF), 0)
    cut_prev = is_prompt & (sub == 0)
    cut_next = is_prompt & (sub == SUBLANES - 1)
    seams = range(lay.seq_p, tm, lay.seq_p)

    def up(f):
        for half, base in enumerate((0, d_ff)):
            lo = base + f * FFN_TF
            u_ref[f % n_slots, :, half * FFN_TF:(half + 1) * FFN_TF] = _dot(h, wup_ref[0, :, lo:lo + FFN_TF])

    def conv(f, half):
        lo = half * d_ff + f * FFN_TF
        cols = slice(half * FFN_TF, (half + 1) * FFN_TF)
        cw = cw_ref[0, :, lo:lo + FFN_TF]
        prev = u_ref[f % n_slots, SUBLANES - 1:SUBLANES - 1 + tm, cols]
        here = u_ref[f % n_slots, SUBLANES:SUBLANES + tm, cols]
        nxt = u_ref[f % n_slots, SUBLANES + 1:SUBLANES + 1 + tm, cols]
        for r in seams:
            prev = jnp.concatenate([prev[:r], jnp.where(cut_prev, 0.0, prev[r:r + SUBLANES]), prev[r + SUBLANES:]], axis=0)
            nxt = jnp.concatenate([nxt[:r - SUBLANES], jnp.where(cut_next, 0.0, nxt[r - SUBLANES:r]), nxt[r:]], axis=0)
        return prev * cw[0:1] + here * cw[1:2] + nxt * cw[2:3] + cb_ref[0, :, lo:lo + FFN_TF]

    for f in range(min(FFN_AHEAD, n_f)):
        up(f)
    for f in range(n_f):
        act_ref[:, f * FFN_TF:(f + 1) * FFN_TF] = (_silu(conv(f, 0)) * conv(f, 1)).astype(BF16)
        if f + FFN_AHEAD < n_f:
            up(f + FFN_AHEAD)
        if (f + 1) % FFN_DOWN == 0 or f + 1 == n_f:
            k0 = (f // FFN_DOWN) * FFN_DOWN * FFN_TF
            acc = acc + _dot(act_ref[:, k0:(f + 1) * FFN_TF], wdn_ref[0, k0:(f + 1) * FFN_TF, :])
    y = x + mod[5:6] * acc
    if not final:
        outs[0][...] = y
    else:
        yn = _rms(y, fg_ref[...])

        @pl.when(i < lay.pt)
        def _():
            outs[0][...] = yn

        @pl.when(i >= lay.pt)
        def _():
            outs[1][...] = yn


def _ffn_call(lay, x, pairs, mods, l, gains, wo, slot, wup, cw, cb, wdn, fgain, final):
    d = wo.shape[2]
    d_ff = wdn.shape[1]
    assert d_ff % FFN_TF == 0
    tm = lay.tm
    in_specs, args = [], []
    if isinstance(x, tuple):
        streams = [(x[0], lay.p_block), (x[1], lay.s_block)]
    else:
        streams = [(x, lambda i: i)]
    for a_p, a_s in pairs:
        streams += [(a_p, lay.p_block), (a_s, lay.s_block)]
    for arr, tile in streams:
        in_specs += _halo_specs(arr.shape[0], arr.shape[1], tm, tile)
        args += [arr, arr, arr]
    in_specs += [_mod_spec(lay, l, d), _layer_spec(gains.shape, l), _layer_spec(wo.shape, slot), _layer_spec(wup.shape, l),
                 _layer_spec(cw.shape, l), _layer_spec(cb.shape, l), _layer_spec(wdn.shape, l), _resident((1, d))]
    args += [mods, gains, wo, wup, cw, cb, wdn, fgain]
    if final:
        out_specs = [pl.BlockSpec((tm, d), lambda i: (lay.p_block(i), 0)), pl.BlockSpec((tm, d), lambda i: (lay.s_block(i), 0))]
        out_shape = [jax.ShapeDtypeStruct((lay.np, d), F32), jax.ShapeDtypeStruct((lay.ns, d), F32)]
    else:
        out_specs = [pl.BlockSpec((tm, d), lambda i: (i, 0))]
        out_shape = [jax.ShapeDtypeStruct((lay.nt, d), F32)]
    return pl.pallas_call(
        functools.partial(_ffn_kernel, lay=lay, d_ff=d_ff, final=final, nx=2 if isinstance(x, tuple) else 1,
                          widths=[a_p.shape[1] for a_p, _ in pairs]),
        name="ffn",
        grid=(lay.tiles,),
        in_specs=in_specs,
        out_specs=out_specs,
        out_shape=out_shape,
        scratch_shapes=[pltpu.VMEM((FFN_AHEAD + 1, tm + 2 * SUBLANES, 2 * FFN_TF), F32), pltpu.VMEM((tm, d_ff), BF16)],
        compiler_params=_cparams(("arbitrary",)),
    )(*args)


def _rope_tables(seq_s, both_halves):
    t = jnp.arange(seq_s)
    rowp = (t // GRID_W).astype(F32)
    colp = (t % GRID_W).astype(F32)
    half = ROPE_B // 2
    inv = ROPE_BASE ** (-jnp.arange(0, half, 2, dtype=F32) / half)
    ar, ac = rowp[:, None] * inv, colp[:, None] * inv
    zero = jnp.zeros_like(ar)
    cos = jnp.concatenate([jnp.cos(ar), jnp.cos(ar), jnp.cos(ac), jnp.cos(ac)], -1)
    sin_dn = jnp.concatenate([zero, jnp.sin(ar), zero, jnp.sin(ac)], -1)
    sin_up = jnp.concatenate([-jnp.sin(ar), zero, -jnp.sin(ac), zero], -1)
    if both_halves:
        parts = [jnp.concatenate([p, p], -1) for p in (cos, sin_dn, sin_up)]
    else:
        one, zz = jnp.ones_like(cos), jnp.zeros_like(cos)
        parts = [jnp.concatenate([cos, one], -1), jnp.concatenate([sin_dn, zz], -1), jnp.concatenate([sin_up, zz], -1)]
    ident = [jnp.ones((seq_s, LANES), F32), jnp.zeros((seq_s, LANES), F32), jnp.zeros((seq_s, LANES), F32)]
    return jnp.stack([jnp.concatenate([i_, p], 0) for i_, p in zip(ident, parts)], 0)


def _even_weights(w_in, w_dec, b_dec, w_uq):
    d = w_in.shape[0]
    sizes = (H_A * DK_A, H_A * DK_A, H_A * DV_A, H_A * DV_A, GLA_LR, GLA_LR, Q_RANK, KV_RANK, ROPE_B)
    offs = np.concatenate([[0], np.cumsum(sizes)])
    qa, ka, va, ga, lrf, lrb, cq, ckv, kr = [w_in[:, offs[j]:offs[j + 1]] for j in range(9)]
    qa = qa * (DK_A ** -0.5)
    qk = jnp.concatenate([qa.reshape(d, H_A, DK_A), ka.reshape(d, H_A, DK_A)], -1).reshape(d, 2 * H_A * DK_A)
    pad = jnp.zeros((d, LANES - ROPE_B - 2 * GLA_LR), w_in.dtype)
    w = jnp.concatenate([qk, va, ga, cq, ckv, kr, lrf, lrb, pad], -1).astype(BF16)
    wd = jnp.zeros((LANES, H_A, 2, DK_A), F32)
    wd = wd.at[ROPE_B:ROPE_B + GLA_LR, :, 0, :].set(w_dec[0].reshape(GLA_LR, H_A, DK_A))
    wd = wd.at[ROPE_B + GLA_LR:ROPE_B + 2 * GLA_LR, :, 1, :].set(w_dec[1].reshape(GLA_LR, H_A, DK_A))
    wd = wd.reshape(LANES, 2 * H_A * DK_A).astype(BF16)
    bd = jnp.stack([b_dec[0].reshape(H_A, DK_A), b_dec[1].reshape(H_A, DK_A)], 1).reshape(1, 2 * H_A * DK_A)
    wq = w_uq.reshape(Q_RANK, H_B, NOPE_B + ROPE_B)
    wq = jnp.concatenate([wq, jnp.zeros((Q_RANK, H_B, 2 * LANES - NOPE_B - ROPE_B), w_uq.dtype)], -1)
    wq = wq.reshape(Q_RANK, H_B * 2 * LANES).astype(BF16)
    return w, wd, bd, wq


def kernel(x_prompt, x_sample, state_gla, cache_mla_ckv, cache_mla_krope, cache_diff_k, cache_diff_v, c, c_ctx, w_ada, b_ada, norm_gain, final_gain, w_in_even, w_out_even, gla_w_decay, gla_b_decay, gla_norm, mla_q_norm, mla_kv_norm, mla_w_uq, mla_w_ukv, w_in_odd, w_out_odd, diff_lambda, diff_norm, ffn_w_up, ffn_conv_w, ffn_conv_b, ffn_w_down):
    nb_p, seq_p, d = x_prompt.shape
    nb_s, seq_s, _ = x_sample.shape
    past = cache_mla_ckv.shape[2]
    depth = w_ada.shape[0]
    n_even, n_odd = w_in_even.shape[0], w_in_odd.shape[0]
    n_p, n_s = nb_p * seq_p, nb_s * seq_s
    lay = _Layout(n_p, n_s, seq_p, seq_s, TM)
    lay_ffn = _Layout(n_p, n_s, seq_p, seq_s, FFN_TM)
    assert seq_p == TM and seq_s % GRID_W == 0 and 1 + nb_s <= ADA_ROWS
    assert n_p % seq_s == 0 and seq_p & (seq_p - 1) == 0 and seq_s & (seq_s - 1) == 0
    assert seq_p % GLA_C == 0 and seq_s % GLA_C == 0 and seq_p % TQ == 0 and seq_s % TQ == 0

    cond = jnp.concatenate([c_ctx[None, :], c, jnp.zeros((ADA_ROWS - 1 - nb_s, d), F32)], 0)
    mods = _ada_call(cond, w_ada, b_ada).reshape(depth, ADA_ROWS, 6, d)

    tab_mla = _rope_tables(seq_s, both_halves=False)
    tab_diff = _rope_tables(seq_s, both_halves=True)
    gla_consts = _gla_consts(GLA_C)
    wup_all = ffn_w_up.astype(BF16)
    wdn_all = ffn_w_down.astype(BF16)
    cb_all = ffn_conv_b.reshape(depth, 1, -1)
    wo_even = w_out_even.astype(BF16)
    wo_odd = w_out_odd.astype(BF16)
    fgain = final_gain.reshape(1, d)
    dgains = diff_norm.reshape(n_odd, 1, 2 * DH_C)

    x = (x_prompt.reshape(n_p, d), x_sample.reshape(n_s, d))
    st_new = caches_even = caches_odd = None
    for l in range(depth):
        i = l // 2
        if l % 2 == 0:
            slot = _Slot(caches_even, i, n_even)
            w, wd, bd, wq = _even_weights(w_in_even[i], gla_w_decay[i], gla_b_decay[i], mla_w_uq[i])
            wukv = mla_w_ukv[i].astype(BF16)
            qk, la, v, ga, qb, kra, kv, ckv_new, kr_new = _even_proj_call(
                lay, x, mods, l, norm_gain, w, wd, bd, mla_q_norm[i].reshape(1, -1), mla_kv_norm[i].reshape(1, -1),
                wq, wukv, tab_mla, slot, caches_even)
            caches_even = (ckv_new, kr_new)
            ggain = gla_norm[i].reshape(1, DV_A)
            og_p, st_new = _gla_call(qk, la, v, ga, ggain, gla_consts, None, 0, 0, nb_p, seq_p,
                                     _Slot(st_new, i, n_even), st_new)
            og_s, = _gla_call(qk, la, v, ga, ggain, gla_consts, state_gla, i, n_p, nb_s, seq_s, None, None)
            kv_ctx, kr_ctx = _kv_expand_call(cache_mla_ckv, cache_mla_krope, i, wukv)
            om_p = _mla_attn_call(qb, [(kv, kra, 0, seq_p)], 0, nb_p, seq_p)
            om_s = _mla_attn_call(qb, [(kv_ctx, kr_ctx, 0, past), (kv, kra, n_p, seq_s)], n_p, nb_s, seq_s)
            pairs, wo = [(og_p, og_s), (om_p, om_s)], wo_even
        else:
            slot = _Slot(caches_odd, i, n_odd)
            lam_init = 0.8 - 0.6 * math.exp(-0.3 * l)
            n = H_C * 2 * DH_C
            col_scale = np.concatenate([np.full(n, DIFF_SCALE * LOG2E, np.float32), np.ones(2 * n, np.float32)])
            wi = (w_in_odd[i] * col_scale).astype(BF16)
            q, kb, vb, kf_new, vf_new = _odd_proj_call(lay, x, mods, l, norm_gain, wi, tab_diff, slot, caches_odd)
            caches_odd = (kf_new, vf_new)
            od_p = _diff_attn_call(q, diff_lambda, dgains, i, None, [(kb, vb, 0, seq_p)], 0, nb_p, seq_p, lam_init)
            od_s = _diff_attn_call(q, diff_lambda, dgains, i, (cache_diff_k, cache_diff_v), [(kb, vb, n_p, seq_s)],
                                   n_p, nb_s, seq_s, lam_init)
            pairs, wo = [(od_p, od_s)], wo_odd
        x = _ffn_call(lay_ffn, x, pairs, mods, l, norm_gain, wo, i, wup_all, ffn_conv_w, cb_all, wdn_all, fgain,
                      final=(l == depth - 1))
        if l < depth - 1:
            x = x[0]
    y_prompt = x[0].reshape(nb_p, seq_p, d)
    y_sample = x[1].reshape(nb_s, seq_s, d)
    diff_shape = (nb_p, n_odd, seq_p, H_C, 2 * DH_C)
    return (y_prompt, y_sample, st_new, caches_even[0], caches_even[1],
            caches_odd[0].reshape(diff_shape), caches_odd[1].reshape(diff_shape))
```

```python
import functools
import math

import numpy as np
import jax
import jax.numpy as jnp
from jax import lax
from jax.experimental import pallas as pl
from jax.experimental.pallas import tpu as pltpu

GRID_W = 64
ROPE_BASE = 10000.0
EPS = 1e-6
H_A, DK_A, DV_A = 4, 64, 128
GLA_LR = 16
GLA_TAU = 16.0
H_B, Q_RANK, KV_RANK, NOPE_B, ROPE_B, V_B = 4, 256, 256, 128, 64, 128
MLA_SCALE = (NOPE_B + ROPE_B) ** -0.5
H_C, DH_C = 8, 64
DIFF_SCALE = DH_C ** -0.5
LOG2E = math.log2(math.e)

LANES = 128
SUBLANES = 8
HALO = 16
VMEM_LIMIT = 56 * 1024 * 1024

TM = 512
TQ = 256
GLA_C = 64
GLA_STEPS = 1
GLA_HEADS = 4
GLA_UNROLL = 2
ADA_TN = 1536
FFN_TM = 512
FFN_TF = 256
FFN_AHEAD = 2
FFN_DOWN = 4
ADA_ROWS = 16

F32 = jnp.float32
BF16 = jnp.bfloat16


def _dot(a, b):
    return jnp.dot(a, b, preferred_element_type=F32)


def _dot_nt(a, b):
    return lax.dot_general(a, b, (((1,), (1,)), ((), ())), preferred_element_type=F32)


def _silu(x):
    return x * (1.0 / (1.0 + jnp.exp(-x)))


def _rms(x, gain):
    return x * lax.rsqrt(jnp.mean(x * x, axis=-1, keepdims=True) + EPS) * gain


def _normmod(x, gain, shift, scale):
    return _rms(x, gain) * (1.0 + scale) + shift


def _rope(x, cos, sin_dn, sin_up):
    return (x * cos + pltpu.roll(x, 16, axis=1) * sin_dn + pltpu.roll(x, LANES - 16, axis=1) * sin_up)


def _cparams(sem, vmem=VMEM_LIMIT):
    return pltpu.CompilerParams(dimension_semantics=sem, vmem_limit_bytes=vmem)


def _resident(shape):
    nd = len(shape)
    return pl.BlockSpec(shape, lambda *_: (0,) * nd, pipeline_mode=pl.Buffered(1))


def _layer_spec(shape, l):
    nd = len(shape)
    return pl.BlockSpec((1,) + tuple(shape[1:]), lambda *_: (l,) + (0,) * (nd - 1), pipeline_mode=pl.Buffered(1))


def _ada_kernel(c_ref, w_ref, b_ref, o_ref):
    s = _silu(c_ref[...]).astype(BF16)
    o_ref[0] = _dot(s, w_ref[0].astype(BF16)) + b_ref[0]


def _ada_call(cond, w_ada, b_ada):
    depth, d, n = w_ada.shape
    return pl.pallas_call(
        _ada_kernel,
        grid=(depth, n // ADA_TN),
        in_specs=[
            pl.BlockSpec((ADA_ROWS, d), lambda l, j: (0, 0)),
            pl.BlockSpec((1, d, ADA_TN), lambda l, j: (l, 0, j)),
            pl.BlockSpec((1, 1, ADA_TN), lambda l, j: (l, 0, j)),
        ],
        out_specs=pl.BlockSpec((1, ADA_ROWS, ADA_TN), lambda l, j: (l, 0, j)),
        out_shape=jax.ShapeDtypeStruct((depth, ADA_ROWS, n), F32),
        name="ada",
        compiler_params=_cparams(("arbitrary", "arbitrary")),
    )(cond, w_ada, b_ada.reshape(depth, 1, n))


class _Layout:
    def __init__(self, n_prompt, n_sample, seq_p, seq_s, tm):
        assert tm % seq_p == 0 and n_prompt % tm == 0 and n_sample % tm == 0 and seq_s % tm == 0
        self.tm = tm
        self.np, self.ns, self.nt = n_prompt, n_sample, n_prompt + n_sample
        self.seq_p, self.seq_s = seq_p, seq_s
        self.pt = n_prompt // tm
        self.st = n_sample // tm
        self.tiles = self.pt + self.st
        self.tiles_per_seq_s = seq_s // tm

    def group(self, i):
        return jnp.where(i < self.pt, 0, 1 + (i - self.pt) // self.tiles_per_seq_s)

    def pos_block(self, i):
        return jnp.where(i < self.pt, 0, self.tiles_per_seq_s + (i - self.pt) % self.tiles_per_seq_s)

    def p_block(self, i):
        return jnp.minimum(i, self.pt - 1)

    def s_block(self, i):
        return jnp.maximum(i - self.pt, 0)


def _mod_spec(lay, l, d):
    return pl.BlockSpec((1, 1, 6, d), lambda i: (l, lay.group(i), 0, 0))


def _tab_spec(lay):
    return pl.BlockSpec((3, lay.tm, LANES), lambda i: (0, lay.pos_block(i), 0))


def _x_specs(lay, x):
    if isinstance(x, tuple):
        d = x[0].shape[1]
        return [pl.BlockSpec((lay.tm, d), lambda i: (lay.p_block(i), 0)),
                pl.BlockSpec((lay.tm, d), lambda i: (lay.s_block(i), 0))], list(x)
    return [pl.BlockSpec((lay.tm, x.shape[1]), lambda i: (i, 0))], [x]


def _read_x(x_refs, prompt_tiles):
    if len(x_refs) == 1:
        return x_refs[0][...]
    return jnp.where(pl.program_id(0) < prompt_tiles, x_refs[0][...], x_refs[1][...])


class _Slot:
    def __init__(self, prev, slot, n_slots, per_block=1):
        self.prev, self.slot, self.n_slots, self.per_block = prev, slot, n_slots, per_block

    @property
    def first(self):
        return self.prev is None

    def spec(self, tail, index):
        zeros = (0,) * len(tail)
        if self.first:
            return pl.BlockSpec((self.per_block, self.n_slots) + tail, lambda *g: (index(*g), 0) + zeros)
        s = self.slot
        return pl.BlockSpec((self.per_block, 1) + tail, lambda *g: (index(*g), s) + zeros)

    def shape(self, nb, tail):
        return jax.ShapeDtypeStruct((nb, self.n_slots) + tail, F32)

    def at(self):
        return self.slot if self.first else 0

    def zero_others(self, ref):
        if self.first:
            for b in range(self.per_block):
                for s in range(self.n_slots):
                    if s != self.slot:
                        ref[b, s] = jnp.zeros(ref.shape[2:], F32)


def _even_proj_kernel(*refs, nx, prompt_tiles, slot):
    x_refs = refs[:nx]
    (mod_ref, gain_ref, w_ref, wd_ref, bd_ref, qn_ref, kvn_ref, wuq_ref, wukv_ref, tab_ref) = refs[nx:nx + 10]
    outs = refs[len(refs) - 9:]
    qk_ref, la_ref, v_ref, ga_ref, qb_ref, kra_ref, kv_ref, ckv_ref, kr_ref = outs
    mod = mod_ref[0, 0]
    seq = ckv_ref.shape[2]
    parts = [slice(b * seq, (b + 1) * seq) for b in range(slot.per_block)]
    x = _read_x(x_refs, prompt_tiles)
    ys = [_dot(_normmod(x[r], gain_ref[0, 0:1], mod[0:1], mod[1:2]).astype(BF16), w_ref[...]) for r in parts]
    ckvns, gs = [], []
    for r, y in zip(parts, ys):
        qk_ref[r, :] = y[:, 0:512]
        v_ref[r, :] = y[:, 512:1024].astype(BF16)
        ga_ref[r, :] = y[:, 1024:1536]
        cq, ckv, g = y[:, 1536:1792], y[:, 1792:2048], y[:, 2048:2176]
        pre = _dot(g.astype(BF16), wd_ref[...]) + bd_ref[...]
        la_ref[r, :] = (jnp.minimum(pre, 0.0) - jnp.log(1.0 + jnp.exp(-jnp.abs(pre)))) * (LOG2E / GLA_TAU)
        cos, sin_dn, sin_up = tab_ref[0, r, :], tab_ref[1, r, :], tab_ref[2, r, :]
        qb = _dot(_rms(cq, qn_ref[...]).astype(BF16), wuq_ref[...])
        for hh in range(H_B):
            lo = hh * 2 * LANES
            qb_ref[r, lo:lo + LANES] = qb[:, lo:lo + LANES].astype(BF16)
            qb_ref[r, lo + LANES:lo + 2 * LANES] = _rope(qb[:, lo + LANES:lo + 2 * LANES], cos, sin_dn, sin_up).astype(BF16)
        ckvn = _rms(ckv, kvn_ref[...])
        kv_ref[r, :] = _dot(ckvn.astype(BF16), wukv_ref[...]).astype(BF16)
        lane = lax.broadcasted_iota(jnp.int32, g.shape, 1)
        kra_ref[r, :] = jnp.where(lane < ROPE_B, _rope(g, cos, sin_dn, sin_up), 0.0).astype(BF16)
        ckvns.append(ckvn)
        gs.append(g)

    @pl.when(pl.program_id(0) < prompt_tiles)
    def _():
        for b in range(slot.per_block):
            ckv_ref[b, slot.at()] = ckvns[b]
            kr_ref[b, slot.at()] = gs[b][:, :ROPE_B]
        slot.zero_others(ckv_ref)
        slot.zero_others(kr_ref)


def _even_proj_call(lay, x, mods, l, gains, w, wd, bd, qn, kvn, wuq, wukv, tab, slot, prev):
    d = w.shape[0]
    x_specs, x_args = _x_specs(lay, x)
    row = lambda n: pl.BlockSpec((TM, n), lambda i: (i, 0))
    outs = [(512, F32), (512, F32), (512, BF16), (512, F32), (1024, BF16), (LANES, BF16), (1024, BF16)]
    in_specs = x_specs + [_mod_spec(lay, l, d), _layer_spec(gains.shape, l), _resident(w.shape), _resident(wd.shape),
                          _resident(bd.shape), _resident(qn.shape), _resident(kvn.shape), _resident(wuq.shape),
                          _resident(wukv.shape), _tab_spec(lay)]
    args = x_args + [mods, gains, w, wd, bd, qn, kvn, wuq, wukv, tab]
    aliases = {}
    if not slot.first:
        aliases = {len(args): len(outs), len(args) + 1: len(outs) + 1}
        in_specs += [pl.BlockSpec(memory_space=pl.ANY), pl.BlockSpec(memory_space=pl.ANY)]
        args += list(prev)
    nb_p = lay.np // lay.seq_p
    return pl.pallas_call(
        functools.partial(_even_proj_kernel, nx=len(x_args), prompt_tiles=lay.pt, slot=slot),
        grid=(lay.tiles,),
        in_specs=in_specs,
        out_specs=[row(n) for n, _ in outs]
        + [slot.spec((lay.seq_p, KV_RANK), lay.p_block), slot.spec((lay.seq_p, ROPE_B), lay.p_block)],
        out_shape=[jax.ShapeDtypeStruct((lay.nt, n), dt) for n, dt in outs]
        + [slot.shape(nb_p, (lay.seq_p, KV_RANK)), slot.shape(nb_p, (lay.seq_p, ROPE_B))],
        input_output_aliases=aliases,
        name="even_proj",
        compiler_params=_cparams(("arbitrary",)),
    )(*args)


def _gla_consts(c):
    levels = int(math.log2(c))
    idx = np.arange(c)
    cum = idx[None, :] <= idx[:, None]
    rows, sgn, masks = [cum], [], []
    for lev in range(levels):
        half = 1 << lev
        node = idx // (2 * half)
        mid = node * 2 * half + half
        upper = idx >= mid
        rows.append(cum[mid - 1])
        sgn.append(np.repeat(np.where(upper, 1.0, -1.0)[:, None], LANES, 1))
        same = node[:, None] == node[None, :]
        masks.append(np.concatenate([same & upper[:, None] & ~upper[None, :],
                                     same & ~upper[:, None] & upper[None, :]], 1))
    eye = np.eye(c, dtype=bool)
    masks.append(np.concatenate([eye, eye], 1))
    m = np.concatenate(rows, 0)
    return (jnp.asarray(np.concatenate([m, m, m], 1), BF16), jnp.asarray(np.stack(sgn, 0), F32),
            jnp.asarray(np.stack(masks, 0), F32), levels)


def _gla_kernel(*refs, t, has_s0, slot, levels):
    qk_ref, la_ref, v_ref, ga_ref, gain_ref, m3_ref, sg_ref, mask_ref = refs[:8]
    s0_ref = refs[8] if has_s0 else None
    n_out = 1 if slot is None else 2
    o_ref = refs[len(refs) - 3 - n_out]
    st_ref = None if slot is None else refs[len(refs) - 4]
    of_ref, ob_ref, st_scr = refs[len(refs) - 3:]
    c = GLA_C
    n = t // c
    heads = range(H_A)
    lane = lax.broadcasted_iota(jnp.int32, (1, LANES), 1)
    is_fwd = lane < DK_A
    keep_f = jnp.where(is_fwd, 1.0, 0.0).astype(BF16)
    keep_b = jnp.where(is_fwd, 0.0, 1.0).astype(BF16)

    def split(x):
        return jnp.concatenate([x * keep_f, x * keep_b], axis=0)

    for h in heads:
        if has_s0:
            st_scr[h] = jnp.concatenate([jnp.transpose(s0_ref[0, 0, 0, h]), jnp.transpose(s0_ref[0, 0, 1, h])], axis=1)
        else:
            st_scr[h] = jnp.zeros((DV_A, LANES), F32)

    steps = GLA_STEPS if n % GLA_STEPS == 0 else 1

    def body(j, carry):
        rf = [pl.multiple_of((j * steps + s) * c, c) for s in range(steps)]
        rb = [pl.multiple_of((n - 1 - j * steps - s) * c, c) for s in range(steps)]
        sl = [slice(h * LANES, (h + 1) * LANES) for h in heads]
        for g in range(0, H_A, GLA_HEADS):
            process([(s, h) for s in range(steps) for h in heads[g:g + GLA_HEADS]], rf, rb, sl)
        return carry

    def process(units, rf, rb, sl):
        qq, kk, ll, split3 = [], [], [], []
        for s, h in units:
            qkf, qkb = qk_ref[pl.ds(rf[s], c), sl[h]], qk_ref[pl.ds(rb[s], c), sl[h]]
            qq.append(jnp.where(is_fwd, qkf, pltpu.roll(qkb, DK_A, axis=1)))
            kk.append(jnp.where(is_fwd, pltpu.roll(qkf, DK_A, axis=1), qkb))
            ll.append(jnp.where(is_fwd, la_ref[pl.ds(rf[s], c), sl[h]], la_ref[pl.ds(rb[s], c), sl[h]]))
            hi = ll[-1].astype(BF16)
            r1 = ll[-1] - hi.astype(F32)
            mid = r1.astype(BF16)
            lo = (r1 - mid.astype(F32)).astype(BF16)
            split3.append(jnp.concatenate([hi, mid, lo], axis=0))
        e_all = _dot(m3_ref[...], jnp.concatenate(split3, axis=1))
        e = [e_all[:, k * LANES:(k + 1) * LANES] for k in range(len(units))]
        cx = [jnp.where(is_fwd, e[k][0:c], e[k][0:c] - ll[k]) for k in range(len(units))]
        a = [_dot_nt(qq[k].astype(BF16), split(kk[k].astype(BF16))) * mask_ref[levels] for k in range(len(units))]
        for lev in range(levels):
            for k in range(len(units)):
                ex = jnp.exp2(sg_ref[lev] * (cx[k] - e[k][(lev + 1) * c:(lev + 2) * c]))
                a[k] = a[k] + _dot_nt((qq[k] * ex).astype(BF16), split((kk[k] * ex).astype(BF16))) * mask_ref[lev]
        for k, (s, h) in enumerate(units):
            tot = e[k][c - 1:c]
            u = jnp.exp2(cx[k])
            w = jnp.exp2(tot - cx[k])
            qd = (qq[k] * jnp.where(is_fwd, u, w)).astype(BF16)
            kd = (kk[k] * jnp.where(is_fwd, w, u)).astype(BF16)
            vs = jnp.concatenate([v_ref[pl.ds(rf[s], c), sl[h]], v_ref[pl.ds(rb[s], c), sl[h]]], axis=0)
            st = st_scr[h]
            o = _dot(split(a[k].astype(BF16)), vs) + _dot_nt(split(qd), st.astype(BF16))
            of_ref[pl.ds(rf[s], c), sl[h]] = o[:c]
            ob_ref[pl.ds(rb[s], c), sl[h]] = o[c:]
            vt = jnp.transpose(vs.astype(F32)).astype(BF16)
            st_scr[h] = st * jnp.exp2(tot) + _dot(vt, split(kd))

    lax.fori_loop(0, n // steps, body, 0, unroll=GLA_UNROLL)
    for h in heads:
        sl = slice(h * LANES, (h + 1) * LANES)
        o = _rms(of_ref[:, sl] + ob_ref[:, sl], gain_ref[...]) * _silu(ga_ref[:, sl])
        o_ref[:, sl] = o.astype(BF16)
        if slot is not None:
            st = st_scr[h]
            st_ref[0, slot.at(), 0, h] = jnp.transpose(st[:, :DK_A])
            st_ref[0, slot.at(), 1, h] = jnp.transpose(st[:, DK_A:])
    if slot is not None:
        slot.zero_others(st_ref)


def _gla_call(qk, la, v, ga, gain, consts, s0, s0_slot, row0, nb, t, slot, prev):
    m3, sg, mask, levels = consts
    blk0 = row0 // t
    width = H_A * LANES
    tok = pl.BlockSpec((t, width), lambda b: (blk0 + b, 0))
    in_specs = [tok, tok, tok, tok, _resident(gain.shape), _resident(m3.shape), _resident(sg.shape), _resident(mask.shape)]
    args = [qk, la, v, ga, gain, m3, sg, mask]
    tail = (2, H_A, DK_A, DV_A)
    if s0 is not None:
        in_specs.append(pl.BlockSpec((1, 1) + tail, lambda b: (b, s0_slot, 0, 0, 0, 0)))
        args.append(s0)
    out_specs = [pl.BlockSpec((t, width), lambda b: (b, 0))]
    out_shape = [jax.ShapeDtypeStruct((nb * t, width), BF16)]
    aliases = {}
    if slot is not None:
        out_specs.append(slot.spec(tail, lambda b: b))
        out_shape.append(slot.shape(nb, tail))
        if not slot.first:
            aliases = {len(args): 1}
            in_specs.append(pl.BlockSpec(memory_space=pl.ANY))
            args.append(prev)
    return pl.pallas_call(
        functools.partial(_gla_kernel, t=t, has_s0=s0 is not None, slot=slot, levels=levels),
        grid=(nb,),
        in_specs=in_specs,
        out_specs=out_specs,
        out_shape=out_shape,
        scratch_shapes=[pltpu.VMEM((t, width), F32), pltpu.VMEM((t, width), F32), pltpu.VMEM((H_A, DV_A, LANES), F32)],
        input_output_aliases=aliases,
        name="gla",
        compiler_params=_cparams(("arbitrary",)),
    )(*args)


def _kv_expand_kernel(c_ref, r_ref, w_ref, o_ref, ro_ref):
    o_ref[...] = _dot(c_ref[0, 0].astype(BF16), w_ref[...]).astype(BF16)
    kr = r_ref[0, 0]
    ro_ref[...] = jnp.concatenate([kr, jnp.zeros((kr.shape[0], LANES - ROPE_B), F32)], axis=1).astype(BF16)


def _kv_expand_call(cache_ckv, cache_kr, slot, wukv):
    nb, _, past, r = cache_ckv.shape
    return pl.pallas_call(
        _kv_expand_kernel,
        grid=(nb,),
        in_specs=[pl.BlockSpec((1, 1, past, r), lambda b: (b, slot, 0, 0)),
                  pl.BlockSpec((1, 1, past, ROPE_B), lambda b: (b, slot, 0, 0)), _resident(wukv.shape)],
        out_specs=[pl.BlockSpec((past, wukv.shape[1]), lambda b: (b, 0)), pl.BlockSpec((past, LANES), lambda b: (b, 0))],
        out_shape=[jax.ShapeDtypeStruct((nb * past, wukv.shape[1]), BF16), jax.ShapeDtypeStruct((nb * past, LANES), BF16)],
        name="kv_expand",
        compiler_params=_cparams(("arbitrary",)),
    )(cache_ckv, cache_kr, wukv)


def _mla_attn_kernel(*refs, nseg):
    q_ref = refs[0]
    seg = [(refs[1 + 2 * s], refs[2 + 2 * s]) for s in range(nseg)]
    o_ref = refs[1 + 2 * nseg]
    def qk(hh):
        q = q_ref[:, hh * 2 * LANES:(hh + 1) * 2 * LANES]
        scores = []
        for kv_ref, kr_ref in seg:
            kcat = jnp.concatenate([kv_ref[:, hh * 2 * LANES:hh * 2 * LANES + LANES], kr_ref[...]], axis=1)
            scores.append(_dot_nt(q, kcat) * (MLA_SCALE * LOG2E))
        return scores

    nxt = qk(0)
    for hh in range(H_B):
        scores = nxt
        if hh + 1 < H_B:
            nxt = qk(hh + 1)
        mx = functools.reduce(jnp.maximum, [jnp.max(s, axis=-1, keepdims=True) for s in scores])
        ps = [jnp.exp2(s - mx) for s in scores]
        den = functools.reduce(lambda a, b: a + b, [jnp.sum(p, axis=-1, keepdims=True) for p in ps])
        acc = functools.reduce(lambda a, b: a + b, [
            _dot(p.astype(BF16), kv_ref[:, hh * 2 * LANES + LANES:(hh + 1) * 2 * LANES])
            for p, (kv_ref, _) in zip(ps, seg)])
        o_ref[:, hh * LANES:(hh + 1) * LANES] = (acc * (1.0 / den)).astype(BF16)


def _mla_attn_call(qb, segs, row0, nb, t):
    q0 = row0 // TQ
    per = t // TQ
    in_specs = [pl.BlockSpec((TQ, qb.shape[1]), lambda b, j: (q0 + b * per + j, 0))]
    args = [qb]
    for kv, kr, r0, s in segs:
        b0 = r0 // s
        in_specs.append(pl.BlockSpec((s, kv.shape[1]), lambda b, j, b0=b0: (b0 + b, 0)))
        in_specs.append(pl.BlockSpec((s, LANES), lambda b, j, b0=b0: (b0 + b, 0)))
        args += [kv, kr]
    return pl.pallas_call(
        functools.partial(_mla_attn_kernel, nseg=len(segs)),
        grid=(nb, per),
        in_specs=in_specs,
        out_specs=pl.BlockSpec((TQ, H_B * V_B), lambda b, j: (b * per + j, 0)),
        out_shape=jax.ShapeDtypeStruct((nb * t, H_B * V_B), BF16),
        name="mla_attn",
        compiler_params=_cparams(("arbitrary", "arbitrary")),
    )(*args)


def _odd_proj_kernel(*refs, prompt_tiles, slot):
    x_ref, mod_ref, gain_ref, w_ref, tab_ref = refs[:5]
    q_ref, kb_ref, vb_ref, kf_ref, vf_ref = refs[len(refs) - 5:]
    mod = mod_ref[0, 0]
    n = q_ref.shape[1]
    heads = n // LANES
    seq = kf_ref.shape[2] // heads
    parts = [slice(b * seq, (b + 1) * seq) for b in range(slot.per_block)]
    x = x_ref[...]
    ys = [_dot(_normmod(x[r], gain_ref[0, 0:1], mod[0:1], mod[1:2]).astype(BF16), w_ref[...]) for r in parts]
    for r, y in zip(parts, ys):
        cos, sin_dn, sin_up = tab_ref[0, r, :], tab_ref[1, r, :], tab_ref[2, r, :]
        vb_ref[r, :] = y[:, 2 * n:3 * n].astype(BF16)
        for g in range(heads):
            lo = g * LANES
            q_ref[r, lo:lo + LANES] = _rope(y[:, lo:lo + LANES], cos, sin_dn, sin_up).astype(BF16)
            kb_ref[r, lo:lo + LANES] = _rope(y[:, n + lo:n + lo + LANES], cos, sin_dn, sin_up).astype(BF16)

    @pl.when(pl.program_id(0) < prompt_tiles)
    def _():
        for b in range(slot.per_block):
            for g in range(heads):
                lo = g * LANES
                kf_ref[b, slot.at(), pl.ds(g, seq, stride=heads), :] = ys[b][:, n + lo:n + lo + LANES]
                vf_ref[b, slot.at(), pl.ds(g, seq, stride=heads), :] = ys[b][:, 2 * n + lo:2 * n + lo + LANES]
        slot.zero_others(kf_ref)
        slot.zero_others(vf_ref)


def _odd_proj_call(lay, x, mods, l, gains, w, tab, slot, prev):
    d = x.shape[1]
    n = w.shape[1] // 3
    row = lambda m: pl.BlockSpec((TM, m), lambda i: (i, 0))
    in_specs = [row(d), _mod_spec(lay, l, d), _layer_spec(gains.shape, l), _resident(w.shape), _tab_spec(lay)]
    args = [x, mods, gains, w, tab]
    aliases = {}
    if not slot.first:
        aliases = {len(args): 3, len(args) + 1: 4}
        in_specs += [pl.BlockSpec(memory_space=pl.ANY), pl.BlockSpec(memory_space=pl.ANY)]
        args += list(prev)
    tail = (lay.seq_p * H_C, 2 * DH_C)
    nb_p = lay.np // lay.seq_p
    return pl.pallas_call(
        functools.partial(_odd_proj_kernel, prompt_tiles=lay.pt, slot=slot),
        grid=(lay.tiles,),
        in_specs=in_specs,
        out_specs=[row(n), row(n), row(n), slot.spec(tail, lay.p_block), slot.spec(tail, lay.p_block)],
        out_shape=[jax.ShapeDtypeStruct((lay.nt, n), BF16)] * 3 + [slot.shape(nb_p, tail)] * 2,
        input_output_aliases=aliases,
        name="odd_proj",
        compiler_params=_cparams(("arbitrary",)),
    )(*args)


def _diff_attn_kernel(*refs, nseg, has_ctx, lam_init):
    q_ref, dl_ref, gain_ref = refs[0], refs[1], refs[2]
    seg = [(refs[3 + 2 * s], refs[4 + 2 * s]) for s in range(nseg)]
    o_ref = refs[3 + 2 * nseg]
    if has_ctx:
        (kc_ref, vc_ref), (kc_scr, vc_scr) = seg[0], refs[4 + 2 * nseg:6 + 2 * nseg]

        @pl.when(pl.program_id(1) == 0)
        def _():
            for hh in range(H_C):
                rows = pl.ds(hh, kc_scr.shape[0], stride=H_C)
                kc_scr[:, hh * LANES:(hh + 1) * LANES] = kc_ref[0, 0, rows, :].astype(BF16)
                vc_scr[:, hh * LANES:(hh + 1) * LANES] = vc_ref[0, 0, rows, :].astype(BF16)

        seg[0] = (kc_scr, vc_scr)
    dl = dl_ref[0]
    lam = (jnp.exp(jnp.sum(dl[0:1] * dl[1:2], axis=-1, keepdims=True))
           - jnp.exp(jnp.sum(dl[2:3] * dl[3:4], axis=-1, keepdims=True)) + lam_init)
    tq = q_ref.shape[0]
    lane = lax.broadcasted_iota(jnp.int32, (1, LANES), 1)
    keep1 = jnp.where(lane < DH_C, 1.0, 0.0).astype(BF16)
    keep2 = jnp.where(lane < DH_C, 0.0, 1.0).astype(BF16)
    def qk(hh):
        q = q_ref[:, hh * LANES:(hh + 1) * LANES]
        qq = jnp.concatenate([q * keep1, q * keep2], axis=0)
        return [_dot_nt(qq, k_ref[:, hh * LANES:(hh + 1) * LANES]) for k_ref, _ in seg]

    nxt = qk(0)
    for hh in range(H_C):
        scores = nxt
        if hh + 1 < H_C:
            nxt = qk(hh + 1)
        mx = functools.reduce(jnp.maximum, [jnp.max(s, axis=-1, keepdims=True) for s in scores])
        ps = [jnp.exp2(s - mx) for s in scores]
        den = functools.reduce(lambda a, b: a + b, [jnp.sum(p, axis=-1, keepdims=True) for p in ps])
        acc = functools.reduce(lambda a, b: a + b, [
            _dot(p.astype(BF16), v_ref[:, hh * LANES:(hh + 1) * LANES]) for p, (_, v_ref) in zip(ps, seg)])
        acc = acc * (1.0 / den)
        o = acc[:tq] - lam * acc[tq:]
        o_ref[:, hh * LANES:(hh + 1) * LANES] = (_rms(o, gain_ref[0]) * (1.0 - lam_init)).astype(BF16)


def _diff_attn_call(q, dls, gains, slot, ctx, segs, row0, nb, t, lam_init):
    q0 = row0 // TQ
    per = t // TQ
    n = q.shape[1]
    in_specs = [pl.BlockSpec((TQ, n), lambda b, j: (q0 + b * per + j, 0)),
                pl.BlockSpec((1,) + dls.shape[1:], lambda b, j: (slot, 0, 0)),
                pl.BlockSpec((1, 1, gains.shape[2]), lambda b, j: (slot, 0, 0))]
    args = [q, dls, gains]
    scratch = []
    if ctx is not None:
        nb_c, n_l, past, heads, dh = ctx[0].shape
        for cache in ctx:
            in_specs.append(pl.BlockSpec((1, 1, past * heads, dh), lambda b, j: (b, slot, 0, 0)))
            args.append(cache.reshape(nb_c, n_l, past * heads, dh))
        scratch = [pltpu.VMEM((past, n), BF16), pltpu.VMEM((past, n), BF16)]
    for k, v, r0, s in segs:
        b0 = r0 // s
        in_specs.append(pl.BlockSpec((s, n), lambda b, j, b0=b0: (b0 + b, 0)))
        in_specs.append(pl.BlockSpec((s, n), lambda b, j, b0=b0: (b0 + b, 0)))
        args += [k, v]
    return pl.pallas_call(
        functools.partial(_diff_attn_kernel, nseg=len(segs) + (ctx is not None), has_ctx=ctx is not None,
                          lam_init=lam_init),
        grid=(nb, per),
        in_specs=in_specs,
        out_specs=pl.BlockSpec((TQ, n), lambda b, j: (b * per + j, 0)),
        out_shape=jax.ShapeDtypeStruct((nb * t, n), BF16),
        scratch_shapes=scratch,
        name="diff_attn",
        compiler_params=_cparams(("arbitrary", "arbitrary")),
    )(*args)


def _halo_specs(rows, width, tm, tile):
    per = tm // HALO
    last = rows // HALO - 1
    return [pl.BlockSpec((HALO, width), lambda i: (jnp.maximum(tile(i) * per - 1, 0), 0)),
            pl.BlockSpec((tm, width), lambda i: (tile(i), 0)),
            pl.BlockSpec((HALO, width), lambda i: (jnp.minimum((tile(i) + 1) * per, last), 0))]


def _ffn_kernel(*refs, lay, d_ff, final, nx, widths):
    pos = 3 * nx + 6 * len(widths)
    trips = [refs[3 * t:3 * t + 3] for t in range(pos // 3)]
    mod_ref, gain_ref, wo_ref, wup_ref, cw_ref, cb_ref, wdn_ref, fg_ref = refs[pos:pos + 8]
    outs = refs[pos + 8:len(refs) - 2]
    u_ref, act_ref = refs[len(refs) - 2:]
    tm = lay.tm
    i = pl.program_id(0)
    row = i * tm
    is_prompt = i < lay.pt
    seq = jnp.where(is_prompt, lay.seq_p, lay.seq_s)
    first = (row & (seq - 1)) == 0
    last = ((row + tm) & (seq - 1)) == 0

    def ext(trip_p, trip_s):
        if trip_s is None:
            return jnp.concatenate([r[...] for r in trip_p], axis=0)
        return jnp.concatenate([jnp.where(is_prompt, a[...], b[...]) for a, b in zip(trip_p, trip_s)], axis=0)

    mod = mod_ref[0, 0]
    x_ext = ext(trips[0], trips[1] if nx == 2 else None)
    proj = None
    k0 = 0
    for p, w in enumerate(widths):
        part = _dot(ext(trips[nx + 2 * p], trips[nx + 2 * p + 1]), wo_ref[0, k0:k0 + w, :])
        k0 += w
        proj = part if proj is None else proj + part
    xm = x_ext + mod[2:3] * proj
    gain, shift, scale = gain_ref[0, 1:2], mod[3:4], mod[4:5]
    hh = _normmod(xm, gain, shift, scale)
    h = jnp.concatenate([jnp.where(first, 0.0, hh[HALO - SUBLANES:HALO]), hh[HALO:HALO + tm],
                         jnp.where(last, 0.0, hh[HALO + tm:HALO + tm + SUBLANES])], axis=0).astype(BF16)
    x = xm[HALO:HALO + tm]
    acc = jnp.zeros((tm, x.shape[1]), F32)
    n_f = d_ff // FFN_TF
    n_slots = u_ref.shape[0]
    sub = lax.broadcasted_iota(jnp.int32, (SUBLANES, FFN_TF), 0)
    cut_prev = is_prompt & (sub == 0)
    cut_next = is_prompt & (sub == SUBLANES - 1)
    seams = range(lay.seq_p, tm, lay.seq_p)

    def up(f):
        for half, base in enumerate((0, d_ff)):
            lo = base + f * FFN_TF
            u_ref[f % n_slots, :, half * FFN_TF:(half + 1) * FFN_TF] = _dot(h, wup_ref[0, :, lo:lo + FFN_TF])

    def conv(f, half):
        lo = half * d_ff + f * FFN_TF
        cols = slice(half * FFN_TF, (half + 1) * FFN_TF)
        cw = cw_ref[0, :, lo:lo + FFN_TF]
        prev = u_ref[f % n_slots, SUBLANES - 1:SUBLANES - 1 + tm, cols]
        here = u_ref[f % n_slots, SUBLANES:SUBLANES + tm, cols]
        nxt = u_ref[f % n_slots, SUBLANES + 1:SUBLANES + 1 + tm, cols]
        for r in seams:
            prev = jnp.concatenate([prev[:r], jnp.where(cut_prev, 0.0, prev[r:r + SUBLANES]), prev[r + SUBLANES:]], axis=0)
            nxt = jnp.concatenate([nxt[:r - SUBLANES], jnp.where(cut_next, 0.0, nxt[r - SUBLANES:r]), nxt[r:]], axis=0)
        return prev * cw[0:1] + here * cw[1:2] + nxt * cw[2:3] + cb_ref[0, :, lo:lo + FFN_TF]

    for f in range(min(FFN_AHEAD, n_f)):
        up(f)
    for f in range(n_f):
        act_ref[:, f * FFN_TF:(f + 1) * FFN_TF] = (_silu(conv(f, 0)) * conv(f, 1)).astype(BF16)
        if f + FFN_AHEAD < n_f:
            up(f + FFN_AHEAD)
        if (f + 1) % FFN_DOWN == 0 or f + 1 == n_f:
            k0 = (f // FFN_DOWN) * FFN_DOWN * FFN_TF
            acc = acc + _dot(act_ref[:, k0:(f + 1) * FFN_TF], wdn_ref[0, k0:(f + 1) * FFN_TF, :])
    y = x + mod[5:6] * acc
    if not final:
        outs[0][...] = y
    else:
        yn = _rms(y, fg_ref[...])

        @pl.when(i < lay.pt)
        def _():
            outs[0][...] = yn

        @pl.when(i >= lay.pt)
        def _():
            outs[1][...] = yn


def _ffn_call(lay, x, pairs, mods, l, gains, wo, slot, wup, cw, cb, wdn, fgain, final):
    d = wo.shape[2]
    d_ff = wdn.shape[1]
    assert d_ff % FFN_TF == 0
    tm = lay.tm
    in_specs, args = [], []
    if isinstance(x, tuple):
        streams = [(x[0], lay.p_block), (x[1], lay.s_block)]
    else:
        streams = [(x, lambda i: i)]
    for a_p, a_s in pairs:
        streams += [(a_p, lay.p_block), (a_s, lay.s_block)]
    for arr, tile in streams:
        in_specs += _halo_specs(arr.shape[0], arr.shape[1], tm, tile)
        args += [arr, arr, arr]
    in_specs += [_mod_spec(lay, l, d), _layer_spec(gains.shape, l), _layer_spec(wo.shape, slot), _layer_spec(wup.shape, l),
                 _layer_spec(cw.shape, l), _layer_spec(cb.shape, l), _layer_spec(wdn.shape, l), _resident((1, d))]
    args += [mods, gains, wo, wup, cw, cb, wdn, fgain]
    if final:
        out_specs = [pl.BlockSpec((tm, d), lambda i: (lay.p_block(i), 0)), pl.BlockSpec((tm, d), lambda i: (lay.s_block(i), 0))]
        out_shape = [jax.ShapeDtypeStruct((lay.np, d), F32), jax.ShapeDtypeStruct((lay.ns, d), F32)]
    else:
        out_specs = [pl.BlockSpec((tm, d), lambda i: (i, 0))]
        out_shape = [jax.ShapeDtypeStruct((lay.nt, d), F32)]
    return pl.pallas_call(
        functools.partial(_ffn_kernel, lay=lay, d_ff=d_ff, final=final, nx=2 if isinstance(x, tuple) else 1,
                          widths=[a_p.shape[1] for a_p, _ in pairs]),
        name="ffn",
        grid=(lay.tiles,),
        in_specs=in_specs,
        out_specs=out_specs,
        out_shape=out_shape,
        scratch_shapes=[pltpu.VMEM((FFN_AHEAD + 1, tm + 2 * SUBLANES, 2 * FFN_TF), F32), pltpu.VMEM((tm, d_ff), BF16)],
        compiler_params=_cparams(("arbitrary",)),
    )(*args)


def _rope_tables(seq_s, both_halves):
    t = jnp.arange(seq_s)
    rowp = (t // GRID_W).astype(F32)
    colp = (t % GRID_W).astype(F32)
    half = ROPE_B // 2
    inv = ROPE_BASE ** (-jnp.arange(0, half, 2, dtype=F32) / half)
    ar, ac = rowp[:, None] * inv, colp[:, None] * inv
    zero = jnp.zeros_like(ar)
    cos = jnp.concatenate([jnp.cos(ar), jnp.cos(ar), jnp.cos(ac), jnp.cos(ac)], -1)
    sin_dn = jnp.concatenate([zero, jnp.sin(ar), zero, jnp.sin(ac)], -1)
    sin_up = jnp.concatenate([-jnp.sin(ar), zero, -jnp.sin(ac), zero], -1)
    if both_halves:
        parts = [jnp.concatenate([p, p], -1) for p in (cos, sin_dn, sin_up)]
    else:
        one, zz = jnp.ones_like(cos), jnp.zeros_like(cos)
        parts = [jnp.concatenate([cos, one], -1), jnp.concatenate([sin_dn, zz], -1), jnp.concatenate([sin_up, zz], -1)]
    ident = [jnp.ones((seq_s, LANES), F32), jnp.zeros((seq_s, LANES), F32), jnp.zeros((seq_s, LANES), F32)]
    return jnp.stack([jnp.concatenate([i_, p], 0) for i_, p in zip(ident, parts)], 0)


def _even_weights(w_in, w_dec, b_dec, w_uq):
    d = w_in.shape[0]
    sizes = (H_A * DK_A, H_A * DK_A, H_A * DV_A, H_A * DV_A, GLA_LR, GLA_LR, Q_RANK, KV_RANK, ROPE_B)
    offs = np.concatenate([[0], np.cumsum(sizes)])
    qa, ka, va, ga, lrf, lrb, cq, ckv, kr = [w_in[:, offs[j]:offs[j + 1]] for j in range(9)]
    qa = qa * (DK_A ** -0.5)
    qk = jnp.concatenate([qa.reshape(d, H_A, DK_A), ka.reshape(d, H_A, DK_A)], -1).reshape(d, 2 * H_A * DK_A)
    pad = jnp.zeros((d, LANES - ROPE_B - 2 * GLA_LR), w_in.dtype)
    w = jnp.concatenate([qk, va, ga, cq, ckv, kr, lrf, lrb, pad], -1).astype(BF16)
    wd = jnp.zeros((LANES, H_A, 2, DK_A), F32)
    wd = wd.at[ROPE_B:ROPE_B + GLA_LR, :, 0, :].set(w_dec[0].reshape(GLA_LR, H_A, DK_A))
    wd = wd.at[ROPE_B + GLA_LR:ROPE_B + 2 * GLA_LR, :, 1, :].set(w_dec[1].reshape(GLA_LR, H_A, DK_A))
    wd = wd.reshape(LANES, 2 * H_A * DK_A).astype(BF16)
    bd = jnp.stack([b_dec[0].reshape(H_A, DK_A), b_dec[1].reshape(H_A, DK_A)], 1).reshape(1, 2 * H_A * DK_A)
    wq = w_uq.reshape(Q_RANK, H_B, NOPE_B + ROPE_B)
    wq = jnp.concatenate([wq, jnp.zeros((Q_RANK, H_B, 2 * LANES - NOPE_B - ROPE_B), w_uq.dtype)], -1)
    wq = wq.reshape(Q_RANK, H_B * 2 * LANES).astype(BF16)
    return w, wd, bd, wq


def kernel(x_prompt, x_sample, state_gla, cache_mla_ckv, cache_mla_krope, cache_diff_k, cache_diff_v, c, c_ctx, w_ada, b_ada, norm_gain, final_gain, w_in_even, w_out_even, gla_w_decay, gla_b_decay, gla_norm, mla_q_norm, mla_kv_norm, mla_w_uq, mla_w_ukv, w_in_odd, w_out_odd, diff_lambda, diff_norm, ffn_w_up, ffn_conv_w, ffn_conv_b, ffn_w_down):
    nb_p, seq_p, d = x_prompt.shape
    nb_s, seq_s, _ = x_sample.shape
    past = cache_mla_ckv.shape[2]
    depth = w_ada.shape[0]
    n_even, n_odd = w_in_even.shape[0], w_in_odd.shape[0]
    n_p, n_s = nb_p * seq_p, nb_s * seq_s
    lay = _Layout(n_p, n_s, seq_p, seq_s, TM)
    lay_ffn = _Layout(n_p, n_s, seq_p, seq_s, FFN_TM)
    per_tile = TM // seq_p
    assert seq_s % GRID_W == 0 and 1 + nb_s <= ADA_ROWS
    assert n_p % seq_s == 0 and seq_p & (seq_p - 1) == 0 and seq_s & (seq_s - 1) == 0
    assert seq_p % GLA_C == 0 and seq_s % GLA_C == 0 and seq_p % TQ == 0 and seq_s % TQ == 0

    cond = jnp.concatenate([c_ctx[None, :], c, jnp.zeros((ADA_ROWS - 1 - nb_s, d), F32)], 0)
    mods = _ada_call(cond, w_ada, b_ada).reshape(depth, ADA_ROWS, 6, d)

    tab_mla = _rope_tables(seq_s, both_halves=False)
    tab_diff = _rope_tables(seq_s, both_halves=True)
    gla_consts = _gla_consts(GLA_C)
    wup_all = ffn_w_up.astype(BF16)
    wdn_all = ffn_w_down.astype(BF16)
    cb_all = ffn_conv_b.reshape(depth, 1, -1)
    wo_even = w_out_even.astype(BF16)
    wo_odd = w_out_odd.astype(BF16)
    fgain = final_gain.reshape(1, d)
    dgains = diff_norm.reshape(n_odd, 1, 2 * DH_C)

    x = (x_prompt.reshape(n_p, d), x_sample.reshape(n_s, d))
    st_new = caches_even = caches_odd = None
    for l in range(depth):
        i = l // 2
        if l % 2 == 0:
            slot = _Slot(caches_even, i, n_even, per_tile)
            w, wd, bd, wq = _even_weights(w_in_even[i], gla_w_decay[i], gla_b_decay[i], mla_w_uq[i])
            wukv = mla_w_ukv[i].astype(BF16)
            qk, la, v, ga, qb, kra, kv, ckv_new, kr_new = _even_proj_call(
                lay, x, mods, l, norm_gain, w, wd, bd, mla_q_norm[i].reshape(1, -1), mla_kv_norm[i].reshape(1, -1),
                wq, wukv, tab_mla, slot, caches_even)
            caches_even = (ckv_new, kr_new)
            ggain = gla_norm[i].reshape(1, DV_A)
            og_p, st_new = _gla_call(qk, la, v, ga, ggain, gla_consts, None, 0, 0, nb_p, seq_p,
                                     _Slot(st_new, i, n_even), st_new)
            og_s, = _gla_call(qk, la, v, ga, ggain, gla_consts, state_gla, i, n_p, nb_s, seq_s, None, None)
            kv_ctx, kr_ctx = _kv_expand_call(cache_mla_ckv, cache_mla_krope, i, wukv)
            om_p = _mla_attn_call(qb, [(kv, kra, 0, seq_p)], 0, nb_p, seq_p)
            om_s = _mla_attn_call(qb, [(kv_ctx, kr_ctx, 0, past), (kv, kra, n_p, seq_s)], n_p, nb_s, seq_s)
            pairs, wo = [(og_p, og_s), (om_p, om_s)], wo_even
        else:
            slot = _Slot(caches_odd, i, n_odd, per_tile)
            lam_init = 0.8 - 0.6 * math.exp(-0.3 * l)
            n = H_C * 2 * DH_C
            col_scale = np.concatenate([np.full(n, DIFF_SCALE * LOG2E, np.float32), np.ones(2 * n, np.float32)])
            wi = (w_in_odd[i] * col_scale).astype(BF16)
            q, kb, vb, kf_new, vf_new = _odd_proj_call(lay, x, mods, l, norm_gain, wi, tab_diff, slot, caches_odd)
            caches_odd = (kf_new, vf_new)
            od_p = _diff_attn_call(q, diff_lambda, dgains, i, None, [(kb, vb, 0, seq_p)], 0, nb_p, seq_p, lam_init)
            od_s = _diff_attn_call(q, diff_lambda, dgains, i, (cache_diff_k, cache_diff_v), [(kb, vb, n_p, seq_s)],
                                   n_p, nb_s, seq_s, lam_init)
            pairs, wo = [(od_p, od_s)], wo_odd
        x = _ffn_call(lay_ffn, x, pairs, mods, l, norm_gain, wo, i, wup_all, ffn_conv_w, cb_all, wdn_all, fgain,
                      final=(l == depth - 1))
        if l < depth - 1:
            x = x[0]
    y_prompt = x[0].reshape(nb_p, seq_p, d)
    y_sample = x[1].reshape(nb_s, seq_s, d)
    diff_shape = (nb_p, n_odd, seq_p, H_C, 2 * DH_C)
    return (y_prompt, y_sample, st_new, caches_even[0], caches_even[1],
            caches_odd[0].reshape(diff_shape), caches_odd[1].reshape(diff_shape))
```

```python
import functools
import math

import numpy as np
import jax
import jax.numpy as jnp
from jax import lax
from jax.experimental import pallas as pl
from jax.experimental.pallas import tpu as pltpu

GRID_W = 64
ROPE_BASE = 10000.0
EPS = 1e-6
H_A, DK_A, DV_A = 4, 64, 128
GLA_LR = 16
GLA_TAU = 16.0
H_B, Q_RANK, KV_RANK, NOPE_B, ROPE_B, V_B = 4, 256, 256, 128, 64, 128
MLA_SCALE = (NOPE_B + ROPE_B) ** -0.5
H_C, DH_C = 8, 64
DIFF_SCALE = DH_C ** -0.5
LOG2E = math.log2(math.e)

LANES = 128
SUBLANES = 8
HALO = 16
VMEM_LIMIT = 56 * 1024 * 1024

TM = 512
TQ = 256
MLA_TQ = 512
GLA_C = 64
GLA_STEPS = 1
GLA_HEADS = 4
GLA_UNROLL = 2
ADA_TN = 1536
FFN_TM = 512
FFN_TF = 256
FFN_AHEAD = 2
FFN_DOWN = 4
ADA_ROWS = 16

F32 = jnp.float32
BF16 = jnp.bfloat16


def _dot(a, b):
    return jnp.dot(a, b, preferred_element_type=F32)


def _dot_nt(a, b):
    return lax.dot_general(a, b, (((1,), (1,)), ((), ())), preferred_element_type=F32)


def _silu(x):
    return x * (1.0 / (1.0 + jnp.exp(-x)))


def _rms(x, gain):
    return x * lax.rsqrt(jnp.mean(x * x, axis=-1, keepdims=True) + EPS) * gain


def _normmod(x, gain, shift, scale):
    return _rms(x, gain) * (1.0 + scale) + shift


def _rope(x, cos, sin_dn, sin_up):
    return (x * cos + pltpu.roll(x, 16, axis=1) * sin_dn + pltpu.roll(x, LANES - 16, axis=1) * sin_up)


def _cparams(sem, vmem=VMEM_LIMIT):
    return pltpu.CompilerParams(dimension_semantics=sem, vmem_limit_bytes=vmem)


def _resident(shape):
    nd = len(shape)
    return pl.BlockSpec(shape, lambda *_: (0,) * nd, pipeline_mode=pl.Buffered(1))


def _layer_spec(shape, l):
    nd = len(shape)
    return pl.BlockSpec((1,) + tuple(shape[1:]), lambda *_: (l,) + (0,) * (nd - 1), pipeline_mode=pl.Buffered(1))


def _ada_kernel(c_ref, w_ref, b_ref, o_ref):
    s = _silu(c_ref[...]).astype(BF16)
    o_ref[0] = _dot(s, w_ref[0].astype(BF16)) + b_ref[0]


def _ada_call(cond, w_ada, b_ada):
    depth, d, n = w_ada.shape
    return pl.pallas_call(
        _ada_kernel,
        grid=(depth, n // ADA_TN),
        in_specs=[
            pl.BlockSpec((ADA_ROWS, d), lambda l, j: (0, 0)),
            pl.BlockSpec((1, d, ADA_TN), lambda l, j: (l, 0, j)),
            pl.BlockSpec((1, 1, ADA_TN), lambda l, j: (l, 0, j)),
        ],
        out_specs=pl.BlockSpec((1, ADA_ROWS, ADA_TN), lambda l, j: (l, 0, j)),
        out_shape=jax.ShapeDtypeStruct((depth, ADA_ROWS, n), F32),
        name="ada",
        compiler_params=_cparams(("arbitrary", "arbitrary")),
    )(cond, w_ada, b_ada.reshape(depth, 1, n))


class _Layout:
    def __init__(self, n_prompt, n_sample, seq_p, seq_s, tm):
        assert tm % seq_p == 0 and n_prompt % tm == 0 and n_sample % tm == 0 and seq_s % tm == 0
        self.tm = tm
        self.np, self.ns, self.nt = n_prompt, n_sample, n_prompt + n_sample
        self.seq_p, self.seq_s = seq_p, seq_s
        self.pt = n_prompt // tm
        self.st = n_sample // tm
        self.tiles = self.pt + self.st
        self.tiles_per_seq_s = seq_s // tm

    def group(self, i):
        return jnp.where(i < self.pt, 0, 1 + (i - self.pt) // self.tiles_per_seq_s)

    def pos_block(self, i):
        return jnp.where(i < self.pt, 0, self.tiles_per_seq_s + (i - self.pt) % self.tiles_per_seq_s)

    def p_block(self, i):
        return jnp.minimum(i, self.pt - 1)

    def s_block(self, i):
        return jnp.maximum(i - self.pt, 0)


def _mod_spec(lay, l, d):
    return pl.BlockSpec((1, 1, 6, d), lambda i: (l, lay.group(i), 0, 0))


def _tab_spec(lay):
    return pl.BlockSpec((3, lay.tm, LANES), lambda i: (0, lay.pos_block(i), 0))


def _x_specs(lay, x):
    if isinstance(x, tuple):
        d = x[0].shape[1]
        return [pl.BlockSpec((lay.tm, d), lambda i: (lay.p_block(i), 0)),
                pl.BlockSpec((lay.tm, d), lambda i: (lay.s_block(i), 0))], list(x)
    return [pl.BlockSpec((lay.tm, x.shape[1]), lambda i: (i, 0))], [x]


def _read_x(x_refs, prompt_tiles):
    if len(x_refs) == 1:
        return x_refs[0][...]
    return jnp.where(pl.program_id(0) < prompt_tiles, x_refs[0][...], x_refs[1][...])


class _Slot:
    def __init__(self, prev, slot, n_slots, per_block=1):
        self.prev, self.slot, self.n_slots, self.per_block = prev, slot, n_slots, per_block

    @property
    def first(self):
        return self.prev is None

    def spec(self, tail, index):
        zeros = (0,) * len(tail)
        if self.first:
            return pl.BlockSpec((self.per_block, self.n_slots) + tail, lambda *g: (index(*g), 0) + zeros)
        s = self.slot
        return pl.BlockSpec((self.per_block, 1) + tail, lambda *g: (index(*g), s) + zeros)

    def shape(self, nb, tail):
        return jax.ShapeDtypeStruct((nb, self.n_slots) + tail, F32)

    def at(self):
        return self.slot if self.first else 0

    def zero_others(self, ref):
        if self.first:
            for b in range(self.per_block):
                for s in range(self.n_slots):
                    if s != self.slot:
                        ref[b, s] = jnp.zeros(ref.shape[2:], F32)


def _even_proj_kernel(*refs, nx, prompt_tiles, slot):
    x_refs = refs[:nx]
    (mod_ref, gain_ref, w_ref, wd_ref, bd_ref, qn_ref, kvn_ref, wuq_ref, wukv_ref, tab_ref) = refs[nx:nx + 10]
    outs = refs[len(refs) - 9:]
    qk_ref, la_ref, v_ref, ga_ref, qb_ref, kra_ref, kv_ref, ckv_ref, kr_ref = outs
    mod = mod_ref[0, 0]
    seq = ckv_ref.shape[2]
    parts = [slice(b * seq, (b + 1) * seq) for b in range(slot.per_block)]
    x = _read_x(x_refs, prompt_tiles)
    ys = [_dot(_normmod(x[r], gain_ref[0, 0:1], mod[0:1], mod[1:2]).astype(BF16), w_ref[...]) for r in parts]
    ckvns, gs = [], []
    for r, y in zip(parts, ys):
        qk_ref[r, :] = y[:, 0:512]
        v_ref[r, :] = y[:, 512:1024].astype(BF16)
        ga_ref[r, :] = y[:, 1024:1536]
        cq, ckv, g = y[:, 1536:1792], y[:, 1792:2048], y[:, 2048:2176]
        pre = _dot(g.astype(BF16), wd_ref[...]) + bd_ref[...]
        la_ref[r, :] = (jnp.minimum(pre, 0.0) - jnp.log(1.0 + jnp.exp(-jnp.abs(pre)))) * (LOG2E / GLA_TAU)
        cos, sin_dn, sin_up = tab_ref[0, r, :], tab_ref[1, r, :], tab_ref[2, r, :]
        qb = _dot(_rms(cq, qn_ref[...]).astype(BF16), wuq_ref[...])
        for hh in range(H_B):
            lo = hh * 2 * LANES
            qb_ref[r, lo:lo + LANES] = qb[:, lo:lo + LANES].astype(BF16)
            qb_ref[r, lo + LANES:lo + 2 * LANES] = _rope(qb[:, lo + LANES:lo + 2 * LANES], cos, sin_dn, sin_up).astype(BF16)
        ckvn = _rms(ckv, kvn_ref[...])
        kv_ref[r, :] = _dot(ckvn.astype(BF16), wukv_ref[...]).astype(BF16)
        lane = lax.broadcasted_iota(jnp.int32, g.shape, 1)
        kra_ref[r, :] = jnp.where(lane < ROPE_B, _rope(g, cos, sin_dn, sin_up), 0.0).astype(BF16)
        ckvns.append(ckvn)
        gs.append(g)

    @pl.when(pl.program_id(0) < prompt_tiles)
    def _():
        for b in range(slot.per_block):
            ckv_ref[b, slot.at()] = ckvns[b]
            kr_ref[b, slot.at()] = gs[b][:, :ROPE_B]
        slot.zero_others(ckv_ref)
        slot.zero_others(kr_ref)


def _even_proj_call(lay, x, mods, l, gains, w, wd, bd, qn, kvn, wuq, wukv, tab, slot, prev):
    d = w.shape[0]
    x_specs, x_args = _x_specs(lay, x)
    row = lambda n: pl.BlockSpec((TM, n), lambda i: (i, 0))
    outs = [(512, F32), (512, F32), (512, BF16), (512, F32), (1024, BF16), (LANES, BF16), (1024, BF16)]
    in_specs = x_specs + [_mod_spec(lay, l, d), _layer_spec(gains.shape, l), _resident(w.shape), _resident(wd.shape),
                          _resident(bd.shape), _resident(qn.shape), _resident(kvn.shape), _resident(wuq.shape),
                          _resident(wukv.shape), _tab_spec(lay)]
    args = x_args + [mods, gains, w, wd, bd, qn, kvn, wuq, wukv, tab]
    aliases = {}
    if not slot.first:
        aliases = {len(args): len(outs), len(args) + 1: len(outs) + 1}
        in_specs += [pl.BlockSpec(memory_space=pl.ANY), pl.BlockSpec(memory_space=pl.ANY)]
        args += list(prev)
    nb_p = lay.np // lay.seq_p
    return pl.pallas_call(
        functools.partial(_even_proj_kernel, nx=len(x_args), prompt_tiles=lay.pt, slot=slot),
        grid=(lay.tiles,),
        in_specs=in_specs,
        out_specs=[row(n) for n, _ in outs]
        + [slot.spec((lay.seq_p, KV_RANK), lay.p_block), slot.spec((lay.seq_p, ROPE_B), lay.p_block)],
        out_shape=[jax.ShapeDtypeStruct((lay.nt, n), dt) for n, dt in outs]
        + [slot.shape(nb_p, (lay.seq_p, KV_RANK)), slot.shape(nb_p, (lay.seq_p, ROPE_B))],
        input_output_aliases=aliases,
        name="even_proj",
        compiler_params=_cparams(("arbitrary",)),
    )(*args)


def _gla_consts(c):
    levels = int(math.log2(c))
    idx = np.arange(c)
    cum = idx[None, :] <= idx[:, None]
    rows, sgn, masks = [cum], [], []
    for lev in range(levels):
        half = 1 << lev
        node = idx // (2 * half)
        mid = node * 2 * half + half
        upper = idx >= mid
        rows.append(cum[mid - 1])
        sgn.append(np.repeat(np.where(upper, 1.0, -1.0)[:, None], LANES, 1))
        same = node[:, None] == node[None, :]
        masks.append(np.concatenate([same & upper[:, None] & ~upper[None, :],
                                     same & ~upper[:, None] & upper[None, :]], 1))
    eye = np.eye(c, dtype=bool)
    masks.append(np.concatenate([eye, eye], 1))
    m = np.concatenate(rows, 0)
    return (jnp.asarray(np.concatenate([m, m, m], 1), BF16), jnp.asarray(np.stack(sgn, 0), F32),
            jnp.asarray(np.stack(masks, 0), F32), levels)


def _gla_kernel(*refs, t, has_s0, slot, levels):
    qk_ref, la_ref, v_ref, ga_ref, gain_ref, m3_ref, sg_ref, mask_ref = refs[:8]
    s0_ref = refs[8] if has_s0 else None
    n_out = 1 if slot is None else 2
    o_ref = refs[len(refs) - 3 - n_out]
    st_ref = None if slot is None else refs[len(refs) - 4]
    of_ref, ob_ref, st_scr = refs[len(refs) - 3:]
    c = GLA_C
    n = t // c
    heads = range(H_A)
    lane = lax.broadcasted_iota(jnp.int32, (1, LANES), 1)
    is_fwd = lane < DK_A
    keep_f = jnp.where(is_fwd, 1.0, 0.0).astype(BF16)
    keep_b = jnp.where(is_fwd, 0.0, 1.0).astype(BF16)

    def split(x):
        return jnp.concatenate([x * keep_f, x * keep_b], axis=0)

    for h in heads:
        if has_s0:
            st_scr[h] = jnp.concatenate([jnp.transpose(s0_ref[0, 0, 0, h]), jnp.transpose(s0_ref[0, 0, 1, h])], axis=1)
        else:
            st_scr[h] = jnp.zeros((DV_A, LANES), F32)

    steps = GLA_STEPS if n % GLA_STEPS == 0 else 1

    def body(j, carry):
        rf = [pl.multiple_of((j * steps + s) * c, c) for s in range(steps)]
        rb = [pl.multiple_of((n - 1 - j * steps - s) * c, c) for s in range(steps)]
        sl = [slice(h * LANES, (h + 1) * LANES) for h in heads]
        for g in range(0, H_A, GLA_HEADS):
            process([(s, h) for s in range(steps) for h in heads[g:g + GLA_HEADS]], rf, rb, sl)
        return carry

    def process(units, rf, rb, sl):
        qq, kk, ll, split3 = [], [], [], []
        for s, h in units:
            qkf, qkb = qk_ref[pl.ds(rf[s], c), sl[h]], qk_ref[pl.ds(rb[s], c), sl[h]]
            qq.append(jnp.where(is_fwd, qkf, pltpu.roll(qkb, DK_A, axis=1)))
            kk.append(jnp.where(is_fwd, pltpu.roll(qkf, DK_A, axis=1), qkb))
            ll.append(jnp.where(is_fwd, la_ref[pl.ds(rf[s], c), sl[h]], la_ref[pl.ds(rb[s], c), sl[h]]))
            hi = ll[-1].astype(BF16)
            r1 = ll[-1] - hi.astype(F32)
            mid = r1.astype(BF16)
            lo = (r1 - mid.astype(F32)).astype(BF16)
            split3.append(jnp.concatenate([hi, mid, lo], axis=0))
        e_all = _dot(m3_ref[...], jnp.concatenate(split3, axis=1))
        e = [e_all[:, k * LANES:(k + 1) * LANES] for k in range(len(units))]
        cx = [jnp.where(is_fwd, e[k][0:c], e[k][0:c] - ll[k]) for k in range(len(units))]
        a = [_dot_nt(qq[k].astype(BF16), split(kk[k].astype(BF16))) * mask_ref[levels] for k in range(len(units))]
        for lev in range(levels):
            for k in range(len(units)):
                ex = jnp.exp2(sg_ref[lev] * (cx[k] - e[k][(lev + 1) * c:(lev + 2) * c]))
                a[k] = a[k] + _dot_nt((qq[k] * ex).astype(BF16), split((kk[k] * ex).astype(BF16))) * mask_ref[lev]
        for k, (s, h) in enumerate(units):
            tot = e[k][c - 1:c]
            u = jnp.exp2(cx[k])
            w = jnp.exp2(tot - cx[k])
            qd = (qq[k] * jnp.where(is_fwd, u, w)).astype(BF16)
            kd = (kk[k] * jnp.where(is_fwd, w, u)).astype(BF16)
            vs = jnp.concatenate([v_ref[pl.ds(rf[s], c), sl[h]], v_ref[pl.ds(rb[s], c), sl[h]]], axis=0)
            st = st_scr[h]
            o = _dot(split(a[k].astype(BF16)), vs) + _dot_nt(split(qd), st.astype(BF16))
            of_ref[pl.ds(rf[s], c), sl[h]] = o[:c]
            ob_ref[pl.ds(rb[s], c), sl[h]] = o[c:]
            vt = jnp.transpose(vs.astype(F32)).astype(BF16)
            st_scr[h] = st * jnp.exp2(tot) + _dot(vt, split(kd))

    lax.fori_loop(0, n // steps, body, 0, unroll=GLA_UNROLL)
    for h in heads:
        sl = slice(h * LANES, (h + 1) * LANES)
        o = _rms(of_ref[:, sl] + ob_ref[:, sl], gain_ref[...]) * _silu(ga_ref[:, sl])
        o_ref[:, sl] = o.astype(BF16)
        if slot is not None:
            st = st_scr[h]
            st_ref[0, slot.at(), 0, h] = jnp.transpose(st[:, :DK_A])
            st_ref[0, slot.at(), 1, h] = jnp.transpose(st[:, DK_A:])
    if slot is not None:
        slot.zero_others(st_ref)


def _gla_call(qk, la, v, ga, gain, consts, s0, s0_slot, row0, nb, t, slot, prev):
    m3, sg, mask, levels = consts
    blk0 = row0 // t
    width = H_A * LANES
    tok = pl.BlockSpec((t, width), lambda b: (blk0 + b, 0))
    in_specs = [tok, tok, tok, tok, _resident(gain.shape), _resident(m3.shape), _resident(sg.shape), _resident(mask.shape)]
    args = [qk, la, v, ga, gain, m3, sg, mask]
    tail = (2, H_A, DK_A, DV_A)
    if s0 is not None:
        in_specs.append(pl.BlockSpec((1, 1) + tail, lambda b: (b, s0_slot, 0, 0, 0, 0)))
        args.append(s0)
    out_specs = [pl.BlockSpec((t, width), lambda b: (b, 0))]
    out_shape = [jax.ShapeDtypeStruct((nb * t, width), BF16)]
    aliases = {}
    if slot is not None:
        out_specs.append(slot.spec(tail, lambda b: b))
        out_shape.append(slot.shape(nb, tail))
        if not slot.first:
            aliases = {len(args): 1}
            in_specs.append(pl.BlockSpec(memory_space=pl.ANY))
            args.append(prev)
    return pl.pallas_call(
        functools.partial(_gla_kernel, t=t, has_s0=s0 is not None, slot=slot, levels=levels),
        grid=(nb,),
        in_specs=in_specs,
        out_specs=out_specs,
        out_shape=out_shape,
        scratch_shapes=[pltpu.VMEM((t, width), F32), pltpu.VMEM((t, width), F32), pltpu.VMEM((H_A, DV_A, LANES), F32)],
        input_output_aliases=aliases,
        name="gla",
        compiler_params=_cparams(("arbitrary",)),
    )(*args)


def _kv_expand_kernel(c_ref, r_ref, w_ref, o_ref, ro_ref):
    o_ref[...] = _dot(c_ref[0, 0].astype(BF16), w_ref[...]).astype(BF16)
    kr = r_ref[0, 0]
    ro_ref[...] = jnp.concatenate([kr, jnp.zeros((kr.shape[0], LANES - ROPE_B), F32)], axis=1).astype(BF16)


def _kv_expand_call(cache_ckv, cache_kr, slot, wukv):
    nb, _, past, r = cache_ckv.shape
    return pl.pallas_call(
        _kv_expand_kernel,
        grid=(nb,),
        in_specs=[pl.BlockSpec((1, 1, past, r), lambda b: (b, slot, 0, 0)),
                  pl.BlockSpec((1, 1, past, ROPE_B), lambda b: (b, slot, 0, 0)), _resident(wukv.shape)],
        out_specs=[pl.BlockSpec((past, wukv.shape[1]), lambda b: (b, 0)), pl.BlockSpec((past, LANES), lambda b: (b, 0))],
        out_shape=[jax.ShapeDtypeStruct((nb * past, wukv.shape[1]), BF16), jax.ShapeDtypeStruct((nb * past, LANES), BF16)],
        name="kv_expand",
        compiler_params=_cparams(("arbitrary",)),
    )(cache_ckv, cache_kr, wukv)


def _mla_attn_kernel(*refs, nseg):
    q_ref = refs[0]
    seg = [(refs[1 + 2 * s], refs[2 + 2 * s]) for s in range(nseg)]
    o_ref = refs[1 + 2 * nseg]
    def qk(hh):
        q = q_ref[:, hh * 2 * LANES:(hh + 1) * 2 * LANES]
        scores = []
        for kv_ref, kr_ref in seg:
            kcat = jnp.concatenate([kv_ref[:, hh * 2 * LANES:hh * 2 * LANES + LANES], kr_ref[...]], axis=1)
            scores.append(_dot_nt(q, kcat) * (MLA_SCALE * LOG2E))
        return scores

    nxt = qk(0)
    for hh in range(H_B):
        scores = nxt
        if hh + 1 < H_B:
            nxt = qk(hh + 1)
        mx = functools.reduce(jnp.maximum, [jnp.max(s, axis=-1, keepdims=True) for s in scores])
        ps = [jnp.exp2(s - mx) for s in scores]
        den = functools.reduce(lambda a, b: a + b, [jnp.sum(p, axis=-1, keepdims=True) for p in ps])
        acc = functools.reduce(lambda a, b: a + b, [
            _dot(p.astype(BF16), kv_ref[:, hh * 2 * LANES + LANES:(hh + 1) * 2 * LANES])
            for p, (kv_ref, _) in zip(ps, seg)])
        o_ref[:, hh * LANES:(hh + 1) * LANES] = (acc * (1.0 / den)).astype(BF16)


def _mla_attn_call(qb, segs, row0, nb, t):
    tq = min(t, MLA_TQ)
    assert t % tq == 0 and row0 % tq == 0
    q0 = row0 // tq
    per = t // tq
    in_specs = [pl.BlockSpec((tq, qb.shape[1]), lambda b, j: (q0 + b * per + j, 0))]
    args = [qb]
    for kv, kr, r0, s in segs:
        b0 = r0 // s
        in_specs.append(pl.BlockSpec((s, kv.shape[1]), lambda b, j, b0=b0: (b0 + b, 0)))
        in_specs.append(pl.BlockSpec((s, LANES), lambda b, j, b0=b0: (b0 + b, 0)))
        args += [kv, kr]
    return pl.pallas_call(
        functools.partial(_mla_attn_kernel, nseg=len(segs)),
        grid=(nb, per),
        in_specs=in_specs,
        out_specs=pl.BlockSpec((tq, H_B * V_B), lambda b, j: (b * per + j, 0)),
        out_shape=jax.ShapeDtypeStruct((nb * t, H_B * V_B), BF16),
        name="mla_attn",
        compiler_params=_cparams(("arbitrary", "arbitrary")),
    )(*args)


def _odd_proj_kernel(*refs, prompt_tiles, slot):
    x_ref, mod_ref, gain_ref, w_ref, tab_ref = refs[:5]
    q_ref, kb_ref, vb_ref, kf_ref, vf_ref = refs[len(refs) - 5:]
    mod = mod_ref[0, 0]
    n = q_ref.shape[1]
    heads = n // LANES
    seq = kf_ref.shape[2] // heads
    parts = [slice(b * seq, (b + 1) * seq) for b in range(slot.per_block)]
    x = x_ref[...]
    ys = [_dot(_normmod(x[r], gain_ref[0, 0:1], mod[0:1], mod[1:2]).astype(BF16), w_ref[...]) for r in parts]
    for r, y in zip(parts, ys):
        cos, sin_dn, sin_up = tab_ref[0, r, :], tab_ref[1, r, :], tab_ref[2, r, :]
        vb_ref[r, :] = y[:, 2 * n:3 * n].astype(BF16)
        for g in range(heads):
            lo = g * LANES
            q_ref[r, lo:lo + LANES] = _rope(y[:, lo:lo + LANES], cos, sin_dn, sin_up).astype(BF16)
            kb_ref[r, lo:lo + LANES] = _rope(y[:, n + lo:n + lo + LANES], cos, sin_dn, sin_up).astype(BF16)

    @pl.when(pl.program_id(0) < prompt_tiles)
    def _():
        for b in range(slot.per_block):
            for g in range(heads):
                lo = g * LANES
                kf_ref[b, slot.at(), pl.ds(g, seq, stride=heads), :] = ys[b][:, n + lo:n + lo + LANES]
                vf_ref[b, slot.at(), pl.ds(g, seq, stride=heads), :] = ys[b][:, 2 * n + lo:2 * n + lo + LANES]
        slot.zero_others(kf_ref)
        slot.zero_others(vf_ref)


def _odd_proj_call(lay, x, mods, l, gains, w, tab, slot, prev):
    d = x.shape[1]
    n = w.shape[1] // 3
    row = lambda m: pl.BlockSpec((TM, m), lambda i: (i, 0))
    in_specs = [row(d), _mod_spec(lay, l, d), _layer_spec(gains.shape, l), _resident(w.shape), _tab_spec(lay)]
    args = [x, mods, gains, w, tab]
    aliases = {}
    if not slot.first:
        aliases = {len(args): 3, len(args) + 1: 4}
        in_specs += [pl.BlockSpec(memory_space=pl.ANY), pl.BlockSpec(memory_space=pl.ANY)]
        args += list(prev)
    tail = (lay.seq_p * H_C, 2 * DH_C)
    nb_p = lay.np // lay.seq_p
    return pl.pallas_call(
        functools.partial(_odd_proj_kernel, prompt_tiles=lay.pt, slot=slot),
        grid=(lay.tiles,),
        in_specs=in_specs,
        out_specs=[row(n), row(n), row(n), slot.spec(tail, lay.p_block), slot.spec(tail, lay.p_block)],
        out_shape=[jax.ShapeDtypeStruct((lay.nt, n), BF16)] * 3 + [slot.shape(nb_p, tail)] * 2,
        input_output_aliases=aliases,
        name="odd_proj",
        compiler_params=_cparams(("arbitrary",)),
    )(*args)


def _diff_attn_kernel(*refs, nseg, has_ctx, lam_init):
    q_ref, dl_ref, gain_ref = refs[0], refs[1], refs[2]
    seg = [(refs[3 + 2 * s], refs[4 + 2 * s]) for s in range(nseg)]
    o_ref = refs[3 + 2 * nseg]
    if has_ctx:
        (kc_ref, vc_ref), (kc_scr, vc_scr) = seg[0], refs[4 + 2 * nseg:6 + 2 * nseg]

        @pl.when(pl.program_id(1) == 0)
        def _():
            for hh in range(H_C):
                rows = pl.ds(hh, kc_scr.shape[0], stride=H_C)
                kc_scr[:, hh * LANES:(hh + 1) * LANES] = kc_ref[0, 0, rows, :].astype(BF16)
                vc_scr[:, hh * LANES:(hh + 1) * LANES] = vc_ref[0, 0, rows, :].astype(BF16)

        seg[0] = (kc_scr, vc_scr)
    dl = dl_ref[0]
    lam = (jnp.exp(jnp.sum(dl[0:1] * dl[1:2], axis=-1, keepdims=True))
           - jnp.exp(jnp.sum(dl[2:3] * dl[3:4], axis=-1, keepdims=True)) + lam_init)
    tq = q_ref.shape[0]
    lane = lax.broadcasted_iota(jnp.int32, (1, LANES), 1)
    keep1 = jnp.where(lane < DH_C, 1.0, 0.0).astype(BF16)
    keep2 = jnp.where(lane < DH_C, 0.0, 1.0).astype(BF16)
    def qk(hh):
        q = q_ref[:, hh * LANES:(hh + 1) * LANES]
        qq = jnp.concatenate([q * keep1, q * keep2], axis=0)
        return [_dot_nt(qq, k_ref[:, hh * LANES:(hh + 1) * LANES]) for k_ref, _ in seg]

    nxt = qk(0)
    for hh in range(H_C):
        scores = nxt
        if hh + 1 < H_C:
            nxt = qk(hh + 1)
        mx = functools.reduce(jnp.maximum, [jnp.max(s, axis=-1, keepdims=True) for s in scores])
        ps = [jnp.exp2(s - mx) for s in scores]
        den = functools.reduce(lambda a, b: a + b, [jnp.sum(p, axis=-1, keepdims=True) for p in ps])
        acc = functools.reduce(lambda a, b: a + b, [
            _dot(p.astype(BF16), v_ref[:, hh * LANES:(hh + 1) * LANES]) for p, (_, v_ref) in zip(ps, seg)])
        acc = acc * (1.0 / den)
        o = acc[:tq] - lam * acc[tq:]
        o_ref[:, hh * LANES:(hh + 1) * LANES] = (_rms(o, gain_ref[0]) * (1.0 - lam_init)).astype(BF16)


def _diff_attn_call(q, dls, gains, slot, ctx, segs, row0, nb, t, lam_init):
    tq = min(t, TQ)
    assert t % tq == 0 and row0 % tq == 0
    q0 = row0 // tq
    per = t // tq
    n = q.shape[1]
    in_specs = [pl.BlockSpec((tq, n), lambda b, j: (q0 + b * per + j, 0)),
                pl.BlockSpec((1,) + dls.shape[1:], lambda b, j: (slot, 0, 0)),
                pl.BlockSpec((1, 1, gains.shape[2]), lambda b, j: (slot, 0, 0))]
    args = [q, dls, gains]
    scratch = []
    if ctx is not None:
        nb_c, n_l, past, heads, dh = ctx[0].shape
        for cache in ctx:
            in_specs.append(pl.BlockSpec((1, 1, past * heads, dh), lambda b, j: (b, slot, 0, 0)))
            args.append(cache.reshape(nb_c, n_l, past * heads, dh))
        scratch = [pltpu.VMEM((past, n), BF16), pltpu.VMEM((past, n), BF16)]
    for k, v, r0, s in segs:
        b0 = r0 // s
        in_specs.append(pl.BlockSpec((s, n), lambda b, j, b0=b0: (b0 + b, 0)))
        in_specs.append(pl.BlockSpec((s, n), lambda b, j, b0=b0: (b0 + b, 0)))
        args += [k, v]
    return pl.pallas_call(
        functools.partial(_diff_attn_kernel, nseg=len(segs) + (ctx is not None), has_ctx=ctx is not None,
                          lam_init=lam_init),
        grid=(nb, per),
        in_specs=in_specs,
        out_specs=pl.BlockSpec((tq, n), lambda b, j: (b * per + j, 0)),
        out_shape=jax.ShapeDtypeStruct((nb * t, n), BF16),
        scratch_shapes=scratch,
        name="diff_attn",
        compiler_params=_cparams(("arbitrary", "arbitrary")),
    )(*args)


def _halo_specs(rows, width, tm, tile):
    per = tm // HALO
    last = rows // HALO - 1
    return [pl.BlockSpec((HALO, width), lambda i: (jnp.maximum(tile(i) * per - 1, 0), 0)),
            pl.BlockSpec((tm, width), lambda i: (tile(i), 0)),
            pl.BlockSpec((HALO, width), lambda i: (jnp.minimum((tile(i) + 1) * per, last), 0))]


def _ffn_kernel(*refs, lay, d_ff, final, nx, widths):
    pos = 3 * nx + 6 * len(widths)
    trips = [refs[3 * t:3 * t + 3] for t in range(pos // 3)]
    mod_ref, gain_ref, wo_ref, wup_ref, cw_ref, cb_ref, wdn_ref, fg_ref = refs[pos:pos + 8]
    outs = refs[pos + 8:len(refs) - 2]
    u_ref, act_ref = refs[len(refs) - 2:]
    tm = lay.tm
    i = pl.program_id(0)
    row = i * tm
    is_prompt = i < lay.pt
    seq = jnp.where(is_prompt, lay.seq_p, lay.seq_s)
    first = (row & (seq - 1)) == 0
    last = ((row + tm) & (seq - 1)) == 0

    def ext(trip_p, trip_s):
        if trip_s is None:
            return jnp.concatenate([r[...] for r in trip_p], axis=0)
        return jnp.concatenate([jnp.where(is_prompt, a[...], b[...]) for a, b in zip(trip_p, trip_s)], axis=0)

    mod = mod_ref[0, 0]
    x_ext = ext(trips[0], trips[1] if nx == 2 else None)
    proj = None
    k0 = 0
    for p, w in enumerate(widths):
        part = _dot(ext(trips[nx + 2 * p], trips[nx + 2 * p + 1]), wo_ref[0, k0:k0 + w, :])
        k0 += w
        proj = part if proj is None else proj + part
    xm = x_ext + mod[2:3] * proj
    gain, shift, scale = gain_ref[0, 1:2], mod[3:4], mod[4:5]
    hh = _normmod(xm, gain, shift, scale)
    h = jnp.concatenate([jnp.where(first, 0.0, hh[HALO - SUBLANES:HALO]), hh[HALO:HALO + tm],
                         jnp.where(last, 0.0, hh[HALO + tm:HALO + tm + SUBLANES])], axis=0).astype(BF16)
    x = xm[HALO:HALO + tm]
    acc = jnp.zeros((tm, x.shape[1]), F32)
    n_f = d_ff // FFN_TF
    n_slots = u_ref.shape[0]
    sub = lax.broadcasted_iota(jnp.int32, (SUBLANES, FFN_TF), 0)
    cut_prev = is_prompt & (sub == 0)
    cut_next = is_prompt & (sub == SUBLANES - 1)
    seams = range(lay.seq_p, tm, lay.seq_p)

    def up(f):
        for half, base in enumerate((0, d_ff)):
            lo = base + f * FFN_TF
            u_ref[f % n_slots, :, half * FFN_TF:(half + 1) * FFN_TF] = _dot(h, wup_ref[0, :, lo:lo + FFN_TF])

    def conv(f, half):
        lo = half * d_ff + f * FFN_TF
        cols = slice(half * FFN_TF, (half + 1) * FFN_TF)
        cw = cw_ref[0, :, lo:lo + FFN_TF]
        prev = u_ref[f % n_slots, SUBLANES - 1:SUBLANES - 1 + tm, cols]
        here = u_ref[f % n_slots, SUBLANES:SUBLANES + tm, cols]
        nxt = u_ref[f % n_slots, SUBLANES + 1:SUBLANES + 1 + tm, cols]
        for r in seams:
            prev = jnp.concatenate([prev[:r], jnp.where(cut_prev, 0.0, prev[r:r + SUBLANES]), prev[r + SUBLANES:]], axis=0)
            nxt = jnp.concatenate([nxt[:r - SUBLANES], jnp.where(cut_next, 0.0, nxt[r - SUBLANES:r]), nxt[r:]], axis=0)
        return prev * cw[0:1] + here * cw[1:2] + nxt * cw[2:3] + cb_ref[0, :, lo:lo + FFN_TF]

    for f in range(min(FFN_AHEAD, n_f)):
        up(f)
    for f in range(n_f):
        act_ref[:, f * FFN_TF:(f + 1) * FFN_TF] = (_silu(conv(f, 0)) * conv(f, 1)).astype(BF16)
        if f + FFN_AHEAD < n_f:
            up(f + FFN_AHEAD)
        if (f + 1) % FFN_DOWN == 0 or f + 1 == n_f:
            k0 = (f // FFN_DOWN) * FFN_DOWN * FFN_TF
            acc = acc + _dot(act_ref[:, k0:(f + 1) * FFN_TF], wdn_ref[0, k0:(f + 1) * FFN_TF, :])
    y = x + mod[5:6] * acc
    if not final:
        outs[0][...] = y
    else:
        yn = _rms(y, fg_ref[...])

        @pl.when(i < lay.pt)
        def _():
            outs[0][...] = yn

        @pl.when(i >= lay.pt)
        def _():
            outs[1][...] = yn


def _ffn_call(lay, x, pairs, mods, l, gains, wo, slot, wup, cw, cb, wdn, fgain, final):
    d = wo.shape[2]
    d_ff = wdn.shape[1]
    assert d_ff % FFN_TF == 0
    tm = lay.tm
    in_specs, args = [], []
    if isinstance(x, tuple):
        streams = [(x[0], lay.p_block), (x[1], lay.s_block)]
    else:
        streams = [(x, lambda i: i)]
    for a_p, a_s in pairs:
        streams += [(a_p, lay.p_block), (a_s, lay.s_block)]
    for arr, tile in streams:
        in_specs += _halo_specs(arr.shape[0], arr.shape[1], tm, tile)
        args += [arr, arr, arr]
    in_specs += [_mod_spec(lay, l, d), _layer_spec(gains.shape, l), _layer_spec(wo.shape, slot), _layer_spec(wup.shape, l),
                 _layer_spec(cw.shape, l), _layer_spec(cb.shape, l), _layer_spec(wdn.shape, l), _resident((1, d))]
    args += [mods, gains, wo, wup, cw, cb, wdn, fgain]
    if final:
        out_specs = [pl.BlockSpec((tm, d), lambda i: (lay.p_block(i), 0)), pl.BlockSpec((tm, d), lambda i: (lay.s_block(i), 0))]
        out_shape = [jax.ShapeDtypeStruct((lay.np, d), F32), jax.ShapeDtypeStruct((lay.ns, d), F32)]
    else:
        out_specs = [pl.BlockSpec((tm, d), lambda i: (i, 0))]
        out_shape = [jax.ShapeDtypeStruct((lay.nt, d), F32)]
    return pl.pallas_call(
        functools.partial(_ffn_kernel, lay=lay, d_ff=d_ff, final=final, nx=2 if isinstance(x, tuple) else 1,
                          widths=[a_p.shape[1] for a_p, _ in pairs]),
        name="ffn",
        grid=(lay.tiles,),
        in_specs=in_specs,
        out_specs=out_specs,
        out_shape=out_shape,
        scratch_shapes=[pltpu.VMEM((FFN_AHEAD + 1, tm + 2 * SUBLANES, 2 * FFN_TF), F32), pltpu.VMEM((tm, d_ff), BF16)],
        compiler_params=_cparams(("arbitrary",)),
    )(*args)


def _rope_tables(seq_s, both_halves):
    t = jnp.arange(seq_s)
    rowp = (t // GRID_W).astype(F32)
    colp = (t % GRID_W).astype(F32)
    half = ROPE_B // 2
    inv = ROPE_BASE ** (-jnp.arange(0, half, 2, dtype=F32) / half)
    ar, ac = rowp[:, None] * inv, colp[:, None] * inv
    zero = jnp.zeros_like(ar)
    cos = jnp.concatenate([jnp.cos(ar), jnp.cos(ar), jnp.cos(ac), jnp.cos(ac)], -1)
    sin_dn = jnp.concatenate([zero, jnp.sin(ar), zero, jnp.sin(ac)], -1)
    sin_up = jnp.concatenate([-jnp.sin(ar), zero, -jnp.sin(ac), zero], -1)
    if both_halves:
        parts = [jnp.concatenate([p, p], -1) for p in (cos, sin_dn, sin_up)]
    else:
        one, zz = jnp.ones_like(cos), jnp.zeros_like(cos)
        parts = [jnp.concatenate([cos, one], -1), jnp.concatenate([sin_dn, zz], -1), jnp.concatenate([sin_up, zz], -1)]
    ident = [jnp.ones((seq_s, LANES), F32), jnp.zeros((seq_s, LANES), F32), jnp.zeros((seq_s, LANES), F32)]
    return jnp.stack([jnp.concatenate([i_, p], 0) for i_, p in zip(ident, parts)], 0)


def _even_weights(w_in, w_dec, b_dec, w_uq):
    d = w_in.shape[0]
    sizes = (H_A * DK_A, H_A * DK_A, H_A * DV_A, H_A * DV_A, GLA_LR, GLA_LR, Q_RANK, KV_RANK, ROPE_B)
    offs = np.concatenate([[0], np.cumsum(sizes)])
    qa, ka, va, ga, lrf, lrb, cq, ckv, kr = [w_in[:, offs[j]:offs[j + 1]] for j in range(9)]
    qa = qa * (DK_A ** -0.5)
    qk = jnp.concatenate([qa.reshape(d, H_A, DK_A), ka.reshape(d, H_A, DK_A)], -1).reshape(d, 2 * H_A * DK_A)
    pad = jnp.zeros((d, LANES - ROPE_B - 2 * GLA_LR), w_in.dtype)
    w = jnp.concatenate([qk, va, ga, cq, ckv, kr, lrf, lrb, pad], -1).astype(BF16)
    wd = jnp.zeros((LANES, H_A, 2, DK_A), F32)
    wd = wd.at[ROPE_B:ROPE_B + GLA_LR, :, 0, :].set(w_dec[0].reshape(GLA_LR, H_A, DK_A))
    wd = wd.at[ROPE_B + GLA_LR:ROPE_B + 2 * GLA_LR, :, 1, :].set(w_dec[1].reshape(GLA_LR, H_A, DK_A))
    wd = wd.reshape(LANES, 2 * H_A * DK_A).astype(BF16)
    bd = jnp.stack([b_dec[0].reshape(H_A, DK_A), b_dec[1].reshape(H_A, DK_A)], 1).reshape(1, 2 * H_A * DK_A)
    wq = w_uq.reshape(Q_RANK, H_B, NOPE_B + ROPE_B)
    wq = jnp.concatenate([wq, jnp.zeros((Q_RANK, H_B, 2 * LANES - NOPE_B - ROPE_B), w_uq.dtype)], -1)
    wq = wq.reshape(Q_RANK, H_B * 2 * LANES).astype(BF16)
    return w, wd, bd, wq


def kernel(x_prompt, x_sample, state_gla, cache_mla_ckv, cache_mla_krope, cache_diff_k, cache_diff_v, c, c_ctx, w_ada, b_ada, norm_gain, final_gain, w_in_even, w_out_even, gla_w_decay, gla_b_decay, gla_norm, mla_q_norm, mla_kv_norm, mla_w_uq, mla_w_ukv, w_in_odd, w_out_odd, diff_lambda, diff_norm, ffn_w_up, ffn_conv_w, ffn_conv_b, ffn_w_down):
    nb_p, seq_p, d = x_prompt.shape
    nb_s, seq_s, _ = x_sample.shape
    past = cache_mla_ckv.shape[2]
    depth = w_ada.shape[0]
    n_even, n_odd = w_in_even.shape[0], w_in_odd.shape[0]
    n_p, n_s = nb_p * seq_p, nb_s * seq_s
    lay = _Layout(n_p, n_s, seq_p, seq_s, TM)
    lay_ffn = _Layout(n_p, n_s, seq_p, seq_s, FFN_TM)
    per_tile = TM // seq_p
    assert seq_s % GRID_W == 0 and 1 + nb_s <= ADA_ROWS
    assert n_p % seq_s == 0 and seq_p & (seq_p - 1) == 0 and seq_s & (seq_s - 1) == 0
    assert seq_p % GLA_C == 0 and seq_s % GLA_C == 0

    cond = jnp.concatenate([c_ctx[None, :], c, jnp.zeros((ADA_ROWS - 1 - nb_s, d), F32)], 0)
    mods = _ada_call(cond, w_ada, b_ada).reshape(depth, ADA_ROWS, 6, d)

    tab_mla = _rope_tables(seq_s, both_halves=False)
    tab_diff = _rope_tables(seq_s, both_halves=True)
    gla_consts = _gla_consts(GLA_C)
    wup_all = ffn_w_up.astype(BF16)
    wdn_all = ffn_w_down.astype(BF16)
    cb_all = ffn_conv_b.reshape(depth, 1, -1)
    wo_even = w_out_even.astype(BF16)
    wo_odd = w_out_odd.astype(BF16)
    fgain = final_gain.reshape(1, d)
    dgains = diff_norm.reshape(n_odd, 1, 2 * DH_C)

    x = (x_prompt.reshape(n_p, d), x_sample.reshape(n_s, d))
    st_new = caches_even = caches_odd = None
    for l in range(depth):
        i = l // 2
        if l % 2 == 0:
            slot = _Slot(caches_even, i, n_even, per_tile)
            w, wd, bd, wq = _even_weights(w_in_even[i], gla_w_decay[i], gla_b_decay[i], mla_w_uq[i])
            wukv = mla_w_ukv[i].astype(BF16)
            qk, la, v, ga, qb, kra, kv, ckv_new, kr_new = _even_proj_call(
                lay, x, mods, l, norm_gain, w, wd, bd, mla_q_norm[i].reshape(1, -1), mla_kv_norm[i].reshape(1, -1),
                wq, wukv, tab_mla, slot, caches_even)
            caches_even = (ckv_new, kr_new)
            ggain = gla_norm[i].reshape(1, DV_A)
            og_p, st_new = _gla_call(qk, la, v, ga, ggain, gla_consts, None, 0, 0, nb_p, seq_p,
                                     _Slot(st_new, i, n_even), st_new)
            og_s, = _gla_call(qk, la, v, ga, ggain, gla_consts, state_gla, i, n_p, nb_s, seq_s, None, None)
            kv_ctx, kr_ctx = _kv_expand_call(cache_mla_ckv, cache_mla_krope, i, wukv)
            om_p = _mla_attn_call(qb, [(kv, kra, 0, seq_p)], 0, nb_p, seq_p)
            om_s = _mla_attn_call(qb, [(kv_ctx, kr_ctx, 0, past), (kv, kra, n_p, seq_s)], n_p, nb_s, seq_s)
            pairs, wo = [(og_p, og_s), (om_p, om_s)], wo_even
        else:
            slot = _Slot(caches_odd, i, n_odd, per_tile)
            lam_init = 0.8 - 0.6 * math.exp(-0.3 * l)
            n = H_C * 2 * DH_C
            col_scale = np.concatenate([np.full(n, DIFF_SCALE * LOG2E, np.float32), np.ones(2 * n, np.float32)])
            wi = (w_in_odd[i] * col_scale).astype(BF16)
            q, kb, vb, kf_new, vf_new = _odd_proj_call(lay, x, mods, l, norm_gain, wi, tab_diff, slot, caches_odd)
            caches_odd = (kf_new, vf_new)
            od_p = _diff_attn_call(q, diff_lambda, dgains, i, None, [(kb, vb, 0, seq_p)], 0, nb_p, seq_p, lam_init)
            od_s = _diff_attn_call(q, diff_lambda, dgains, i, (cache_diff_k, cache_diff_v), [(kb, vb, n_p, seq_s)],
                                   n_p, nb_s, seq_s, lam_init)
            pairs, wo = [(od_p, od_s)], wo_odd
        x = _ffn_call(lay_ffn, x, pairs, mods, l, norm_gain, wo, i, wup_all, ffn_conv_w, cb_all, wdn_all, fgain,
                      final=(l == depth - 1))
        if l < depth - 1:
            x = x[0]
    y_prompt = x[0].reshape(nb_p, seq_p, d)
    y_sample = x[1].reshape(nb_s, seq_s, d)
    diff_shape = (nb_p, n_odd, seq_p, H_C, 2 * DH_C)
    return (y_prompt, y_sample, st_new, caches_even[0], caches_even[1],
            caches_odd[0].reshape(diff_shape), caches_odd[1].reshape(diff_shape))
```

```python
import functools
import math

import numpy as np
import jax
import jax.numpy as jnp
from jax import lax
from jax.experimental import pallas as pl
from jax.experimental.pallas import tpu as pltpu

GRID_W = 64
ROPE_BASE = 10000.0
EPS = 1e-6
H_A, DK_A, DV_A = 4, 64, 128
GLA_LR = 16
GLA_TAU = 16.0
H_B, Q_RANK, KV_RANK, NOPE_B, ROPE_B, V_B = 4, 256, 256, 128, 64, 128
MLA_SCALE = (NOPE_B + ROPE_B) ** -0.5
H_C, DH_C = 8, 64
DIFF_SCALE = DH_C ** -0.5
LOG2E = math.log2(math.e)

LANES = 128
SUBLANES = 8
HALO = 16
VMEM_LIMIT = 56 * 1024 * 1024

TM = 512
TQ = 256
MLA_TQ = 512
GLA_C = 64
GLA_STEPS = 1
GLA_HEADS = 4
GLA_UNROLL = 2
ADA_TN = 1536
FFN_TM = 512
FFN_TF = 256
FFN_AHEAD = 2
FFN_DOWN = 4
ADA_ROWS = 16

F32 = jnp.float32
BF16 = jnp.bfloat16


def _dot(a, b):
    return jnp.dot(a, b, preferred_element_type=F32)


def _dot_nt(a, b):
    return lax.dot_general(a, b, (((1,), (1,)), ((), ())), preferred_element_type=F32)


def _silu(x):
    return x * (1.0 / (1.0 + jnp.exp(-x)))


def _rms(x, gain):
    return x * lax.rsqrt(jnp.mean(x * x, axis=-1, keepdims=True) + EPS) * gain


def _normmod(x, gain, shift, scale):
    return _rms(x, gain) * (1.0 + scale) + shift


def _rope(x, cos, sin_dn, sin_up):
    return (x * cos + pltpu.roll(x, 16, axis=1) * sin_dn + pltpu.roll(x, LANES - 16, axis=1) * sin_up)


def _cparams(sem, vmem=VMEM_LIMIT):
    return pltpu.CompilerParams(dimension_semantics=sem, vmem_limit_bytes=vmem)


def _resident(shape):
    nd = len(shape)
    return pl.BlockSpec(shape, lambda *_: (0,) * nd, pipeline_mode=pl.Buffered(1))


def _layer_spec(shape, l):
    nd = len(shape)
    return pl.BlockSpec((1,) + tuple(shape[1:]), lambda *_: (l,) + (0,) * (nd - 1), pipeline_mode=pl.Buffered(1))


def _ada_kernel(c_ref, w_ref, b_ref, o_ref):
    s = _silu(c_ref[...]).astype(BF16)
    o_ref[0] = _dot(s, w_ref[0].astype(BF16)) + b_ref[0]


def _ada_call(cond, w_ada, b_ada):
    depth, d, n = w_ada.shape
    return pl.pallas_call(
        _ada_kernel,
        grid=(depth, n // ADA_TN),
        in_specs=[
            pl.BlockSpec((ADA_ROWS, d), lambda l, j: (0, 0)),
            pl.BlockSpec((1, d, ADA_TN), lambda l, j: (l, 0, j)),
            pl.BlockSpec((1, 1, ADA_TN), lambda l, j: (l, 0, j)),
        ],
        out_specs=pl.BlockSpec((1, ADA_ROWS, ADA_TN), lambda l, j: (l, 0, j)),
        out_shape=jax.ShapeDtypeStruct((depth, ADA_ROWS, n), F32),
        name="ada",
        compiler_params=_cparams(("arbitrary", "arbitrary")),
    )(cond, w_ada, b_ada.reshape(depth, 1, n))


class _Layout:
    def __init__(self, n_prompt, n_sample, seq_p, seq_s, tm):
        assert tm % seq_p == 0 and n_prompt % tm == 0 and n_sample % tm == 0 and seq_s % tm == 0
        self.tm = tm
        self.np, self.ns, self.nt = n_prompt, n_sample, n_prompt + n_sample
        self.seq_p, self.seq_s = seq_p, seq_s
        self.pt = n_prompt // tm
        self.st = n_sample // tm
        self.tiles = self.pt + self.st
        self.tiles_per_seq_s = seq_s // tm

    def group(self, i):
        return jnp.where(i < self.pt, 0, 1 + (i - self.pt) // self.tiles_per_seq_s)

    def pos_block(self, i):
        return jnp.where(i < self.pt, 0, self.tiles_per_seq_s + (i - self.pt) % self.tiles_per_seq_s)

    def p_block(self, i):
        return jnp.minimum(i, self.pt - 1)

    def s_block(self, i):
        return jnp.maximum(i - self.pt, 0)


def _mod_spec(lay, l, d):
    return pl.BlockSpec((1, 1, 6, d), lambda i: (l, lay.group(i), 0, 0))


def _tab_spec(lay):
    return pl.BlockSpec((3, lay.tm, LANES), lambda i: (0, lay.pos_block(i), 0))


def _x_specs(lay, x):
    if isinstance(x, tuple):
        d = x[0].shape[1]
        return [pl.BlockSpec((lay.tm, d), lambda i: (lay.p_block(i), 0)),
                pl.BlockSpec((lay.tm, d), lambda i: (lay.s_block(i), 0))], list(x)
    return [pl.BlockSpec((lay.tm, x.shape[1]), lambda i: (i, 0))], [x]


def _read_x(x_refs, prompt_tiles):
    if len(x_refs) == 1:
        return x_refs[0][...]
    return jnp.where(pl.program_id(0) < prompt_tiles, x_refs[0][...], x_refs[1][...])


class _Slot:
    def __init__(self, prev, slot, n_slots, per_block=1):
        self.prev, self.slot, self.n_slots, self.per_block = prev, slot, n_slots, per_block

    @property
    def first(self):
        return self.prev is None

    def spec(self, tail, index):
        zeros = (0,) * len(tail)
        if self.first:
            return pl.BlockSpec((self.per_block, self.n_slots) + tail, lambda *g: (index(*g), 0) + zeros)
        s = self.slot
        return pl.BlockSpec((self.per_block, 1) + tail, lambda *g: (index(*g), s) + zeros)

    def shape(self, nb, tail):
        return jax.ShapeDtypeStruct((nb, self.n_slots) + tail, F32)

    def at(self):
        return self.slot if self.first else 0

    def zero_others(self, ref):
        if self.first:
            for b in range(self.per_block):
                for s in range(self.n_slots):
                    if s != self.slot:
                        ref[b, s] = jnp.zeros(ref.shape[2:], F32)


def _even_proj_kernel(*refs, nx, prompt_tiles, slot):
    x_refs = refs[:nx]
    (mod_ref, gain_ref, w_ref, wd_ref, bd_ref, qn_ref, kvn_ref, wuq_ref, wukv_ref, tab_ref) = refs[nx:nx + 10]
    outs = refs[len(refs) - 9:]
    qk_ref, la_ref, v_ref, ga_ref, qb_ref, kra_ref, kv_ref, ckv_ref, kr_ref = outs
    mod = mod_ref[0, 0]
    seq = ckv_ref.shape[2]
    parts = [slice(b * seq, (b + 1) * seq) for b in range(slot.per_block)]
    x = _read_x(x_refs, prompt_tiles)
    ys = [_dot(_normmod(x[r], gain_ref[0, 0:1], mod[0:1], mod[1:2]).astype(BF16), w_ref[...]) for r in parts]
    ckvns, gs = [], []
    for r, y in zip(parts, ys):
        qk_ref[r, :] = y[:, 0:512]
        v_ref[r, :] = y[:, 512:1024].astype(BF16)
        ga_ref[r, :] = y[:, 1024:1536]
        cq, ckv, g = y[:, 1536:1792], y[:, 1792:2048], y[:, 2048:2176]
        pre = _dot(g.astype(BF16), wd_ref[...]) + bd_ref[...]
        la_ref[r, :] = (jnp.minimum(pre, 0.0) - jnp.log(1.0 + jnp.exp(-jnp.abs(pre)))) * (LOG2E / GLA_TAU)
        cos, sin_dn, sin_up = tab_ref[0, r, :], tab_ref[1, r, :], tab_ref[2, r, :]
        qb = _dot(_rms(cq, qn_ref[...]).astype(BF16), wuq_ref[...])
        for hh in range(H_B):
            lo = hh * 2 * LANES
            qb_ref[r, lo:lo + LANES] = qb[:, lo:lo + LANES].astype(BF16)
            qb_ref[r, lo + LANES:lo + 2 * LANES] = _rope(qb[:, lo + LANES:lo + 2 * LANES], cos, sin_dn, sin_up).astype(BF16)
        ckvn = _rms(ckv, kvn_ref[...])
        kv_ref[r, :] = _dot(ckvn.astype(BF16), wukv_ref[...]).astype(BF16)
        lane = lax.broadcasted_iota(jnp.int32, g.shape, 1)
        kra_ref[r, :] = jnp.where(lane < ROPE_B, _rope(g, cos, sin_dn, sin_up), 0.0).astype(BF16)
        ckvns.append(ckvn)
        gs.append(g)

    @pl.when(pl.program_id(0) < prompt_tiles)
    def _():
        for b in range(slot.per_block):
            ckv_ref[b, slot.at()] = ckvns[b]
            kr_ref[b, slot.at()] = gs[b][:, :ROPE_B]
        slot.zero_others(ckv_ref)
        slot.zero_others(kr_ref)


def _even_proj_call(lay, x, mods, l, gains, w, wd, bd, qn, kvn, wuq, wukv, tab, slot, prev):
    d = w.shape[0]
    x_specs, x_args = _x_specs(lay, x)
    row = lambda n: pl.BlockSpec((TM, n), lambda i: (i, 0))
    outs = [(512, F32), (512, F32), (512, BF16), (512, F32), (1024, BF16), (LANES, BF16), (1024, BF16)]
    in_specs = x_specs + [_mod_spec(lay, l, d), _layer_spec(gains.shape, l), _resident(w.shape), _resident(wd.shape),
                          _resident(bd.shape), _resident(qn.shape), _resident(kvn.shape), _resident(wuq.shape),
                          _resident(wukv.shape), _tab_spec(lay)]
    args = x_args + [mods, gains, w, wd, bd, qn, kvn, wuq, wukv, tab]
    aliases = {}
    if not slot.first:
        aliases = {len(args): len(outs), len(args) + 1: len(outs) + 1}
        in_specs += [pl.BlockSpec(memory_space=pl.ANY), pl.BlockSpec(memory_space=pl.ANY)]
        args += list(prev)
    nb_p = lay.np // lay.seq_p
    return pl.pallas_call(
        functools.partial(_even_proj_kernel, nx=len(x_args), prompt_tiles=lay.pt, slot=slot),
        grid=(lay.tiles,),
        in_specs=in_specs,
        out_specs=[row(n) for n, _ in outs]
        + [slot.spec((lay.seq_p, KV_RANK), lay.p_block), slot.spec((lay.seq_p, ROPE_B), lay.p_block)],
        out_shape=[jax.ShapeDtypeStruct((lay.nt, n), dt) for n, dt in outs]
        + [slot.shape(nb_p, (lay.seq_p, KV_RANK)), slot.shape(nb_p, (lay.seq_p, ROPE_B))],
        input_output_aliases=aliases,
        name="even_proj",
        compiler_params=_cparams(("arbitrary",)),
    )(*args)


def _gla_consts(c):
    levels = int(math.log2(c))
    idx = np.arange(c)
    cum = idx[None, :] <= idx[:, None]
    rows, sgn, masks = [cum], [], []
    for lev in range(levels):
        half = 1 << lev
        node = idx // (2 * half)
        mid = node * 2 * half + half
        upper = idx >= mid
        rows.append(cum[mid - 1])
        sgn.append(np.repeat(np.where(upper, 1.0, -1.0)[:, None], LANES, 1))
        same = node[:, None] == node[None, :]
        masks.append(np.concatenate([same & upper[:, None] & ~upper[None, :],
                                     same & ~upper[:, None] & upper[None, :]], 1))
    eye = np.eye(c, dtype=bool)
    masks.append(np.concatenate([eye, eye], 1))
    m = np.concatenate(rows, 0)
    return (jnp.asarray(np.concatenate([m, m, m], 1), BF16), jnp.asarray(np.stack(sgn, 0), F32),
            jnp.asarray(np.stack(masks, 0), F32), levels)


def _gla_kernel(*refs, t, has_s0, slot, levels):
    qk_ref, la_ref, v_ref, ga_ref, gain_ref, m3_ref, sg_ref, mask_ref = refs[:8]
    s0_ref = refs[8] if has_s0 else None
    n_out = 1 if slot is None else 2
    o_ref = refs[len(refs) - 3 - n_out]
    st_ref = None if slot is None else refs[len(refs) - 4]
    of_ref, ob_ref, st_scr = refs[len(refs) - 3:]
    c = GLA_C
    n = t // c
    heads = range(H_A)
    lane = lax.broadcasted_iota(jnp.int32, (1, LANES), 1)
    is_fwd = lane < DK_A
    keep_f = jnp.where(is_fwd, 1.0, 0.0).astype(BF16)
    keep_b = jnp.where(is_fwd, 0.0, 1.0).astype(BF16)

    def split(x):
        return jnp.concatenate([x * keep_f, x * keep_b], axis=0)

    for h in heads:
        if has_s0:
            st_scr[h] = jnp.concatenate([jnp.transpose(s0_ref[0, 0, 0, h]), jnp.transpose(s0_ref[0, 0, 1, h])], axis=1)
        else:
            st_scr[h] = jnp.zeros((DV_A, LANES), F32)

    steps = GLA_STEPS if n % GLA_STEPS == 0 else 1

    def body(j, carry):
        rf = [pl.multiple_of((j * steps + s) * c, c) for s in range(steps)]
        rb = [pl.multiple_of((n - 1 - j * steps - s) * c, c) for s in range(steps)]
        sl = [slice(h * LANES, (h + 1) * LANES) for h in heads]
        for g in range(0, H_A, GLA_HEADS):
            process([(s, h) for s in range(steps) for h in heads[g:g + GLA_HEADS]], rf, rb, sl)
        return carry

    def process(units, rf, rb, sl):
        qq, kk, ll, split3 = [], [], [], []
        for s, h in units:
            qkf, qkb = qk_ref[pl.ds(rf[s], c), sl[h]], qk_ref[pl.ds(rb[s], c), sl[h]]
            qq.append(jnp.where(is_fwd, qkf, pltpu.roll(qkb, DK_A, axis=1)))
            kk.append(jnp.where(is_fwd, pltpu.roll(qkf, DK_A, axis=1), qkb))
            ll.append(jnp.where(is_fwd, la_ref[pl.ds(rf[s], c), sl[h]], la_ref[pl.ds(rb[s], c), sl[h]]))
            hi = ll[-1].astype(BF16)
            r1 = ll[-1] - hi.astype(F32)
            mid = r1.astype(BF16)
            lo = (r1 - mid.astype(F32)).astype(BF16)
            split3.append(jnp.concatenate([hi, mid, lo], axis=0))
        e_all = _dot(m3_ref[...], jnp.concatenate(split3, axis=1))
        e = [e_all[:, k * LANES:(k + 1) * LANES] for k in range(len(units))]
        cx = [jnp.where(is_fwd, e[k][0:c], e[k][0:c] - ll[k]) for k in range(len(units))]
        a = [_dot_nt(qq[k].astype(BF16), split(kk[k].astype(BF16))) * mask_ref[levels] for k in range(len(units))]
        for lev in range(levels):
            for k in range(len(units)):
                ex = jnp.exp2(sg_ref[lev] * (cx[k] - e[k][(lev + 1) * c:(lev + 2) * c]))
                a[k] = a[k] + _dot_nt((qq[k] * ex).astype(BF16), split((kk[k] * ex).astype(BF16))) * mask_ref[lev]
        for k, (s, h) in enumerate(units):
            tot = e[k][c - 1:c]
            u = jnp.exp2(cx[k])
            w = jnp.exp2(tot - cx[k])
            qd = (qq[k] * jnp.where(is_fwd, u, w)).astype(BF16)
            kd = (kk[k] * jnp.where(is_fwd, w, u)).astype(BF16)
            vs = jnp.concatenate([v_ref[pl.ds(rf[s], c), sl[h]], v_ref[pl.ds(rb[s], c), sl[h]]], axis=0)
            st = st_scr[h]
            o = _dot(split(a[k].astype(BF16)), vs) + _dot_nt(split(qd), st.astype(BF16))
            of_ref[pl.ds(rf[s], c), sl[h]] = o[:c]
            ob_ref[pl.ds(rb[s], c), sl[h]] = o[c:]
            vt = jnp.transpose(vs.astype(F32)).astype(BF16)
            st_scr[h] = st * jnp.exp2(tot) + _dot(vt, split(kd))

    lax.fori_loop(0, n // steps, body, 0, unroll=GLA_UNROLL)
    for h in heads:
        sl = slice(h * LANES, (h + 1) * LANES)
        o = _rms(of_ref[:, sl] + ob_ref[:, sl], gain_ref[...]) * _silu(ga_ref[:, sl])
        o_ref[:, sl] = o.astype(BF16)
        if slot is not None:
            st = st_scr[h]
            st_ref[0, slot.at(), 0, h] = jnp.transpose(st[:, :DK_A])
            st_ref[0, slot.at(), 1, h] = jnp.transpose(st[:, DK_A:])
    if slot is not None:
        slot.zero_others(st_ref)


def _gla_call(qk, la, v, ga, gain, consts, s0, s0_slot, row0, nb, t, slot, prev):
    m3, sg, mask, levels = consts
    blk0 = row0 // t
    width = H_A * LANES
    tok = pl.BlockSpec((t, width), lambda b: (blk0 + b, 0))
    in_specs = [tok, tok, tok, tok, _resident(gain.shape), _resident(m3.shape), _resident(sg.shape), _resident(mask.shape)]
    args = [qk, la, v, ga, gain, m3, sg, mask]
    tail = (2, H_A, DK_A, DV_A)
    if s0 is not None:
        in_specs.append(pl.BlockSpec((1, 1) + tail, lambda b: (b, s0_slot, 0, 0, 0, 0)))
        args.append(s0)
    out_specs = [pl.BlockSpec((t, width), lambda b: (b, 0))]
    out_shape = [jax.ShapeDtypeStruct((nb * t, width), BF16)]
    aliases = {}
    if slot is not None:
        out_specs.append(slot.spec(tail, lambda b: b))
        out_shape.append(slot.shape(nb, tail))
        if not slot.first:
            aliases = {len(args): 1}
            in_specs.append(pl.BlockSpec(memory_space=pl.ANY))
            args.append(prev)
    return pl.pallas_call(
        functools.partial(_gla_kernel, t=t, has_s0=s0 is not None, slot=slot, levels=levels),
        grid=(nb,),
        in_specs=in_specs,
        out_specs=out_specs,
        out_shape=out_shape,
        scratch_shapes=[pltpu.VMEM((t, width), F32), pltpu.VMEM((t, width), F32), pltpu.VMEM((H_A, DV_A, LANES), F32)],
        input_output_aliases=aliases,
        name="gla",
        compiler_params=_cparams(("arbitrary",)),
    )(*args)


def _kv_expand_kernel(c_ref, r_ref, w_ref, o_ref, ro_ref):
    o_ref[...] = _dot(c_ref[0, 0].astype(BF16), w_ref[...]).astype(BF16)
    kr = r_ref[0, 0]
    ro_ref[...] = jnp.concatenate([kr, jnp.zeros((kr.shape[0], LANES - ROPE_B), F32)], axis=1).astype(BF16)


def _kv_expand_call(cache_ckv, cache_kr, slot, wukv):
    nb, _, past, r = cache_ckv.shape
    return pl.pallas_call(
        _kv_expand_kernel,
        grid=(nb,),
        in_specs=[pl.BlockSpec((1, 1, past, r), lambda b: (b, slot, 0, 0)),
                  pl.BlockSpec((1, 1, past, ROPE_B), lambda b: (b, slot, 0, 0)), _resident(wukv.shape)],
        out_specs=[pl.BlockSpec((past, wukv.shape[1]), lambda b: (b, 0)), pl.BlockSpec((past, LANES), lambda b: (b, 0))],
        out_shape=[jax.ShapeDtypeStruct((nb * past, wukv.shape[1]), BF16), jax.ShapeDtypeStruct((nb * past, LANES), BF16)],
        name="kv_expand",
        compiler_params=_cparams(("arbitrary",)),
    )(cache_ckv, cache_kr, wukv)


def _mla_attn_kernel(*refs, nseg):
    q_ref = refs[0]
    seg = [(refs[1 + 2 * s], refs[2 + 2 * s]) for s in range(nseg)]
    o_ref = refs[1 + 2 * nseg]
    def qk(hh):
        q = q_ref[:, hh * 2 * LANES:(hh + 1) * 2 * LANES]
        scores = []
        for kv_ref, kr_ref in seg:
            kcat = jnp.concatenate([kv_ref[:, hh * 2 * LANES:hh * 2 * LANES + LANES], kr_ref[...]], axis=1)
            scores.append(_dot_nt(q, kcat) * (MLA_SCALE * LOG2E))
        return scores

    nxt = qk(0)
    for hh in range(H_B):
        scores = nxt
        if hh + 1 < H_B:
            nxt = qk(hh + 1)
        mx = functools.reduce(jnp.maximum, [jnp.max(s, axis=-1, keepdims=True) for s in scores])
        ps = [jnp.exp2(s - mx) for s in scores]
        den = functools.reduce(lambda a, b: a + b, [jnp.sum(p, axis=-1, keepdims=True) for p in ps])
        acc = functools.reduce(lambda a, b: a + b, [
            _dot(p.astype(BF16), kv_ref[:, hh * 2 * LANES + LANES:(hh + 1) * 2 * LANES])
            for p, (kv_ref, _) in zip(ps, seg)])
        o_ref[:, hh * LANES:(hh + 1) * LANES] = (acc * (1.0 / den)).astype(BF16)


def _mla_attn_call(qb, segs, row0, nb, t):
    tq = min(t, MLA_TQ)
    assert t % tq == 0 and row0 % tq == 0
    q0 = row0 // tq
    per = t // tq
    in_specs = [pl.BlockSpec((tq, qb.shape[1]), lambda b, j: (q0 + b * per + j, 0))]
    args = [qb]
    for kv, kr, r0, s in segs:
        b0 = r0 // s
        in_specs.append(pl.BlockSpec((s, kv.shape[1]), lambda b, j, b0=b0: (b0 + b, 0)))
        in_specs.append(pl.BlockSpec((s, LANES), lambda b, j, b0=b0: (b0 + b, 0)))
        args += [kv, kr]
    return pl.pallas_call(
        functools.partial(_mla_attn_kernel, nseg=len(segs)),
        grid=(nb, per),
        in_specs=in_specs,
        out_specs=pl.BlockSpec((tq, H_B * V_B), lambda b, j: (b * per + j, 0)),
        out_shape=jax.ShapeDtypeStruct((nb * t, H_B * V_B), BF16),
        name="mla_attn",
        compiler_params=_cparams(("arbitrary", "arbitrary")),
    )(*args)


def _odd_proj_kernel(*refs, prompt_tiles, slot):
    x_ref, mod_ref, gain_ref, w_ref, tab_ref = refs[:5]
    q_ref, kb_ref, vb_ref, kf_ref, vf_ref = refs[len(refs) - 5:]
    mod = mod_ref[0, 0]
    n = q_ref.shape[1]
    heads = n // LANES
    seq = kf_ref.shape[2] // heads
    parts = [slice(b * seq, (b + 1) * seq) for b in range(slot.per_block)]
    x = x_ref[...]
    ys = [_dot(_normmod(x[r], gain_ref[0, 0:1], mod[0:1], mod[1:2]).astype(BF16), w_ref[...]) for r in parts]
    for r, y in zip(parts, ys):
        cos, sin_dn, sin_up = tab_ref[0, r, :], tab_ref[1, r, :], tab_ref[2, r, :]
        vb_ref[r, :] = y[:, 2 * n:3 * n].astype(BF16)
        for g in range(heads):
            lo = g * LANES
            q_ref[r, lo:lo + LANES] = _rope(y[:, lo:lo + LANES], cos, sin_dn, sin_up).astype(BF16)
            kb_ref[r, lo:lo + LANES] = _rope(y[:, n + lo:n + lo + LANES], cos, sin_dn, sin_up).astype(BF16)

    @pl.when(pl.program_id(0) < prompt_tiles)
    def _():
        for b in range(slot.per_block):
            for g in range(heads):
                lo = g * LANES
                kf_ref[b, slot.at(), pl.ds(g, seq, stride=heads), :] = ys[b][:, n + lo:n + lo + LANES]
                vf_ref[b, slot.at(), pl.ds(g, seq, stride=heads), :] = ys[b][:, 2 * n + lo:2 * n + lo + LANES]
        slot.zero_others(kf_ref)
        slot.zero_others(vf_ref)


def _odd_proj_call(lay, x, mods, l, gains, w, tab, slot, prev):
    d = x.shape[1]
    n = w.shape[1] // 3
    row = lambda m: pl.BlockSpec((TM, m), lambda i: (i, 0))
    in_specs = [row(d), _mod_spec(lay, l, d), _layer_spec(gains.shape, l), _resident(w.shape), _tab_spec(lay)]
    args = [x, mods, gains, w, tab]
    aliases = {}
    if not slot.first:
        aliases = {len(args): 3, len(args) + 1: 4}
        in_specs += [pl.BlockSpec(memory_space=pl.ANY), pl.BlockSpec(memory_space=pl.ANY)]
        args += list(prev)
    tail = (lay.seq_p * H_C, 2 * DH_C)
    nb_p = lay.np // lay.seq_p
    return pl.pallas_call(
        functools.partial(_odd_proj_kernel, prompt_tiles=lay.pt, slot=slot),
        grid=(lay.tiles,),
        in_specs=in_specs,
        out_specs=[row(n), row(n), row(n), slot.spec(tail, lay.p_block), slot.spec(tail, lay.p_block)],
        out_shape=[jax.ShapeDtypeStruct((lay.nt, n), BF16)] * 3 + [slot.shape(nb_p, tail)] * 2,
        input_output_aliases=aliases,
        name="odd_proj",
        compiler_params=_cparams(("arbitrary",)),
    )(*args)


def _diff_attn_kernel(*refs, nseg, has_ctx, lam_init):
    q_ref, dl_ref, gain_ref = refs[0], refs[1], refs[2]
    seg = [(refs[3 + 2 * s], refs[4 + 2 * s]) for s in range(nseg)]
    o_ref = refs[3 + 2 * nseg]
    if has_ctx:
        (kc_ref, vc_ref), (kc_scr, vc_scr) = seg[0], refs[4 + 2 * nseg:6 + 2 * nseg]

        @pl.when(pl.program_id(1) == 0)
        def _():
            for hh in range(H_C):
                rows = pl.ds(hh, kc_scr.shape[0], stride=H_C)
                kc_scr[:, hh * LANES:(hh + 1) * LANES] = kc_ref[0, 0, rows, :].astype(BF16)
                vc_scr[:, hh * LANES:(hh + 1) * LANES] = vc_ref[0, 0, rows, :].astype(BF16)

        seg[0] = (kc_scr, vc_scr)
    dl = dl_ref[0]
    lam = (jnp.exp(jnp.sum(dl[0:1] * dl[1:2], axis=-1, keepdims=True))
           - jnp.exp(jnp.sum(dl[2:3] * dl[3:4], axis=-1, keepdims=True)) + lam_init)
    tq = q_ref.shape[0]
    lane = lax.broadcasted_iota(jnp.int32, (1, LANES), 1)
    keep1 = jnp.where(lane < DH_C, 1.0, 0.0).astype(BF16)
    keep2 = jnp.where(lane < DH_C, 0.0, 1.0).astype(BF16)
    def qk(hh):
        q = q_ref[:, hh * LANES:(hh + 1) * LANES]
        qq = jnp.concatenate([q * keep1, q * keep2], axis=0)
        return [_dot_nt(qq, k_ref[:, hh * LANES:(hh + 1) * LANES]) for k_ref, _ in seg]

    nxt = qk(0)
    for hh in range(H_C):
        scores = nxt
        if hh + 1 < H_C:
            nxt = qk(hh + 1)
        mx = functools.reduce(jnp.maximum, [jnp.max(s, axis=-1, keepdims=True) for s in scores])
        ps = [jnp.exp2(s - mx) for s in scores]
        den = functools.reduce(lambda a, b: a + b, [jnp.sum(p, axis=-1, keepdims=True) for p in ps])
        acc = functools.reduce(lambda a, b: a + b, [
            _dot(p.astype(BF16), v_ref[:, hh * LANES:(hh + 1) * LANES]) for p, (_, v_ref) in zip(ps, seg)])
        acc = acc * (1.0 / den)
        o = acc[:tq] - lam * acc[tq:]
        o_ref[:, hh * LANES:(hh + 1) * LANES] = (_rms(o, gain_ref[0]) * (1.0 - lam_init)).astype(BF16)


def _diff_attn_call(q, dls, gains, slot, ctx, segs, row0, nb, t, lam_init):
    tq = min(t, TQ)
    assert t % tq == 0 and row0 % tq == 0
    q0 = row0 // tq
    per = t // tq
    n = q.shape[1]
    in_specs = [pl.BlockSpec((tq, n), lambda b, j: (q0 + b * per + j, 0)),
                pl.BlockSpec((1,) + dls.shape[1:], lambda b, j: (slot, 0, 0)),
                pl.BlockSpec((1, 1, gains.shape[2]), lambda b, j: (slot, 0, 0))]
    args = [q, dls, gains]
    scratch = []
    if ctx is not None:
        nb_c, n_l, past, heads, dh = ctx[0].shape
        for cache in ctx:
            in_specs.append(pl.BlockSpec((1, 1, past * heads, dh), lambda b, j: (b, slot, 0, 0)))
            args.append(cache.reshape(nb_c, n_l, past * heads, dh))
        scratch = [pltpu.VMEM((past, n), BF16), pltpu.VMEM((past, n), BF16)]
    for k, v, r0, s in segs:
        b0 = r0 // s
        in_specs.append(pl.BlockSpec((s, n), lambda b, j, b0=b0: (b0 + b, 0)))
        in_specs.append(pl.BlockSpec((s, n), lambda b, j, b0=b0: (b0 + b, 0)))
        args += [k, v]
    return pl.pallas_call(
        functools.partial(_diff_attn_kernel, nseg=len(segs) + (ctx is not None), has_ctx=ctx is not None,
                          lam_init=lam_init),
        grid=(nb, per),
        in_specs=in_specs,
        out_specs=pl.BlockSpec((tq, n), lambda b, j: (b * per + j, 0)),
        out_shape=jax.ShapeDtypeStruct((nb * t, n), BF16),
        scratch_shapes=scratch,
        name="diff_attn",
        compiler_params=_cparams(("arbitrary", "arbitrary")),
    )(*args)


def _halo_specs(rows, width, tm, tile):
    per = tm // HALO
    last = rows // HALO - 1
    return [pl.BlockSpec((HALO, width), lambda i: (jnp.maximum(tile(i) * per - 1, 0), 0)),
            pl.BlockSpec((tm, width), lambda i: (tile(i), 0)),
            pl.BlockSpec((HALO, width), lambda i: (jnp.minimum((tile(i) + 1) * per, last), 0))]


def _ffn_kernel(*refs, lay, d_ff, final, nx, widths):
    pos = 3 * nx + 6 * len(widths)
    trips = [refs[3 * t:3 * t + 3] for t in range(pos // 3)]
    mod_ref, gain_ref, wo_ref, wup_ref, cw_ref, cb_ref, wdn_ref, fg_ref = refs[pos:pos + 8]
    outs = refs[pos + 8:len(refs) - 2]
    u_ref, act_ref = refs[len(refs) - 2:]
    tm = lay.tm
    i = pl.program_id(0)
    row = i * tm
    is_prompt = i < lay.pt
    seq = jnp.where(is_prompt, lay.seq_p, lay.seq_s)
    first = (row & (seq - 1)) == 0
    last = ((row + tm) & (seq - 1)) == 0

    def ext(trip_p, trip_s):
        if trip_s is None:
            return jnp.concatenate([r[...] for r in trip_p], axis=0)
        return jnp.concatenate([jnp.where(is_prompt, a[...], b[...]) for a, b in zip(trip_p, trip_s)], axis=0)

    mod = mod_ref[0, 0]
    x_ext = ext(trips[0], trips[1] if nx == 2 else None)
    proj = None
    k0 = 0
    for p, w in enumerate(widths):
        part = _dot(ext(trips[nx + 2 * p], trips[nx + 2 * p + 1]), wo_ref[0, k0:k0 + w, :])
        k0 += w
        proj = part if proj is None else proj + part
    xm = x_ext + mod[2:3] * proj
    gain, shift, scale = gain_ref[0, 1:2], mod[3:4], mod[4:5]
    hh = _normmod(xm, gain, shift, scale)
    h = jnp.concatenate([jnp.where(first, 0.0, hh[HALO - SUBLANES:HALO]), hh[HALO:HALO + tm],
                         jnp.where(last, 0.0, hh[HALO + tm:HALO + tm + SUBLANES])], axis=0).astype(BF16)
    x = xm[HALO:HALO + tm]
    acc = jnp.zeros((tm, x.shape[1]), F32)
    n_f = d_ff // FFN_TF
    n_slots = u_ref.shape[0]
    slabs = FFN_TF // LANES
    sub = lax.broadcasted_iota(jnp.int32, (SUBLANES, FFN_TF), 0)
    cut_prev = is_prompt & (sub == 0)
    cut_next = is_prompt & (sub == SUBLANES - 1)
    seams = range(lay.seq_p, tm, lay.seq_p)

    def up(f):
        for half, base in enumerate((0, d_ff)):
            lo = base + f * FFN_TF
            u = _dot(h, wup_ref[0, :, lo:lo + FFN_TF])
            for j in range(slabs):
                u_ref[f % n_slots, half * slabs + j] = u[:, j * LANES:(j + 1) * LANES]

    def tap(f, half, row0):
        return jnp.concatenate([u_ref[f % n_slots, half * slabs + j, pl.ds(row0, tm, stride=1), :]
                                for j in range(slabs)], axis=1)

    def conv(f, half):
        lo = half * d_ff + f * FFN_TF
        cw = cw_ref[0, :, lo:lo + FFN_TF]
        prev = tap(f, half, SUBLANES - 1)
        here = tap(f, half, SUBLANES)
        nxt = tap(f, half, SUBLANES + 1)
        for r in seams:
            prev = jnp.concatenate([prev[:r], jnp.where(cut_prev, 0.0, prev[r:r + SUBLANES]), prev[r + SUBLANES:]], axis=0)
            nxt = jnp.concatenate([nxt[:r - SUBLANES], jnp.where(cut_next, 0.0, nxt[r - SUBLANES:r]), nxt[r:]], axis=0)
        return prev * cw[0:1] + here * cw[1:2] + nxt * cw[2:3] + cb_ref[0, :, lo:lo + FFN_TF]

    for f in range(min(FFN_AHEAD, n_f)):
        up(f)
    for f in range(n_f):
        act_ref[:, f * FFN_TF:(f + 1) * FFN_TF] = (_silu(conv(f, 0)) * conv(f, 1)).astype(BF16)
        if f + FFN_AHEAD < n_f:
            up(f + FFN_AHEAD)
        if (f + 1) % FFN_DOWN == 0 or f + 1 == n_f:
            k0 = (f // FFN_DOWN) * FFN_DOWN * FFN_TF
            acc = acc + _dot(act_ref[:, k0:(f + 1) * FFN_TF], wdn_ref[0, k0:(f + 1) * FFN_TF, :])
    y = x + mod[5:6] * acc
    if not final:
        outs[0][...] = y
    else:
        yn = _rms(y, fg_ref[...])

        @pl.when(i < lay.pt)
        def _():
            outs[0][...] = yn

        @pl.when(i >= lay.pt)
        def _():
            outs[1][...] = yn


def _ffn_call(lay, x, pairs, mods, l, gains, wo, slot, wup, cw, cb, wdn, fgain, final):
    d = wo.shape[2]
    d_ff = wdn.shape[1]
    assert d_ff % FFN_TF == 0
    tm = lay.tm
    in_specs, args = [], []
    if isinstance(x, tuple):
        streams = [(x[0], lay.p_block), (x[1], lay.s_block)]
    else:
        streams = [(x, lambda i: i)]
    for a_p, a_s in pairs:
        streams += [(a_p, lay.p_block), (a_s, lay.s_block)]
    for arr, tile in streams:
        in_specs += _halo_specs(arr.shape[0], arr.shape[1], tm, tile)
        args += [arr, arr, arr]
    in_specs += [_mod_spec(lay, l, d), _layer_spec(gains.shape, l), _layer_spec(wo.shape, slot), _layer_spec(wup.shape, l),
                 _layer_spec(cw.shape, l), _layer_spec(cb.shape, l), _layer_spec(wdn.shape, l), _resident((1, d))]
    args += [mods, gains, wo, wup, cw, cb, wdn, fgain]
    if final:
        out_specs = [pl.BlockSpec((tm, d), lambda i: (lay.p_block(i), 0)), pl.BlockSpec((tm, d), lambda i: (lay.s_block(i), 0))]
        out_shape = [jax.ShapeDtypeStruct((lay.np, d), F32), jax.ShapeDtypeStruct((lay.ns, d), F32)]
    else:
        out_specs = [pl.BlockSpec((tm, d), lambda i: (i, 0))]
        out_shape = [jax.ShapeDtypeStruct((lay.nt, d), F32)]
    return pl.pallas_call(
        functools.partial(_ffn_kernel, lay=lay, d_ff=d_ff, final=final, nx=2 if isinstance(x, tuple) else 1,
                          widths=[a_p.shape[1] for a_p, _ in pairs]),
        name="ffn",
        grid=(lay.tiles,),
        in_specs=in_specs,
        out_specs=out_specs,
        out_shape=out_shape,
        scratch_shapes=[pltpu.VMEM((FFN_AHEAD + 1, 2 * FFN_TF // LANES, tm + 2 * SUBLANES, LANES), F32), pltpu.VMEM((tm, d_ff), BF16)],
        compiler_params=_cparams(("arbitrary",)),
    )(*args)


def _rope_tables(seq_s, both_halves):
    t = jnp.arange(seq_s)
    rowp = (t // GRID_W).astype(F32)
    colp = (t % GRID_W).astype(F32)
    half = ROPE_B // 2
    inv = ROPE_BASE ** (-jnp.arange(0, half, 2, dtype=F32) / half)
    ar, ac = rowp[:, None] * inv, colp[:, None] * inv
    zero = jnp.zeros_like(ar)
    cos = jnp.concatenate([jnp.cos(ar), jnp.cos(ar), jnp.cos(ac), jnp.cos(ac)], -1)
    sin_dn = jnp.concatenate([zero, jnp.sin(ar), zero, jnp.sin(ac)], -1)
    sin_up = jnp.concatenate([-jnp.sin(ar), zero, -jnp.sin(ac), zero], -1)
    if both_halves:
        parts = [jnp.concatenate([p, p], -1) for p in (cos, sin_dn, sin_up)]
    else:
        one, zz = jnp.ones_like(cos), jnp.zeros_like(cos)
        parts = [jnp.concatenate([cos, one], -1), jnp.concatenate([sin_dn, zz], -1), jnp.concatenate([sin_up, zz], -1)]
    ident = [jnp.ones((seq_s, LANES), F32), jnp.zeros((seq_s, LANES), F32), jnp.zeros((seq_s, LANES), F32)]
    return jnp.stack([jnp.concatenate([i_, p], 0) for i_, p in zip(ident, parts)], 0)


def _even_weights(w_in, w_dec, b_dec, w_uq):
    d = w_in.shape[0]
    sizes = (H_A * DK_A, H_A * DK_A, H_A * DV_A, H_A * DV_A, GLA_LR, GLA_LR, Q_RANK, KV_RANK, ROPE_B)
    offs = np.concatenate([[0], np.cumsum(sizes)])
    qa, ka, va, ga, lrf, lrb, cq, ckv, kr = [w_in[:, offs[j]:offs[j + 1]] for j in range(9)]
    qa = qa * (DK_A ** -0.5)
    qk = jnp.concatenate([qa.reshape(d, H_A, DK_A), ka.reshape(d, H_A, DK_A)], -1).reshape(d, 2 * H_A * DK_A)
    pad = jnp.zeros((d, LANES - ROPE_B - 2 * GLA_LR), w_in.dtype)
    w = jnp.concatenate([qk, va, ga, cq, ckv, kr, lrf, lrb, pad], -1).astype(BF16)
    wd = jnp.zeros((LANES, H_A, 2, DK_A), F32)
    wd = wd.at[ROPE_B:ROPE_B + GLA_LR, :, 0, :].set(w_dec[0].reshape(GLA_LR, H_A, DK_A))
    wd = wd.at[ROPE_B + GLA_LR:ROPE_B + 2 * GLA_LR, :, 1, :].set(w_dec[1].reshape(GLA_LR, H_A, DK_A))
    wd = wd.reshape(LANES, 2 * H_A * DK_A).astype(BF16)
    bd = jnp.stack([b_dec[0].reshape(H_A, DK_A), b_dec[1].reshape(H_A, DK_A)], 1).reshape(1, 2 * H_A * DK_A)
    wq = w_uq.reshape(Q_RANK, H_B, NOPE_B + ROPE_B)
    wq = jnp.concatenate([wq, jnp.zeros((Q_RANK, H_B, 2 * LANES - NOPE_B - ROPE_B), w_uq.dtype)], -1)
    wq = wq.reshape(Q_RANK, H_B * 2 * LANES).astype(BF16)
    return w, wd, bd, wq


def kernel(x_prompt, x_sample, state_gla, cache_mla_ckv, cache_mla_krope, cache_diff_k, cache_diff_v, c, c_ctx, w_ada, b_ada, norm_gain, final_gain, w_in_even, w_out_even, gla_w_decay, gla_b_decay, gla_norm, mla_q_norm, mla_kv_norm, mla_w_uq, mla_w_ukv, w_in_odd, w_out_odd, diff_lambda, diff_norm, ffn_w_up, ffn_conv_w, ffn_conv_b, ffn_w_down):
    nb_p, seq_p, d = x_prompt.shape
    nb_s, seq_s, _ = x_sample.shape
    past = cache_mla_ckv.shape[2]
    depth = w_ada.shape[0]
    n_even, n_odd = w_in_even.shape[0], w_in_odd.shape[0]
    n_p, n_s = nb_p * seq_p, nb_s * seq_s
    lay = _Layout(n_p, n_s, seq_p, seq_s, TM)
    lay_ffn = _Layout(n_p, n_s, seq_p, seq_s, FFN_TM)
    per_tile = TM // seq_p
    assert seq_s % GRID_W == 0 and 1 + nb_s <= ADA_ROWS
    assert n_p % seq_s == 0 and seq_p & (seq_p - 1) == 0 and seq_s & (seq_s - 1) == 0
    assert seq_p % GLA_C == 0 and seq_s % GLA_C == 0

    cond = jnp.concatenate([c_ctx[None, :], c, jnp.zeros((ADA_ROWS - 1 - nb_s, d), F32)], 0)
    mods = _ada_call(cond, w_ada, b_ada).reshape(depth, ADA_ROWS, 6, d)

    tab_mla = _rope_tables(seq_s, both_halves=False)
    tab_diff = _rope_tables(seq_s, both_halves=True)
    gla_consts = _gla_consts(GLA_C)
    wup_all = ffn_w_up.astype(BF16)
    wdn_all = ffn_w_down.astype(BF16)
    cb_all = ffn_conv_b.reshape(depth, 1, -1)
    wo_even = w_out_even.astype(BF16)
    wo_odd = w_out_odd.astype(BF16)
    fgain = final_gain.reshape(1, d)
    dgains = diff_norm.reshape(n_odd, 1, 2 * DH_C)

    x = (x_prompt.reshape(n_p, d), x_sample.reshape(n_s, d))
    st_new = caches_even = caches_odd = None
    for l in range(depth):
        i = l // 2
        if l % 2 == 0:
            slot = _Slot(caches_even, i, n_even, per_tile)
            w, wd, bd, wq = _even_weights(w_in_even[i], gla_w_decay[i], gla_b_decay[i], mla_w_uq[i])
            wukv = mla_w_ukv[i].astype(BF16)
            qk, la, v, ga, qb, kra, kv, ckv_new, kr_new = _even_proj_call(
                lay, x, mods, l, norm_gain, w, wd, bd, mla_q_norm[i].reshape(1, -1), mla_kv_norm[i].reshape(1, -1),
                wq, wukv, tab_mla, slot, caches_even)
            caches_even = (ckv_new, kr_new)
            ggain = gla_norm[i].reshape(1, DV_A)
            og_p, st_new = _gla_call(qk, la, v, ga, ggain, gla_consts, None, 0, 0, nb_p, seq_p,
                                     _Slot(st_new, i, n_even), st_new)
            og_s, = _gla_call(qk, la, v, ga, ggain, gla_consts, state_gla, i, n_p, nb_s, seq_s, None, None)
            kv_ctx, kr_ctx = _kv_expand_call(cache_mla_ckv, cache_mla_krope, i, wukv)
            om_p = _mla_attn_call(qb, [(kv, kra, 0, seq_p)], 0, nb_p, seq_p)
            om_s = _mla_attn_call(qb, [(kv_ctx, kr_ctx, 0, past), (kv, kra, n_p, seq_s)], n_p, nb_s, seq_s)
            pairs, wo = [(og_p, og_s), (om_p, om_s)], wo_even
        else:
            slot = _Slot(caches_odd, i, n_odd, per_tile)
            lam_init = 0.8 - 0.6 * math.exp(-0.3 * l)
            n = H_C * 2 * DH_C
            col_scale = np.concatenate([np.full(n, DIFF_SCALE * LOG2E, np.float32), np.ones(2 * n, np.float32)])
            wi = (w_in_odd[i] * col_scale).astype(BF16)
            q, kb, vb, kf_new, vf_new = _odd_proj_call(lay, x, mods, l, norm_gain, wi, tab_diff, slot, caches_odd)
            caches_odd = (kf_new, vf_new)
            od_p = _diff_attn_call(q, diff_lambda, dgains, i, None, [(kb, vb, 0, seq_p)], 0, nb_p, seq_p, lam_init)
            od_s = _diff_attn_call(q, diff_lambda, dgains, i, (cache_diff_k, cache_diff_v), [(kb, vb, n_p, seq_s)],
                                   n_p, nb_s, seq_s, lam_init)
            pairs, wo = [(od_p, od_s)], wo_odd
        x = _ffn_call(lay_ffn, x, pairs, mods, l, norm_gain, wo, i, wup_all, ffn_conv_w, cb_all, wdn_all, fgain,
                      final=(l == depth - 1))
        if l < depth - 1:
            x = x[0]
    y_prompt = x[0].reshape(nb_p, seq_p, d)
    y_sample = x[1].reshape(nb_s, seq_s, d)
    diff_shape = (nb_p, n_odd, seq_p, H_C, 2 * DH_C)
    return (y_prompt, y_sample, st_new, caches_even[0], caches_even[1],
            caches_odd[0].reshape(diff_shape), caches_odd[1].reshape(diff_shape))
```

```python
import functools
import math

import numpy as np
import jax
import jax.numpy as jnp
from jax import lax
from jax.experimental import pallas as pl
from jax.experimental.pallas import tpu as pltpu

GRID_W = 64
ROPE_BASE = 10000.0
EPS = 1e-6
H_A, DK_A, DV_A = 4, 64, 128
GLA_LR = 16
GLA_TAU = 16.0
H_B, Q_RANK, KV_RANK, NOPE_B, ROPE_B, V_B = 4, 256, 256, 128, 64, 128
MLA_SCALE = (NOPE_B + ROPE_B) ** -0.5
H_C, DH_C = 8, 64
DIFF_SCALE = DH_C ** -0.5
LOG2E = math.log2(math.e)

LANES = 128
SUBLANES = 8
HALO = 16
VMEM_LIMIT = 56 * 1024 * 1024

TM = 512
TQ = 256
MLA_TQ = 512
ATTN_GROUP = 2
GLA_C = 64
GLA_STEPS = 1
GLA_HEADS = 4
GLA_UNROLL = 2
ADA_TN = 1536
FFN_TM = 512
FFN_TF = 256
FFN_AHEAD = 2
FFN_DOWN = 4
ADA_ROWS = 16

F32 = jnp.float32
BF16 = jnp.bfloat16


def _dot(a, b):
    return jnp.dot(a, b, preferred_element_type=F32)


def _dot_nt(a, b):
    return lax.dot_general(a, b, (((1,), (1,)), ((), ())), preferred_element_type=F32)


def _silu(x):
    return x * (1.0 / (1.0 + jnp.exp(-x)))


def _rms(x, gain):
    return x * lax.rsqrt(jnp.mean(x * x, axis=-1, keepdims=True) + EPS) * gain


def _normmod(x, gain, shift, scale):
    return _rms(x, gain) * (1.0 + scale) + shift


def _rope(x, cos, sin_dn, sin_up):
    return (x * cos + pltpu.roll(x, 16, axis=1) * sin_dn + pltpu.roll(x, LANES - 16, axis=1) * sin_up)


def _cparams(sem, vmem=VMEM_LIMIT):
    return pltpu.CompilerParams(dimension_semantics=sem, vmem_limit_bytes=vmem)


def _resident(shape):
    nd = len(shape)
    return pl.BlockSpec(shape, lambda *_: (0,) * nd, pipeline_mode=pl.Buffered(1))


def _layer_spec(shape, l):
    nd = len(shape)
    return pl.BlockSpec((1,) + tuple(shape[1:]), lambda *_: (l,) + (0,) * (nd - 1), pipeline_mode=pl.Buffered(1))


def _ada_kernel(c_ref, w_ref, b_ref, o_ref):
    s = _silu(c_ref[...]).astype(BF16)
    o_ref[0] = _dot(s, w_ref[0].astype(BF16)) + b_ref[0]


def _ada_call(cond, w_ada, b_ada):
    depth, d, n = w_ada.shape
    return pl.pallas_call(
        _ada_kernel,
        grid=(depth, n // ADA_TN),
        in_specs=[
            pl.BlockSpec((ADA_ROWS, d), lambda l, j: (0, 0)),
            pl.BlockSpec((1, d, ADA_TN), lambda l, j: (l, 0, j)),
            pl.BlockSpec((1, 1, ADA_TN), lambda l, j: (l, 0, j)),
        ],
        out_specs=pl.BlockSpec((1, ADA_ROWS, ADA_TN), lambda l, j: (l, 0, j)),
        out_shape=jax.ShapeDtypeStruct((depth, ADA_ROWS, n), F32),
        name="ada",
        compiler_params=_cparams(("arbitrary", "arbitrary")),
    )(cond, w_ada, b_ada.reshape(depth, 1, n))


class _Layout:
    def __init__(self, n_prompt, n_sample, seq_p, seq_s, tm):
        assert tm % seq_p == 0 and n_prompt % tm == 0 and n_sample % tm == 0 and seq_s % tm == 0
        self.tm = tm
        self.np, self.ns, self.nt = n_prompt, n_sample, n_prompt + n_sample
        self.seq_p, self.seq_s = seq_p, seq_s
        self.pt = n_prompt // tm
        self.st = n_sample // tm
        self.tiles = self.pt + self.st
        self.tiles_per_seq_s = seq_s // tm

    def group(self, i):
        return jnp.where(i < self.pt, 0, 1 + (i - self.pt) // self.tiles_per_seq_s)

    def pos_block(self, i):
        return jnp.where(i < self.pt, 0, self.tiles_per_seq_s + (i - self.pt) % self.tiles_per_seq_s)

    def p_block(self, i):
        return jnp.minimum(i, self.pt - 1)

    def s_block(self, i):
        return jnp.maximum(i - self.pt, 0)


def _mod_spec(lay, l, d):
    return pl.BlockSpec((1, 1, 6, d), lambda i: (l, lay.group(i), 0, 0))


def _tab_spec(lay):
    return pl.BlockSpec((3, lay.tm, LANES), lambda i: (0, lay.pos_block(i), 0))


def _x_specs(lay, x):
    if isinstance(x, tuple):
        d = x[0].shape[1]
        return [pl.BlockSpec((lay.tm, d), lambda i: (lay.p_block(i), 0)),
                pl.BlockSpec((lay.tm, d), lambda i: (lay.s_block(i), 0))], list(x)
    return [pl.BlockSpec((lay.tm, x.shape[1]), lambda i: (i, 0))], [x]


def _read_x(x_refs, prompt_tiles):
    if len(x_refs) == 1:
        return x_refs[0][...]
    return jnp.where(pl.program_id(0) < prompt_tiles, x_refs[0][...], x_refs[1][...])


class _Slot:
    def __init__(self, prev, slot, n_slots, per_block=1):
        self.prev, self.slot, self.n_slots, self.per_block = prev, slot, n_slots, per_block

    @property
    def first(self):
        return self.prev is None

    def spec(self, tail, index):
        zeros = (0,) * len(tail)
        if self.first:
            return pl.BlockSpec((self.per_block, self.n_slots) + tail, lambda *g: (index(*g), 0) + zeros)
        s = self.slot
        return pl.BlockSpec((self.per_block, 1) + tail, lambda *g: (index(*g), s) + zeros)

    def shape(self, nb, tail):
        return jax.ShapeDtypeStruct((nb, self.n_slots) + tail, F32)

    def at(self):
        return self.slot if self.first else 0

    def zero_others(self, ref):
        if self.first:
            for b in range(self.per_block):
                for s in range(self.n_slots):
                    if s != self.slot:
                        ref[b, s] = jnp.zeros(ref.shape[2:], F32)


def _even_proj_kernel(*refs, nx, prompt_tiles, slot):
    x_refs = refs[:nx]
    (mod_ref, gain_ref, w_ref, wd_ref, bd_ref, qn_ref, kvn_ref, wuq_ref, wukv_ref, tab_ref) = refs[nx:nx + 10]
    outs = refs[len(refs) - 9:]
    qk_ref, la_ref, v_ref, ga_ref, qb_ref, kra_ref, kv_ref, ckv_ref, kr_ref = outs
    mod = mod_ref[0, 0]
    seq = ckv_ref.shape[2]
    parts = [slice(b * seq, (b + 1) * seq) for b in range(slot.per_block)]
    x = _read_x(x_refs, prompt_tiles)
    ys = [_dot(_normmod(x[r], gain_ref[0, 0:1], mod[0:1], mod[1:2]).astype(BF16), w_ref[...]) for r in parts]
    ckvns, gs = [], []
    for r, y in zip(parts, ys):
        qk_ref[r, :] = y[:, 0:512]
        v_ref[r, :] = y[:, 512:1024].astype(BF16)
        ga_ref[r, :] = y[:, 1024:1536]
        cq, ckv, g = y[:, 1536:1792], y[:, 1792:2048], y[:, 2048:2176]
        pre = _dot(g.astype(BF16), wd_ref[...]) + bd_ref[...]
        la_ref[r, :] = (jnp.minimum(pre, 0.0) - jnp.log(1.0 + jnp.exp(-jnp.abs(pre)))) * (LOG2E / GLA_TAU)
        cos, sin_dn, sin_up = tab_ref[0, r, :], tab_ref[1, r, :], tab_ref[2, r, :]
        qb = _dot(_rms(cq, qn_ref[...]).astype(BF16), wuq_ref[...])
        for hh in range(H_B):
            lo = hh * 2 * LANES
            qb_ref[r, lo:lo + LANES] = qb[:, lo:lo + LANES].astype(BF16)
            qb_ref[r, lo + LANES:lo + 2 * LANES] = _rope(qb[:, lo + LANES:lo + 2 * LANES], cos, sin_dn, sin_up).astype(BF16)
        ckvn = _rms(ckv, kvn_ref[...])
        kv_ref[r, :] = _dot(ckvn.astype(BF16), wukv_ref[...]).astype(BF16)
        lane = lax.broadcasted_iota(jnp.int32, g.shape, 1)
        kra_ref[r, :] = jnp.where(lane < ROPE_B, _rope(g, cos, sin_dn, sin_up), 0.0).astype(BF16)
        ckvns.append(ckvn)
        gs.append(g)

    @pl.when(pl.program_id(0) < prompt_tiles)
    def _():
        for b in range(slot.per_block):
            ckv_ref[b, slot.at()] = ckvns[b]
            kr_ref[b, slot.at()] = gs[b][:, :ROPE_B]
        slot.zero_others(ckv_ref)
        slot.zero_others(kr_ref)


def _even_proj_call(lay, x, mods, l, gains, w, wd, bd, qn, kvn, wuq, wukv, tab, slot, prev):
    d = w.shape[0]
    x_specs, x_args = _x_specs(lay, x)
    row = lambda n: pl.BlockSpec((TM, n), lambda i: (i, 0))
    outs = [(512, F32), (512, F32), (512, BF16), (512, F32), (1024, BF16), (LANES, BF16), (1024, BF16)]
    in_specs = x_specs + [_mod_spec(lay, l, d), _layer_spec(gains.shape, l), _resident(w.shape), _resident(wd.shape),
                          _resident(bd.shape), _resident(qn.shape), _resident(kvn.shape), _resident(wuq.shape),
                          _resident(wukv.shape), _tab_spec(lay)]
    args = x_args + [mods, gains, w, wd, bd, qn, kvn, wuq, wukv, tab]
    aliases = {}
    if not slot.first:
        aliases = {len(args): len(outs), len(args) + 1: len(outs) + 1}
        in_specs += [pl.BlockSpec(memory_space=pl.ANY), pl.BlockSpec(memory_space=pl.ANY)]
        args += list(prev)
    nb_p = lay.np // lay.seq_p
    return pl.pallas_call(
        functools.partial(_even_proj_kernel, nx=len(x_args), prompt_tiles=lay.pt, slot=slot),
        grid=(lay.tiles,),
        in_specs=in_specs,
        out_specs=[row(n) for n, _ in outs]
        + [slot.spec((lay.seq_p, KV_RANK), lay.p_block), slot.spec((lay.seq_p, ROPE_B), lay.p_block)],
        out_shape=[jax.ShapeDtypeStruct((lay.nt, n), dt) for n, dt in outs]
        + [slot.shape(nb_p, (lay.seq_p, KV_RANK)), slot.shape(nb_p, (lay.seq_p, ROPE_B))],
        input_output_aliases=aliases,
        name="even_proj",
        compiler_params=_cparams(("arbitrary",)),
    )(*args)


def _gla_consts(c):
    levels = int(math.log2(c))
    idx = np.arange(c)
    cum = idx[None, :] <= idx[:, None]
    rows, sgn, masks = [cum], [], []
    for lev in range(levels):
        half = 1 << lev
        node = idx // (2 * half)
        mid = node * 2 * half + half
        upper = idx >= mid
        rows.append(cum[mid - 1])
        sgn.append(np.repeat(np.where(upper, 1.0, -1.0)[:, None], LANES, 1))
        same = node[:, None] == node[None, :]
        masks.append(np.concatenate([same & upper[:, None] & ~upper[None, :],
                                     same & ~upper[:, None] & upper[None, :]], 1))
    eye = np.eye(c, dtype=bool)
    masks.append(np.concatenate([eye, eye], 1))
    m = np.concatenate(rows, 0)
    return (jnp.asarray(np.concatenate([m, m, m], 1), BF16), jnp.asarray(np.stack(sgn, 0), F32),
            jnp.asarray(np.stack(masks, 0), F32), levels)


def _gla_kernel(*refs, t, has_s0, slot, levels):
    qk_ref, la_ref, v_ref, ga_ref, gain_ref, m3_ref, sg_ref, mask_ref = refs[:8]
    s0_ref = refs[8] if has_s0 else None
    n_out = 1 if slot is None else 2
    o_ref = refs[len(refs) - 3 - n_out]
    st_ref = None if slot is None else refs[len(refs) - 4]
    of_ref, ob_ref, st_scr = refs[len(refs) - 3:]
    c = GLA_C
    n = t // c
    heads = range(H_A)
    lane = lax.broadcasted_iota(jnp.int32, (1, LANES), 1)
    is_fwd = lane < DK_A
    keep_f = jnp.where(is_fwd, 1.0, 0.0).astype(BF16)
    keep_b = jnp.where(is_fwd, 0.0, 1.0).astype(BF16)

    def split(x):
        return jnp.concatenate([x * keep_f, x * keep_b], axis=0)

    for h in heads:
        if has_s0:
            st_scr[h] = jnp.concatenate([jnp.transpose(s0_ref[0, 0, 0, h]), jnp.transpose(s0_ref[0, 0, 1, h])], axis=1)
        else:
            st_scr[h] = jnp.zeros((DV_A, LANES), F32)

    steps = GLA_STEPS if n % GLA_STEPS == 0 else 1

    def body(j, carry):
        rf = [pl.multiple_of((j * steps + s) * c, c) for s in range(steps)]
        rb = [pl.multiple_of((n - 1 - j * steps - s) * c, c) for s in range(steps)]
        sl = [slice(h * LANES, (h + 1) * LANES) for h in heads]
        for g in range(0, H_A, GLA_HEADS):
            process([(s, h) for s in range(steps) for h in heads[g:g + GLA_HEADS]], rf, rb, sl)
        return carry

    def process(units, rf, rb, sl):
        qq, kk, ll, split3 = [], [], [], []
        for s, h in units:
            qkf, qkb = qk_ref[pl.ds(rf[s], c), sl[h]], qk_ref[pl.ds(rb[s], c), sl[h]]
            qq.append(jnp.where(is_fwd, qkf, pltpu.roll(qkb, DK_A, axis=1)))
            kk.append(jnp.where(is_fwd, pltpu.roll(qkf, DK_A, axis=1), qkb))
            ll.append(jnp.where(is_fwd, la_ref[pl.ds(rf[s], c), sl[h]], la_ref[pl.ds(rb[s], c), sl[h]]))
            hi = ll[-1].astype(BF16)
            r1 = ll[-1] - hi.astype(F32)
            mid = r1.astype(BF16)
            lo = (r1 - mid.astype(F32)).astype(BF16)
            split3.append(jnp.concatenate([hi, mid, lo], axis=0))
        e_all = _dot(m3_ref[...], jnp.concatenate(split3, axis=1))
        e = [e_all[:, k * LANES:(k + 1) * LANES] for k in range(len(units))]
        cx = [jnp.where(is_fwd, e[k][0:c], e[k][0:c] - ll[k]) for k in range(len(units))]
        a = [_dot_nt(qq[k].astype(BF16), split(kk[k].astype(BF16))) * mask_ref[levels] for k in range(len(units))]
        for lev in range(levels):
            for k in range(len(units)):
                ex = jnp.exp2(sg_ref[lev] * (cx[k] - e[k][(lev + 1) * c:(lev + 2) * c]))
                a[k] = a[k] + _dot_nt((qq[k] * ex).astype(BF16), split((kk[k] * ex).astype(BF16))) * mask_ref[lev]
        for k, (s, h) in enumerate(units):
            tot = e[k][c - 1:c]
            u = jnp.exp2(cx[k])
            w = jnp.exp2(tot - cx[k])
            qd = (qq[k] * jnp.where(is_fwd, u, w)).astype(BF16)
            kd = (kk[k] * jnp.where(is_fwd, w, u)).astype(BF16)
            vs = jnp.concatenate([v_ref[pl.ds(rf[s], c), sl[h]], v_ref[pl.ds(rb[s], c), sl[h]]], axis=0)
            st = st_scr[h]
            o = _dot(split(a[k].astype(BF16)), vs) + _dot_nt(split(qd), st.astype(BF16))
            of_ref[pl.ds(rf[s], c), sl[h]] = o[:c]
            ob_ref[pl.ds(rb[s], c), sl[h]] = o[c:]
            vt = jnp.transpose(vs.astype(F32)).astype(BF16)
            st_scr[h] = st * jnp.exp2(tot) + _dot(vt, split(kd))

    lax.fori_loop(0, n // steps, body, 0, unroll=GLA_UNROLL)
    for h in heads:
        sl = slice(h * LANES, (h + 1) * LANES)
        o = _rms(of_ref[:, sl] + ob_ref[:, sl], gain_ref[...]) * _silu(ga_ref[:, sl])
        o_ref[:, sl] = o.astype(BF16)
        if slot is not None:
            st = st_scr[h]
            st_ref[0, slot.at(), 0, h] = jnp.transpose(st[:, :DK_A])
            st_ref[0, slot.at(), 1, h] = jnp.transpose(st[:, DK_A:])
    if slot is not None:
        slot.zero_others(st_ref)


def _gla_call(qk, la, v, ga, gain, consts, s0, s0_slot, row0, nb, t, slot, prev):
    m3, sg, mask, levels = consts
    blk0 = row0 // t
    width = H_A * LANES
    tok = pl.BlockSpec((t, width), lambda b: (blk0 + b, 0))
    in_specs = [tok, tok, tok, tok, _resident(gain.shape), _resident(m3.shape), _resident(sg.shape), _resident(mask.shape)]
    args = [qk, la, v, ga, gain, m3, sg, mask]
    tail = (2, H_A, DK_A, DV_A)
    if s0 is not None:
        in_specs.append(pl.BlockSpec((1, 1) + tail, lambda b: (b, s0_slot, 0, 0, 0, 0)))
        args.append(s0)
    out_specs = [pl.BlockSpec((t, width), lambda b: (b, 0))]
    out_shape = [jax.ShapeDtypeStruct((nb * t, width), BF16)]
    aliases = {}
    if slot is not None:
        out_specs.append(slot.spec(tail, lambda b: b))
        out_shape.append(slot.shape(nb, tail))
        if not slot.first:
            aliases = {len(args): 1}
            in_specs.append(pl.BlockSpec(memory_space=pl.ANY))
            args.append(prev)
    return pl.pallas_call(
        functools.partial(_gla_kernel, t=t, has_s0=s0 is not None, slot=slot, levels=levels),
        grid=(nb,),
        in_specs=in_specs,
        out_specs=out_specs,
        out_shape=out_shape,
        scratch_shapes=[pltpu.VMEM((t, width), F32), pltpu.VMEM((t, width), F32), pltpu.VMEM((H_A, DV_A, LANES), F32)],
        input_output_aliases=aliases,
        name="gla",
        compiler_params=_cparams(("arbitrary",)),
    )(*args)


def _kv_expand_kernel(c_ref, r_ref, w_ref, o_ref, ro_ref):
    o_ref[...] = _dot(c_ref[0, 0].astype(BF16), w_ref[...]).astype(BF16)
    kr = r_ref[0, 0]
    ro_ref[...] = jnp.concatenate([kr, jnp.zeros((kr.shape[0], LANES - ROPE_B), F32)], axis=1).astype(BF16)


def _kv_expand_call(cache_ckv, cache_kr, slot, wukv):
    nb, _, past, r = cache_ckv.shape
    return pl.pallas_call(
        _kv_expand_kernel,
        grid=(nb,),
        in_specs=[pl.BlockSpec((1, 1, past, r), lambda b: (b, slot, 0, 0)),
                  pl.BlockSpec((1, 1, past, ROPE_B), lambda b: (b, slot, 0, 0)), _resident(wukv.shape)],
        out_specs=[pl.BlockSpec((past, wukv.shape[1]), lambda b: (b, 0)), pl.BlockSpec((past, LANES), lambda b: (b, 0))],
        out_shape=[jax.ShapeDtypeStruct((nb * past, wukv.shape[1]), BF16), jax.ShapeDtypeStruct((nb * past, LANES), BF16)],
        name="kv_expand",
        compiler_params=_cparams(("arbitrary",)),
    )(cache_ckv, cache_kr, wukv)


def _attn_group(nb, per, row0, t, seg_blocks):
    g = ATTN_GROUP
    if per == 1 and nb % g == 0 and (row0 // t) % g == 0 and all(b0 % g == 0 for b0 in seg_blocks):
        return g
    return 1


def _mla_attn_kernel(*refs, nseg, group):
    q_ref = refs[0]
    seg = [(refs[1 + 2 * s], refs[2 + 2 * s]) for s in range(nseg)]
    o_ref = refs[1 + 2 * nseg]
    tq = q_ref.shape[0] // group
    units = [(g, hh) for g in range(group) for hh in range(H_B)]

    def keys(ref, g):
        s = ref.shape[0] // group
        return slice(g * s, (g + 1) * s)

    def qk(u):
        g, hh = units[u]
        q = q_ref[g * tq:(g + 1) * tq, hh * 2 * LANES:(hh + 1) * 2 * LANES]
        scores = []
        for kv_ref, kr_ref in seg:
            kcat = jnp.concatenate([kv_ref[keys(kv_ref, g), hh * 2 * LANES:hh * 2 * LANES + LANES],
                                    kr_ref[keys(kr_ref, g), :]], axis=1)
            scores.append(_dot_nt(q, kcat) * (MLA_SCALE * LOG2E))
        return scores

    nxt = qk(0)
    for u, (g, hh) in enumerate(units):
        scores = nxt
        if u + 1 < len(units):
            nxt = qk(u + 1)
        mx = functools.reduce(jnp.maximum, [jnp.max(s, axis=-1, keepdims=True) for s in scores])
        ps = [jnp.exp2(s - mx) for s in scores]
        den = functools.reduce(lambda a, b: a + b, [jnp.sum(p, axis=-1, keepdims=True) for p in ps])
        acc = functools.reduce(lambda a, b: a + b, [
            _dot(p.astype(BF16), kv_ref[keys(kv_ref, g), hh * 2 * LANES + LANES:(hh + 1) * 2 * LANES])
            for p, (kv_ref, _) in zip(ps, seg)])
        o_ref[g * tq:(g + 1) * tq, hh * LANES:(hh + 1) * LANES] = (acc * (1.0 / den)).astype(BF16)


def _mla_attn_call(qb, segs, row0, nb, t):
    tq = min(t, MLA_TQ)
    assert t % tq == 0 and row0 % tq == 0
    per = t // tq
    group = _attn_group(nb, per, row0, t, [r0 // s for _, _, r0, s in segs])
    tq, nb = tq * group, nb // group
    q0 = row0 // tq
    in_specs = [pl.BlockSpec((tq, qb.shape[1]), lambda b, j: (q0 + b * per + j, 0))]
    args = [qb]
    for kv, kr, r0, s in segs:
        b0 = r0 // (s * group)
        in_specs.append(pl.BlockSpec((s * group, kv.shape[1]), lambda b, j, b0=b0: (b0 + b, 0)))
        in_specs.append(pl.BlockSpec((s * group, LANES), lambda b, j, b0=b0: (b0 + b, 0)))
        args += [kv, kr]
    return pl.pallas_call(
        functools.partial(_mla_attn_kernel, nseg=len(segs), group=group),
        grid=(nb, per),
        in_specs=in_specs,
        out_specs=pl.BlockSpec((tq, H_B * V_B), lambda b, j: (b * per + j, 0)),
        out_shape=jax.ShapeDtypeStruct((nb * per * tq, H_B * V_B), BF16),
        name="mla_attn",
        compiler_params=_cparams(("arbitrary", "arbitrary")),
    )(*args)


def _odd_proj_kernel(*refs, prompt_tiles, slot):
    x_ref, mod_ref, gain_ref, w_ref, tab_ref = refs[:5]
    q_ref, kb_ref, vb_ref, kf_ref, vf_ref = refs[len(refs) - 5:]
    mod = mod_ref[0, 0]
    n = q_ref.shape[1]
    heads = n // LANES
    seq = kf_ref.shape[2] // heads
    parts = [slice(b * seq, (b + 1) * seq) for b in range(slot.per_block)]
    x = x_ref[...]
    ys = [_dot(_normmod(x[r], gain_ref[0, 0:1], mod[0:1], mod[1:2]).astype(BF16), w_ref[...]) for r in parts]
    for r, y in zip(parts, ys):
        cos, sin_dn, sin_up = tab_ref[0, r, :], tab_ref[1, r, :], tab_ref[2, r, :]
        vb_ref[r, :] = y[:, 2 * n:3 * n].astype(BF16)
        for g in range(heads):
            lo = g * LANES
            q_ref[r, lo:lo + LANES] = _rope(y[:, lo:lo + LANES], cos, sin_dn, sin_up).astype(BF16)
            kb_ref[r, lo:lo + LANES] = _rope(y[:, n + lo:n + lo + LANES], cos, sin_dn, sin_up).astype(BF16)

    @pl.when(pl.program_id(0) < prompt_tiles)
    def _():
        for b in range(slot.per_block):
            for g in range(heads):
                lo = g * LANES
                kf_ref[b, slot.at(), pl.ds(g, seq, stride=heads), :] = ys[b][:, n + lo:n + lo + LANES]
                vf_ref[b, slot.at(), pl.ds(g, seq, stride=heads), :] = ys[b][:, 2 * n + lo:2 * n + lo + LANES]
        slot.zero_others(kf_ref)
        slot.zero_others(vf_ref)


def _odd_proj_call(lay, x, mods, l, gains, w, tab, slot, prev):
    d = x.shape[1]
    n = w.shape[1] // 3
    row = lambda m: pl.BlockSpec((TM, m), lambda i: (i, 0))
    in_specs = [row(d), _mod_spec(lay, l, d), _layer_spec(gains.shape, l), _resident(w.shape), _tab_spec(lay)]
    args = [x, mods, gains, w, tab]
    aliases = {}
    if not slot.first:
        aliases = {len(args): 3, len(args) + 1: 4}
        in_specs += [pl.BlockSpec(memory_space=pl.ANY), pl.BlockSpec(memory_space=pl.ANY)]
        args += list(prev)
    tail = (lay.seq_p * H_C, 2 * DH_C)
    nb_p = lay.np // lay.seq_p
    return pl.pallas_call(
        functools.partial(_odd_proj_kernel, prompt_tiles=lay.pt, slot=slot),
        grid=(lay.tiles,),
        in_specs=in_specs,
        out_specs=[row(n), row(n), row(n), slot.spec(tail, lay.p_block), slot.spec(tail, lay.p_block)],
        out_shape=[jax.ShapeDtypeStruct((lay.nt, n), BF16)] * 3 + [slot.shape(nb_p, tail)] * 2,
        input_output_aliases=aliases,
        name="odd_proj",
        compiler_params=_cparams(("arbitrary",)),
    )(*args)


def _diff_attn_kernel(*refs, nseg, has_ctx, lam_init):
    q_ref, dl_ref, gain_ref = refs[0], refs[1], refs[2]
    seg = [(refs[3 + 2 * s], refs[4 + 2 * s]) for s in range(nseg)]
    o_ref = refs[3 + 2 * nseg]
    if has_ctx:
        (kc_ref, vc_ref), (kc_scr, vc_scr) = seg[0], refs[4 + 2 * nseg:6 + 2 * nseg]

        @pl.when(pl.program_id(1) == 0)
        def _():
            for hh in range(H_C):
                rows = pl.ds(hh, kc_scr.shape[0], stride=H_C)
                kc_scr[:, hh * LANES:(hh + 1) * LANES] = kc_ref[0, 0, rows, :].astype(BF16)
                vc_scr[:, hh * LANES:(hh + 1) * LANES] = vc_ref[0, 0, rows, :].astype(BF16)

        seg[0] = (kc_scr, vc_scr)
    dl = dl_ref[0]
    lam = (jnp.exp(jnp.sum(dl[0:1] * dl[1:2], axis=-1, keepdims=True))
           - jnp.exp(jnp.sum(dl[2:3] * dl[3:4], axis=-1, keepdims=True)) + lam_init)
    tq = q_ref.shape[0]
    lane = lax.broadcasted_iota(jnp.int32, (1, LANES), 1)
    keep1 = jnp.where(lane < DH_C, 1.0, 0.0).astype(BF16)
    keep2 = jnp.where(lane < DH_C, 0.0, 1.0).astype(BF16)
    def qk(hh):
        q = q_ref[:, hh * LANES:(hh + 1) * LANES]
        qq = jnp.concatenate([q * keep1, q * keep2], axis=0)
        return [_dot_nt(qq, k_ref[:, hh * LANES:(hh + 1) * LANES]) for k_ref, _ in seg]

    nxt = qk(0)
    for hh in range(H_C):
        scores = nxt
        if hh + 1 < H_C:
            nxt = qk(hh + 1)
        mx = functools.reduce(jnp.maximum, [jnp.max(s, axis=-1, keepdims=True) for s in scores])
        ps = [jnp.exp2(s - mx) for s in scores]
        den = functools.reduce(lambda a, b: a + b, [jnp.sum(p, axis=-1, keepdims=True) for p in ps])
        acc = functools.reduce(lambda a, b: a + b, [
            _dot(p.astype(BF16), v_ref[:, hh * LANES:(hh + 1) * LANES]) for p, (_, v_ref) in zip(ps, seg)])
        acc = acc * (1.0 / den)
        o = acc[:tq] - lam * acc[tq:]
        o_ref[:, hh * LANES:(hh + 1) * LANES] = (_rms(o, gain_ref[0]) * (1.0 - lam_init)).astype(BF16)


def _diff_attn_call(q, dls, gains, slot, ctx, segs, row0, nb, t, lam_init):
    tq = min(t, TQ)
    assert t % tq == 0 and row0 % tq == 0
    q0 = row0 // tq
    per = t // tq
    n = q.shape[1]
    in_specs = [pl.BlockSpec((tq, n), lambda b, j: (q0 + b * per + j, 0)),
                pl.BlockSpec((1,) + dls.shape[1:], lambda b, j: (slot, 0, 0)),
                pl.BlockSpec((1, 1, gains.shape[2]), lambda b, j: (slot, 0, 0))]
    args = [q, dls, gains]
    scratch = []
    if ctx is not None:
        nb_c, n_l, past, heads, dh = ctx[0].shape
        for cache in ctx:
            in_specs.append(pl.BlockSpec((1, 1, past * heads, dh), lambda b, j: (b, slot, 0, 0)))
            args.append(cache.reshape(nb_c, n_l, past * heads, dh))
        scratch = [pltpu.VMEM((past, n), BF16), pltpu.VMEM((past, n), BF16)]
    for k, v, r0, s in segs:
        b0 = r0 // s
        in_specs.append(pl.BlockSpec((s, n), lambda b, j, b0=b0: (b0 + b, 0)))
        in_specs.append(pl.BlockSpec((s, n), lambda b, j, b0=b0: (b0 + b, 0)))
        args += [k, v]
    return pl.pallas_call(
        functools.partial(_diff_attn_kernel, nseg=len(segs) + (ctx is not None), has_ctx=ctx is not None,
                          lam_init=lam_init),
        grid=(nb, per),
        in_specs=in_specs,
        out_specs=pl.BlockSpec((tq, n), lambda b, j: (b * per + j, 0)),
        out_shape=jax.ShapeDtypeStruct((nb * t, n), BF16),
        scratch_shapes=scratch,
        name="diff_attn",
        compiler_params=_cparams(("arbitrary", "arbitrary")),
    )(*args)


def _halo_specs(rows, width, tm, tile):
    per = tm // HALO
    last = rows // HALO - 1
    return [pl.BlockSpec((HALO, width), lambda i: (jnp.maximum(tile(i) * per - 1, 0), 0)),
            pl.BlockSpec((tm, width), lambda i: (tile(i), 0)),
            pl.BlockSpec((HALO, width), lambda i: (jnp.minimum((tile(i) + 1) * per, last), 0))]


def _ffn_kernel(*refs, lay, d_ff, final, nx, widths):
    pos = 3 * nx + 6 * len(widths)
    trips = [refs[3 * t:3 * t + 3] for t in range(pos // 3)]
    mod_ref, gain_ref, wo_ref, wup_ref, cw_ref, cb_ref, wdn_ref, fg_ref = refs[pos:pos + 8]
    outs = refs[pos + 8:len(refs) - 2]
    u_ref, act_ref = refs[len(refs) - 2:]
    tm = lay.tm
    i = pl.program_id(0)
    row = i * tm
    is_prompt = i < lay.pt
    seq = jnp.where(is_prompt, lay.seq_p, lay.seq_s)
    first = (row & (seq - 1)) == 0
    last = ((row + tm) & (seq - 1)) == 0

    def ext(trip_p, trip_s):
        if trip_s is None:
            return jnp.concatenate([r[...] for r in trip_p], axis=0)
        return jnp.concatenate([jnp.where(is_prompt, a[...], b[...]) for a, b in zip(trip_p, trip_s)], axis=0)

    mod = mod_ref[0, 0]
    x_ext = ext(trips[0], trips[1] if nx == 2 else None)
    proj = None
    k0 = 0
    for p, w in enumerate(widths):
        part = _dot(ext(trips[nx + 2 * p], trips[nx + 2 * p + 1]), wo_ref[0, k0:k0 + w, :])
        k0 += w
        proj = part if proj is None else proj + part
    xm = x_ext + mod[2:3] * proj
    gain, shift, scale = gain_ref[0, 1:2], mod[3:4], mod[4:5]
    hh = _normmod(xm, gain, shift, scale)
    h = jnp.concatenate([jnp.where(first, 0.0, hh[HALO - SUBLANES:HALO]), hh[HALO:HALO + tm],
                         jnp.where(last, 0.0, hh[HALO + tm:HALO + tm + SUBLANES])], axis=0).astype(BF16)
    x = xm[HALO:HALO + tm]
    acc = jnp.zeros((tm, x.shape[1]), F32)
    n_f = d_ff // FFN_TF
    n_slots = u_ref.shape[0]
    sub = lax.broadcasted_iota(jnp.int32, (SUBLANES, FFN_TF), 0)
    cut_prev = is_prompt & (sub == 0)
    cut_next = is_prompt & (sub == SUBLANES - 1)
    seams = range(lay.seq_p, tm, lay.seq_p)

    def up(f):
        for half, base in enumerate((0, d_ff)):
            lo = base + f * FFN_TF
            u_ref[f % n_slots, :, half * FFN_TF:(half + 1) * FFN_TF] = _dot(h, wup_ref[0, :, lo:lo + FFN_TF])

    def conv(f, half):
        lo = half * d_ff + f * FFN_TF
        cols = slice(half * FFN_TF, (half + 1) * FFN_TF)
        cw = cw_ref[0, :, lo:lo + FFN_TF]
        prev = u_ref[f % n_slots, SUBLANES - 1:SUBLANES - 1 + tm, cols]
        here = u_ref[f % n_slots, SUBLANES:SUBLANES + tm, cols]
        nxt = u_ref[f % n_slots, SUBLANES + 1:SUBLANES + 1 + tm, cols]
        for r in seams:
            prev = jnp.concatenate([prev[:r], jnp.where(cut_prev, 0.0, prev[r:r + SUBLANES]), prev[r + SUBLANES:]], axis=0)
            nxt = jnp.concatenate([nxt[:r - SUBLANES], jnp.where(cut_next, 0.0, nxt[r - SUBLANES:r]), nxt[r:]], axis=0)
        return prev * cw[0:1] + here * cw[1:2] + nxt * cw[2:3] + cb_ref[0, :, lo:lo + FFN_TF]

    for f in range(min(FFN_AHEAD, n_f)):
        up(f)
    for f in range(n_f):
        act_ref[:, f * FFN_TF:(f + 1) * FFN_TF] = (_silu(conv(f, 0)) * conv(f, 1)).astype(BF16)
        if f + FFN_AHEAD < n_f:
            up(f + FFN_AHEAD)
        if (f + 1) % FFN_DOWN == 0 or f + 1 == n_f:
            k0 = (f // FFN_DOWN) * FFN_DOWN * FFN_TF
            acc = acc + _dot(act_ref[:, k0:(f + 1) * FFN_TF], wdn_ref[0, k0:(f + 1) * FFN_TF, :])
    y = x + mod[5:6] * acc
    if not final:
        outs[0][...] = y
    else:
        yn = _rms(y, fg_ref[...])

        @pl.when(i < lay.pt)
        def _():
            outs[0][...] = yn

        @pl.when(i >= lay.pt)
        def _():
            outs[1][...] = yn


def _ffn_call(lay, x, pairs, mods, l, gains, wo, slot, wup, cw, cb, wdn, fgain, final):
    d = wo.shape[2]
    d_ff = wdn.shape[1]
    assert d_ff % FFN_TF == 0
    tm = lay.tm
    in_specs, args = [], []
    if isinstance(x, tuple):
        streams = [(x[0], lay.p_block), (x[1], lay.s_block)]
    else:
        streams = [(x, lambda i: i)]
    for a_p, a_s in pairs:
        streams += [(a_p, lay.p_block), (a_s, lay.s_block)]
    for arr, tile in streams:
        in_specs += _halo_specs(arr.shape[0], arr.shape[1], tm, tile)
        args += [arr, arr, arr]
    in_specs += [_mod_spec(lay, l, d), _layer_spec(gains.shape, l), _layer_spec(wo.shape, slot), _layer_spec(wup.shape, l),
                 _layer_spec(cw.shape, l), _layer_spec(cb.shape, l), _layer_spec(wdn.shape, l), _resident((1, d))]
    args += [mods, gains, wo, wup, cw, cb, wdn, fgain]
    if final:
        out_specs = [pl.BlockSpec((tm, d), lambda i: (lay.p_block(i), 0)), pl.BlockSpec((tm, d), lambda i: (lay.s_block(i), 0))]
        out_shape = [jax.ShapeDtypeStruct((lay.np, d), F32), jax.ShapeDtypeStruct((lay.ns, d), F32)]
    else:
        out_specs = [pl.BlockSpec((tm, d), lambda i: (i, 0))]
        out_shape = [jax.ShapeDtypeStruct((lay.nt, d), F32)]
    return pl.pallas_call(
        functools.partial(_ffn_kernel, lay=lay, d_ff=d_ff, final=final, nx=2 if isinstance(x, tuple) else 1,
                          widths=[a_p.shape[1] for a_p, _ in pairs]),
        name="ffn",
        grid=(lay.tiles,),
        in_specs=in_specs,
        out_specs=out_specs,
        out_shape=out_shape,
        scratch_shapes=[pltpu.VMEM((FFN_AHEAD + 1, tm + 2 * SUBLANES, 2 * FFN_TF), F32), pltpu.VMEM((tm, d_ff), BF16)],
        compiler_params=_cparams(("arbitrary",)),
    )(*args)


def _rope_tables(seq_s, both_halves):
    t = jnp.arange(seq_s)
    rowp = (t // GRID_W).astype(F32)
    colp = (t % GRID_W).astype(F32)
    half = ROPE_B // 2
    inv = ROPE_BASE ** (-jnp.arange(0, half, 2, dtype=F32) / half)
    ar, ac = rowp[:, None] * inv, colp[:, None] * inv
    zero = jnp.zeros_like(ar)
    cos = jnp.concatenate([jnp.cos(ar), jnp.cos(ar), jnp.cos(ac), jnp.cos(ac)], -1)
    sin_dn = jnp.concatenate([zero, jnp.sin(ar), zero, jnp.sin(ac)], -1)
    sin_up = jnp.concatenate([-jnp.sin(ar), zero, -jnp.sin(ac), zero], -1)
    if both_halves:
        parts = [jnp.concatenate([p, p], -1) for p in (cos, sin_dn, sin_up)]
    else:
        one, zz = jnp.ones_like(cos), jnp.zeros_like(cos)
        parts = [jnp.concatenate([cos, one], -1), jnp.concatenate([sin_dn, zz], -1), jnp.concatenate([sin_up, zz], -1)]
    ident = [jnp.ones((seq_s, LANES), F32), jnp.zeros((seq_s, LANES), F32), jnp.zeros((seq_s, LANES), F32)]
    return jnp.stack([jnp.concatenate([i_, p], 0) for i_, p in zip(ident, parts)], 0)


def _even_weights(w_in, w_dec, b_dec, w_uq):
    d = w_in.shape[0]
    sizes = (H_A * DK_A, H_A * DK_A, H_A * DV_A, H_A * DV_A, GLA_LR, GLA_LR, Q_RANK, KV_RANK, ROPE_B)
    offs = np.concatenate([[0], np.cumsum(sizes)])
    qa, ka, va, ga, lrf, lrb, cq, ckv, kr = [w_in[:, offs[j]:offs[j + 1]] for j in range(9)]
    qa = qa * (DK_A ** -0.5)
    qk = jnp.concatenate([qa.reshape(d, H_A, DK_A), ka.reshape(d, H_A, DK_A)], -1).reshape(d, 2 * H_A * DK_A)
    pad = jnp.zeros((d, LANES - ROPE_B - 2 * GLA_LR), w_in.dtype)
    w = jnp.concatenate([qk, va, ga, cq, ckv, kr, lrf, lrb, pad], -1).astype(BF16)
    wd = jnp.zeros((LANES, H_A, 2, DK_A), F32)
    wd = wd.at[ROPE_B:ROPE_B + GLA_LR, :, 0, :].set(w_dec[0].reshape(GLA_LR, H_A, DK_A))
    wd = wd.at[ROPE_B + GLA_LR:ROPE_B + 2 * GLA_LR, :, 1, :].set(w_dec[1].reshape(GLA_LR, H_A, DK_A))
    wd = wd.reshape(LANES, 2 * H_A * DK_A).astype(BF16)
    bd = jnp.stack([b_dec[0].reshape(H_A, DK_A), b_dec[1].reshape(H_A, DK_A)], 1).reshape(1, 2 * H_A * DK_A)
    wq = w_uq.reshape(Q_RANK, H_B, NOPE_B + ROPE_B)
    wq = jnp.concatenate([wq, jnp.zeros((Q_RANK, H_B, 2 * LANES - NOPE_B - ROPE_B), w_uq.dtype)], -1)
    wq = wq.reshape(Q_RANK, H_B * 2 * LANES).astype(BF16)
    return w, wd, bd, wq


def kernel(x_prompt, x_sample, state_gla, cache_mla_ckv, cache_mla_krope, cache_diff_k, cache_diff_v, c, c_ctx, w_ada, b_ada, norm_gain, final_gain, w_in_even, w_out_even, gla_w_decay, gla_b_decay, gla_norm, mla_q_norm, mla_kv_norm, mla_w_uq, mla_w_ukv, w_in_odd, w_out_odd, diff_lambda, diff_norm, ffn_w_up, ffn_conv_w, ffn_conv_b, ffn_w_down):
    nb_p, seq_p, d = x_prompt.shape
    nb_s, seq_s, _ = x_sample.shape
    past = cache_mla_ckv.shape[2]
    depth = w_ada.shape[0]
    n_even, n_odd = w_in_even.shape[0], w_in_odd.shape[0]
    n_p, n_s = nb_p * seq_p, nb_s * seq_s
    lay = _Layout(n_p, n_s, seq_p, seq_s, TM)
    lay_ffn = _Layout(n_p, n_s, seq_p, seq_s, FFN_TM)
    per_tile = TM // seq_p
    assert seq_s % GRID_W == 0 and 1 + nb_s <= ADA_ROWS
    assert n_p % seq_s == 0 and seq_p & (seq_p - 1) == 0 and seq_s & (seq_s - 1) == 0
    assert seq_p % GLA_C == 0 and seq_s % GLA_C == 0

    cond = jnp.concatenate([c_ctx[None, :], c, jnp.zeros((ADA_ROWS - 1 - nb_s, d), F32)], 0)
    mods = _ada_call(cond, w_ada, b_ada).reshape(depth, ADA_ROWS, 6, d)

    tab_mla = _rope_tables(seq_s, both_halves=False)
    tab_diff = _rope_tables(seq_s, both_halves=True)
    gla_consts = _gla_consts(GLA_C)
    wup_all = ffn_w_up.astype(BF16)
    wdn_all = ffn_w_down.astype(BF16)
    cb_all = ffn_conv_b.reshape(depth, 1, -1)
    wo_even = w_out_even.astype(BF16)
    wo_odd = w_out_odd.astype(BF16)
    fgain = final_gain.reshape(1, d)
    dgains = diff_norm.reshape(n_odd, 1, 2 * DH_C)

    x = (x_prompt.reshape(n_p, d), x_sample.reshape(n_s, d))
    st_new = caches_even = caches_odd = None
    for l in range(depth):
        i = l // 2
        if l % 2 == 0:
            slot = _Slot(caches_even, i, n_even, per_tile)
            w, wd, bd, wq = _even_weights(w_in_even[i], gla_w_decay[i], gla_b_decay[i], mla_w_uq[i])
            wukv = mla_w_ukv[i].astype(BF16)
            qk, la, v, ga, qb, kra, kv, ckv_new, kr_new = _even_proj_call(
                lay, x, mods, l, norm_gain, w, wd, bd, mla_q_norm[i].reshape(1, -1), mla_kv_norm[i].reshape(1, -1),
                wq, wukv, tab_mla, slot, caches_even)
            caches_even = (ckv_new, kr_new)
            ggain = gla_norm[i].reshape(1, DV_A)
            og_p, st_new = _gla_call(qk, la, v, ga, ggain, gla_consts, None, 0, 0, nb_p, seq_p,
                                     _Slot(st_new, i, n_even), st_new)
            og_s, = _gla_call(qk, la, v, ga, ggain, gla_consts, state_gla, i, n_p, nb_s, seq_s, None, None)
            kv_ctx, kr_ctx = _kv_expand_call(cache_mla_ckv, cache_mla_krope, i, wukv)
            om_p = _mla_attn_call(qb, [(kv, kra, 0, seq_p)], 0, nb_p, seq_p)
            om_s = _mla_attn_call(qb, [(kv_ctx, kr_ctx, 0, past), (kv, kra, n_p, seq_s)], n_p, nb_s, seq_s)
            pairs, wo = [(og_p, og_s), (om_p, om_s)], wo_even
        else:
            slot = _Slot(caches_odd, i, n_odd, per_tile)
            lam_init = 0.8 - 0.6 * math.exp(-0.3 * l)
            n = H_C * 2 * DH_C
            col_scale = np.concatenate([np.full(n, DIFF_SCALE * LOG2E, np.float32), np.ones(2 * n, np.float32)])
            wi = (w_in_odd[i] * col_scale).astype(BF16)
            q, kb, vb, kf_new, vf_new = _odd_proj_call(lay, x, mods, l, norm_gain, wi, tab_diff, slot, caches_odd)
            caches_odd = (kf_new, vf_new)
            od_p = _diff_attn_call(q, diff_lambda, dgains, i, None, [(kb, vb, 0, seq_p)], 0, nb_p, seq_p, lam_init)
            od_s = _diff_attn_call(q, diff_lambda, dgains, i, (cache_diff_k, cache_diff_v), [(kb, vb, n_p, seq_s)],
                                   n_p, nb_s, seq_s, lam_init)
            pairs, wo = [(od_p, od_s)], wo_odd
        x = _ffn_call(lay_ffn, x, pairs, mods, l, norm_gain, wo, i, wup_all, ffn_conv_w, cb_all, wdn_all, fgain,
                      final=(l == depth - 1))
        if l < depth - 1:
            x = x[0]
    y_prompt = x[0].reshape(nb_p, seq_p, d)
    y_sample = x[1].reshape(nb_s, seq_s, d)
    diff_shape = (nb_p, n_odd, seq_p, H_C, 2 * DH_C)
    return (y_prompt, y_sample, st_new, caches_even[0], caches_even[1],
            caches_odd[0].reshape(diff_shape), caches_odd[1].reshape(diff_shape))
```

```python
import functools
import math

import numpy as np
import jax
import jax.numpy as jnp
from jax import lax
from jax.experimental import pallas as pl
from jax.experimental.pallas import tpu as pltpu

GRID_W = 64
ROPE_BASE = 10000.0
EPS = 1e-6
H_A, DK_A, DV_A = 4, 64, 128
GLA_LR = 16
GLA_TAU = 16.0
H_B, Q_RANK, KV_RANK, NOPE_B, ROPE_B, V_B = 4, 256, 256, 128, 64, 128
MLA_SCALE = (NOPE_B + ROPE_B) ** -0.5
H_C, DH_C = 8, 64
DIFF_SCALE = DH_C ** -0.5
LOG2E = math.log2(math.e)

LANES = 128
SUBLANES = 8
HALO = 16
VMEM_LIMIT = 56 * 1024 * 1024

TM = 512
TQ = 256
MLA_TQ = 512
ATTN_GROUP = 4
GLA_C = 64
GLA_STEPS = 1
GLA_HEADS = 4
GLA_UNROLL = 2
ADA_TN = 1536
FFN_TM = 512
FFN_TF = 256
FFN_AHEAD = 2
FFN_DOWN = 4
ADA_ROWS = 16

F32 = jnp.float32
BF16 = jnp.bfloat16


def _dot(a, b):
    return jnp.dot(a, b, preferred_element_type=F32)


def _dot_nt(a, b):
    return lax.dot_general(a, b, (((1,), (1,)), ((), ())), preferred_element_type=F32)


def _silu(x):
    return x * (1.0 / (1.0 + jnp.exp(-x)))


def _rms(x, gain):
    return x * lax.rsqrt(jnp.mean(x * x, axis=-1, keepdims=True) + EPS) * gain


def _normmod(x, gain, shift, scale):
    return _rms(x, gain) * (1.0 + scale) + shift


def _rope(x, cos, sin_dn, sin_up):
    return (x * cos + pltpu.roll(x, 16, axis=1) * sin_dn + pltpu.roll(x, LANES - 16, axis=1) * sin_up)


def _cparams(sem, vmem=VMEM_LIMIT):
    return pltpu.CompilerParams(dimension_semantics=sem, vmem_limit_bytes=vmem)


def _resident(shape):
    nd = len(shape)
    return pl.BlockSpec(shape, lambda *_: (0,) * nd, pipeline_mode=pl.Buffered(1))


def _layer_spec(shape, l):
    nd = len(shape)
    return pl.BlockSpec((1,) + tuple(shape[1:]), lambda *_: (l,) + (0,) * (nd - 1), pipeline_mode=pl.Buffered(1))


def _ada_kernel(c_ref, w_ref, b_ref, o_ref):
    s = _silu(c_ref[...]).astype(BF16)
    o_ref[0] = _dot(s, w_ref[0].astype(BF16)) + b_ref[0]


def _ada_call(cond, w_ada, b_ada):
    depth, d, n = w_ada.shape
    return pl.pallas_call(
        _ada_kernel,
        grid=(depth, n // ADA_TN),
        in_specs=[
            pl.BlockSpec((ADA_ROWS, d), lambda l, j: (0, 0)),
            pl.BlockSpec((1, d, ADA_TN), lambda l, j: (l, 0, j)),
            pl.BlockSpec((1, 1, ADA_TN), lambda l, j: (l, 0, j)),
        ],
        out_specs=pl.BlockSpec((1, ADA_ROWS, ADA_TN), lambda l, j: (l, 0, j)),
        out_shape=jax.ShapeDtypeStruct((depth, ADA_ROWS, n), F32),
        name="ada",
        compiler_params=_cparams(("arbitrary", "arbitrary")),
    )(cond, w_ada, b_ada.reshape(depth, 1, n))


class _Layout:
    def __init__(self, n_prompt, n_sample, seq_p, seq_s, tm):
        assert tm % seq_p == 0 and n_prompt % tm == 0 and n_sample % tm == 0 and seq_s % tm == 0
        self.tm = tm
        self.np, self.ns, self.nt = n_prompt, n_sample, n_prompt + n_sample
        self.seq_p, self.seq_s = seq_p, seq_s
        self.pt = n_prompt // tm
        self.st = n_sample // tm
        self.tiles = self.pt + self.st
        self.tiles_per_seq_s = seq_s // tm

    def group(self, i):
        return jnp.where(i < self.pt, 0, 1 + (i - self.pt) // self.tiles_per_seq_s)

    def pos_block(self, i):
        return jnp.where(i < self.pt, 0, self.tiles_per_seq_s + (i - self.pt) % self.tiles_per_seq_s)

    def p_block(self, i):
        return jnp.minimum(i, self.pt - 1)

    def s_block(self, i):
        return jnp.maximum(i - self.pt, 0)


def _mod_spec(lay, l, d):
    return pl.BlockSpec((1, 1, 6, d), lambda i: (l, lay.group(i), 0, 0))


def _tab_spec(lay):
    return pl.BlockSpec((3, lay.tm, LANES), lambda i: (0, lay.pos_block(i), 0))


def _x_specs(lay, x):
    if isinstance(x, tuple):
        d = x[0].shape[1]
        return [pl.BlockSpec((lay.tm, d), lambda i: (lay.p_block(i), 0)),
                pl.BlockSpec((lay.tm, d), lambda i: (lay.s_block(i), 0))], list(x)
    return [pl.BlockSpec((lay.tm, x.shape[1]), lambda i: (i, 0))], [x]


def _read_x(x_refs, prompt_tiles):
    if len(x_refs) == 1:
        return x_refs[0][...]
    return jnp.where(pl.program_id(0) < prompt_tiles, x_refs[0][...], x_refs[1][...])


class _Slot:
    def __init__(self, prev, slot, n_slots, per_block=1):
        self.prev, self.slot, self.n_slots, self.per_block = prev, slot, n_slots, per_block

    @property
    def first(self):
        return self.prev is None

    def spec(self, tail, index):
        zeros = (0,) * len(tail)
        if self.first:
            return pl.BlockSpec((self.per_block, self.n_slots) + tail, lambda *g: (index(*g), 0) + zeros)
        s = self.slot
        return pl.BlockSpec((self.per_block, 1) + tail, lambda *g: (index(*g), s) + zeros)

    def shape(self, nb, tail):
        return jax.ShapeDtypeStruct((nb, self.n_slots) + tail, F32)

    def at(self):
        return self.slot if self.first else 0

    def zero_others(self, ref):
        if self.first:
            for b in range(self.per_block):
                for s in range(self.n_slots):
                    if s != self.slot:
                        ref[b, s] = jnp.zeros(ref.shape[2:], F32)


def _even_proj_kernel(*refs, nx, prompt_tiles, slot):
    x_refs = refs[:nx]
    (mod_ref, gain_ref, w_ref, wd_ref, bd_ref, qn_ref, kvn_ref, wuq_ref, wukv_ref, tab_ref) = refs[nx:nx + 10]
    outs = refs[len(refs) - 9:]
    qk_ref, la_ref, v_ref, ga_ref, qb_ref, kra_ref, kv_ref, ckv_ref, kr_ref = outs
    mod = mod_ref[0, 0]
    seq = ckv_ref.shape[2]
    parts = [slice(b * seq, (b + 1) * seq) for b in range(slot.per_block)]
    x = _read_x(x_refs, prompt_tiles)
    ys = [_dot(_normmod(x[r], gain_ref[0, 0:1], mod[0:1], mod[1:2]).astype(BF16), w_ref[...]) for r in parts]
    ckvns, gs = [], []
    for r, y in zip(parts, ys):
        qk_ref[r, :] = y[:, 0:512]
        v_ref[r, :] = y[:, 512:1024].astype(BF16)
        ga_ref[r, :] = y[:, 1024:1536]
        cq, ckv, g = y[:, 1536:1792], y[:, 1792:2048], y[:, 2048:2176]
        pre = _dot(g.astype(BF16), wd_ref[...]) + bd_ref[...]
        la_ref[r, :] = (jnp.minimum(pre, 0.0) - jnp.log(1.0 + jnp.exp(-jnp.abs(pre)))) * (LOG2E / GLA_TAU)
        cos, sin_dn, sin_up = tab_ref[0, r, :], tab_ref[1, r, :], tab_ref[2, r, :]
        qb = _dot(_rms(cq, qn_ref[...]).astype(BF16), wuq_ref[...])
        for hh in range(H_B):
            lo = hh * 2 * LANES
            qb_ref[r, lo:lo + LANES] = qb[:, lo:lo + LANES].astype(BF16)
            qb_ref[r, lo + LANES:lo + 2 * LANES] = _rope(qb[:, lo + LANES:lo + 2 * LANES], cos, sin_dn, sin_up).astype(BF16)
        ckvn = _rms(ckv, kvn_ref[...])
        kv_ref[r, :] = _dot(ckvn.astype(BF16), wukv_ref[...]).astype(BF16)
        lane = lax.broadcasted_iota(jnp.int32, g.shape, 1)
        kra_ref[r, :] = jnp.where(lane < ROPE_B, _rope(g, cos, sin_dn, sin_up), 0.0).astype(BF16)
        ckvns.append(ckvn)
        gs.append(g)

    @pl.when(pl.program_id(0) < prompt_tiles)
    def _():
        for b in range(slot.per_block):
            ckv_ref[b, slot.at()] = ckvns[b]
            kr_ref[b, slot.at()] = gs[b][:, :ROPE_B]
        slot.zero_others(ckv_ref)
        slot.zero_others(kr_ref)


def _even_proj_call(lay, x, mods, l, gains, w, wd, bd, qn, kvn, wuq, wukv, tab, slot, prev):
    d = w.shape[0]
    x_specs, x_args = _x_specs(lay, x)
    row = lambda n: pl.BlockSpec((TM, n), lambda i: (i, 0))
    outs = [(512, F32), (512, F32), (512, BF16), (512, F32), (1024, BF16), (LANES, BF16), (1024, BF16)]
    in_specs = x_specs + [_mod_spec(lay, l, d), _layer_spec(gains.shape, l), _resident(w.shape), _resident(wd.shape),
                          _resident(bd.shape), _resident(qn.shape), _resident(kvn.shape), _resident(wuq.shape),
                          _resident(wukv.shape), _tab_spec(lay)]
    args = x_args + [mods, gains, w, wd, bd, qn, kvn, wuq, wukv, tab]
    aliases = {}
    if not slot.first:
        aliases = {len(args): len(outs), len(args) + 1: len(outs) + 1}
        in_specs += [pl.BlockSpec(memory_space=pl.ANY), pl.BlockSpec(memory_space=pl.ANY)]
        args += list(prev)
    nb_p = lay.np // lay.seq_p
    return pl.pallas_call(
        functools.partial(_even_proj_kernel, nx=len(x_args), prompt_tiles=lay.pt, slot=slot),
        grid=(lay.tiles,),
        in_specs=in_specs,
        out_specs=[row(n) for n, _ in outs]
        + [slot.spec((lay.seq_p, KV_RANK), lay.p_block), slot.spec((lay.seq_p, ROPE_B), lay.p_block)],
        out_shape=[jax.ShapeDtypeStruct((lay.nt, n), dt) for n, dt in outs]
        + [slot.shape(nb_p, (lay.seq_p, KV_RANK)), slot.shape(nb_p, (lay.seq_p, ROPE_B))],
        input_output_aliases=aliases,
        name="even_proj",
        compiler_params=_cparams(("arbitrary",)),
    )(*args)


def _gla_consts(c):
    levels = int(math.log2(c))
    idx = np.arange(c)
    cum = idx[None, :] <= idx[:, None]
    rows, sgn, masks = [cum], [], []
    for lev in range(levels):
        half = 1 << lev
        node = idx // (2 * half)
        mid = node * 2 * half + half
        upper = idx >= mid
        rows.append(cum[mid - 1])
        sgn.append(np.repeat(np.where(upper, 1.0, -1.0)[:, None], LANES, 1))
        same = node[:, None] == node[None, :]
        masks.append(np.concatenate([same & upper[:, None] & ~upper[None, :],
                                     same & ~upper[:, None] & upper[None, :]], 1))
    eye = np.eye(c, dtype=bool)
    masks.append(np.concatenate([eye, eye], 1))
    m = np.concatenate(rows, 0)
    return (jnp.asarray(np.concatenate([m, m, m], 1), BF16), jnp.asarray(np.stack(sgn, 0), F32),
            jnp.asarray(np.stack(masks, 0), F32), levels)


def _gla_kernel(*refs, t, has_s0, slot, levels):
    qk_ref, la_ref, v_ref, ga_ref, gain_ref, m3_ref, sg_ref, mask_ref = refs[:8]
    s0_ref = refs[8] if has_s0 else None
    n_out = 1 if slot is None else 2
    o_ref = refs[len(refs) - 3 - n_out]
    st_ref = None if slot is None else refs[len(refs) - 4]
    of_ref, ob_ref, st_scr = refs[len(refs) - 3:]
    c = GLA_C
    n = t // c
    heads = range(H_A)
    lane = lax.broadcasted_iota(jnp.int32, (1, LANES), 1)
    is_fwd = lane < DK_A
    keep_f = jnp.where(is_fwd, 1.0, 0.0).astype(BF16)
    keep_b = jnp.where(is_fwd, 0.0, 1.0).astype(BF16)

    def split(x):
        return jnp.concatenate([x * keep_f, x * keep_b], axis=0)

    for h in heads:
        if has_s0:
            st_scr[h] = jnp.concatenate([jnp.transpose(s0_ref[0, 0, 0, h]), jnp.transpose(s0_ref[0, 0, 1, h])], axis=1)
        else:
            st_scr[h] = jnp.zeros((DV_A, LANES), F32)

    steps = GLA_STEPS if n % GLA_STEPS == 0 else 1

    def body(j, carry):
        rf = [pl.multiple_of((j * steps + s) * c, c) for s in range(steps)]
        rb = [pl.multiple_of((n - 1 - j * steps - s) * c, c) for s in range(steps)]
        sl = [slice(h * LANES, (h + 1) * LANES) for h in heads]
        for g in range(0, H_A, GLA_HEADS):
            process([(s, h) for s in range(steps) for h in heads[g:g + GLA_HEADS]], rf, rb, sl)
        return carry

    def process(units, rf, rb, sl):
        qq, kk, ll, split3 = [], [], [], []
        for s, h in units:
            qkf, qkb = qk_ref[pl.ds(rf[s], c), sl[h]], qk_ref[pl.ds(rb[s], c), sl[h]]
            qq.append(jnp.where(is_fwd, qkf, pltpu.roll(qkb, DK_A, axis=1)))
            kk.append(jnp.where(is_fwd, pltpu.roll(qkf, DK_A, axis=1), qkb))
            ll.append(jnp.where(is_fwd, la_ref[pl.ds(rf[s], c), sl[h]], la_ref[pl.ds(rb[s], c), sl[h]]))
            hi = ll[-1].astype(BF16)
            r1 = ll[-1] - hi.astype(F32)
            mid = r1.astype(BF16)
            lo = (r1 - mid.astype(F32)).astype(BF16)
            split3.append(jnp.concatenate([hi, mid, lo], axis=0))
        e_all = _dot(m3_ref[...], jnp.concatenate(split3, axis=1))
        e = [e_all[:, k * LANES:(k + 1) * LANES] for k in range(len(units))]
        cx = [jnp.where(is_fwd, e[k][0:c], e[k][0:c] - ll[k]) for k in range(len(units))]
        a = [_dot_nt(qq[k].astype(BF16), split(kk[k].astype(BF16))) * mask_ref[levels] for k in range(len(units))]
        for lev in range(levels):
            for k in range(len(units)):
                ex = jnp.exp2(sg_ref[lev] * (cx[k] - e[k][(lev + 1) * c:(lev + 2) * c]))
                a[k] = a[k] + _dot_nt((qq[k] * ex).astype(BF16), split((kk[k] * ex).astype(BF16))) * mask_ref[lev]
        for k, (s, h) in enumerate(units):
            tot = e[k][c - 1:c]
            u = jnp.exp2(cx[k])
            w = jnp.exp2(tot - cx[k])
            qd = (qq[k] * jnp.where(is_fwd, u, w)).astype(BF16)
            kd = (kk[k] * jnp.where(is_fwd, w, u)).astype(BF16)
            vs = jnp.concatenate([v_ref[pl.ds(rf[s], c), sl[h]], v_ref[pl.ds(rb[s], c), sl[h]]], axis=0)
            st = st_scr[h]
            o = _dot(split(a[k].astype(BF16)), vs) + _dot_nt(split(qd), st.astype(BF16))
            of_ref[pl.ds(rf[s], c), sl[h]] = o[:c]
            ob_ref[pl.ds(rb[s], c), sl[h]] = o[c:]
            vt = jnp.transpose(vs.astype(F32)).astype(BF16)
            st_scr[h] = st * jnp.exp2(tot) + _dot(vt, split(kd))

    lax.fori_loop(0, n // steps, body, 0, unroll=GLA_UNROLL)
    for h in heads:
        sl = slice(h * LANES, (h + 1) * LANES)
        o = _rms(of_ref[:, sl] + ob_ref[:, sl], gain_ref[...]) * _silu(ga_ref[:, sl])
        o_ref[:, sl] = o.astype(BF16)
        if slot is not None:
            st = st_scr[h]
            st_ref[0, slot.at(), 0, h] = jnp.transpose(st[:, :DK_A])
            st_ref[0, slot.at(), 1, h] = jnp.transpose(st[:, DK_A:])
    if slot is not None:
        slot.zero_others(st_ref)


def _gla_call(qk, la, v, ga, gain, consts, s0, s0_slot, row0, nb, t, slot, prev):
    m3, sg, mask, levels = consts
    blk0 = row0 // t
    width = H_A * LANES
    tok = pl.BlockSpec((t, width), lambda b: (blk0 + b, 0))
    in_specs = [tok, tok, tok, tok, _resident(gain.shape), _resident(m3.shape), _resident(sg.shape), _resident(mask.shape)]
    args = [qk, la, v, ga, gain, m3, sg, mask]
    tail = (2, H_A, DK_A, DV_A)
    if s0 is not None:
        in_specs.append(pl.BlockSpec((1, 1) + tail, lambda b: (b, s0_slot, 0, 0, 0, 0)))
        args.append(s0)
    out_specs = [pl.BlockSpec((t, width), lambda b: (b, 0))]
    out_shape = [jax.ShapeDtypeStruct((nb * t, width), BF16)]
    aliases = {}
    if slot is not None:
        out_specs.append(slot.spec(tail, lambda b: b))
        out_shape.append(slot.shape(nb, tail))
        if not slot.first:
            aliases = {len(args): 1}
            in_specs.append(pl.BlockSpec(memory_space=pl.ANY))
            args.append(prev)
    return pl.pallas_call(
        functools.partial(_gla_kernel, t=t, has_s0=s0 is not None, slot=slot, levels=levels),
        grid=(nb,),
        in_specs=in_specs,
        out_specs=out_specs,
        out_shape=out_shape,
        scratch_shapes=[pltpu.VMEM((t, width), F32), pltpu.VMEM((t, width), F32), pltpu.VMEM((H_A, DV_A, LANES), F32)],
        input_output_aliases=aliases,
        name="gla",
        compiler_params=_cparams(("arbitrary",)),
    )(*args)


def _kv_expand_kernel(c_ref, r_ref, w_ref, o_ref, ro_ref):
    o_ref[...] = _dot(c_ref[0, 0].astype(BF16), w_ref[...]).astype(BF16)
    kr = r_ref[0, 0]
    ro_ref[...] = jnp.concatenate([kr, jnp.zeros((kr.shape[0], LANES - ROPE_B), F32)], axis=1).astype(BF16)


def _kv_expand_call(cache_ckv, cache_kr, slot, wukv):
    nb, _, past, r = cache_ckv.shape
    return pl.pallas_call(
        _kv_expand_kernel,
        grid=(nb,),
        in_specs=[pl.BlockSpec((1, 1, past, r), lambda b: (b, slot, 0, 0)),
                  pl.BlockSpec((1, 1, past, ROPE_B), lambda b: (b, slot, 0, 0)), _resident(wukv.shape)],
        out_specs=[pl.BlockSpec((past, wukv.shape[1]), lambda b: (b, 0)), pl.BlockSpec((past, LANES), lambda b: (b, 0))],
        out_shape=[jax.ShapeDtypeStruct((nb * past, wukv.shape[1]), BF16), jax.ShapeDtypeStruct((nb * past, LANES), BF16)],
        name="kv_expand",
        compiler_params=_cparams(("arbitrary",)),
    )(cache_ckv, cache_kr, wukv)


def _attn_group(nb, per, row0, t, seg_blocks):
    g = ATTN_GROUP
    if per == 1 and nb % g == 0 and (row0 // t) % g == 0 and all(b0 % g == 0 for b0 in seg_blocks):
        return g
    return 1


def _mla_attn_kernel(*refs, nseg, group):
    q_ref = refs[0]
    seg = [(refs[1 + 2 * s], refs[2 + 2 * s]) for s in range(nseg)]
    o_ref = refs[1 + 2 * nseg]
    tq = q_ref.shape[0] // group
    units = [(g, hh) for g in range(group) for hh in range(H_B)]

    def keys(ref, g):
        s = ref.shape[0] // group
        return slice(g * s, (g + 1) * s)

    def qk(u):
        g, hh = units[u]
        q = q_ref[g * tq:(g + 1) * tq, hh * 2 * LANES:(hh + 1) * 2 * LANES]
        scores = []
        for kv_ref, kr_ref in seg:
            kcat = jnp.concatenate([kv_ref[keys(kv_ref, g), hh * 2 * LANES:hh * 2 * LANES + LANES],
                                    kr_ref[keys(kr_ref, g), :]], axis=1)
            scores.append(_dot_nt(q, kcat) * (MLA_SCALE * LOG2E))
        return scores

    nxt = qk(0)
    for u, (g, hh) in enumerate(units):
        scores = nxt
        if u + 1 < len(units):
            nxt = qk(u + 1)
        mx = functools.reduce(jnp.maximum, [jnp.max(s, axis=-1, keepdims=True) for s in scores])
        ps = [jnp.exp2(s - mx) for s in scores]
        den = functools.reduce(lambda a, b: a + b, [jnp.sum(p, axis=-1, keepdims=True) for p in ps])
        acc = functools.reduce(lambda a, b: a + b, [
            _dot(p.astype(BF16), kv_ref[keys(kv_ref, g), hh * 2 * LANES + LANES:(hh + 1) * 2 * LANES])
            for p, (kv_ref, _) in zip(ps, seg)])
        o_ref[g * tq:(g + 1) * tq, hh * LANES:(hh + 1) * LANES] = (acc * (1.0 / den)).astype(BF16)


def _mla_attn_call(qb, segs, row0, nb, t):
    tq = min(t, MLA_TQ)
    assert t % tq == 0 and row0 % tq == 0
    per = t // tq
    group = _attn_group(nb, per, row0, t, [r0 // s for _, _, r0, s in segs])
    tq, nb = tq * group, nb // group
    q0 = row0 // tq
    in_specs = [pl.BlockSpec((tq, qb.shape[1]), lambda b, j: (q0 + b * per + j, 0))]
    args = [qb]
    for kv, kr, r0, s in segs:
        b0 = r0 // (s * group)
        in_specs.append(pl.BlockSpec((s * group, kv.shape[1]), lambda b, j, b0=b0: (b0 + b, 0)))
        in_specs.append(pl.BlockSpec((s * group, LANES), lambda b, j, b0=b0: (b0 + b, 0)))
        args += [kv, kr]
    return pl.pallas_call(
        functools.partial(_mla_attn_kernel, nseg=len(segs), group=group),
        grid=(nb, per),
        in_specs=in_specs,
        out_specs=pl.BlockSpec((tq, H_B * V_B), lambda b, j: (b * per + j, 0)),
        out_shape=jax.ShapeDtypeStruct((nb * per * tq, H_B * V_B), BF16),
        name="mla_attn",
        compiler_params=_cparams(("arbitrary", "arbitrary")),
    )(*args)


def _odd_proj_kernel(*refs, prompt_tiles, slot):
    x_ref, mod_ref, gain_ref, w_ref, tab_ref = refs[:5]
    q_ref, kb_ref, vb_ref, kf_ref, vf_ref = refs[len(refs) - 5:]
    mod = mod_ref[0, 0]
    n = q_ref.shape[1]
    heads = n // LANES
    seq = kf_ref.shape[2] // heads
    parts = [slice(b * seq, (b + 1) * seq) for b in range(slot.per_block)]
    x = x_ref[...]
    ys = [_dot(_normmod(x[r], gain_ref[0, 0:1], mod[0:1], mod[1:2]).astype(BF16), w_ref[...]) for r in parts]
    for r, y in zip(parts, ys):
        cos, sin_dn, sin_up = tab_ref[0, r, :], tab_ref[1, r, :], tab_ref[2, r, :]
        vb_ref[r, :] = y[:, 2 * n:3 * n].astype(BF16)
        for g in range(heads):
            lo = g * LANES
            q_ref[r, lo:lo + LANES] = _rope(y[:, lo:lo + LANES], cos, sin_dn, sin_up).astype(BF16)
            kb_ref[r, lo:lo + LANES] = _rope(y[:, n + lo:n + lo + LANES], cos, sin_dn, sin_up).astype(BF16)

    @pl.when(pl.program_id(0) < prompt_tiles)
    def _():
        for b in range(slot.per_block):
            for g in range(heads):
                lo = g * LANES
                kf_ref[b, slot.at(), pl.ds(g, seq, stride=heads), :] = ys[b][:, n + lo:n + lo + LANES]
                vf_ref[b, slot.at(), pl.ds(g, seq, stride=heads), :] = ys[b][:, 2 * n + lo:2 * n + lo + LANES]
        slot.zero_others(kf_ref)
        slot.zero_others(vf_ref)


def _odd_proj_call(lay, x, mods, l, gains, w, tab, slot, prev):
    d = x.shape[1]
    n = w.shape[1] // 3
    row = lambda m: pl.BlockSpec((TM, m), lambda i: (i, 0))
    in_specs = [row(d), _mod_spec(lay, l, d), _layer_spec(gains.shape, l), _resident(w.shape), _tab_spec(lay)]
    args = [x, mods, gains, w, tab]
    aliases = {}
    if not slot.first:
        aliases = {len(args): 3, len(args) + 1: 4}
        in_specs += [pl.BlockSpec(memory_space=pl.ANY), pl.BlockSpec(memory_space=pl.ANY)]
        args += list(prev)
    tail = (lay.seq_p * H_C, 2 * DH_C)
    nb_p = lay.np // lay.seq_p
    return pl.pallas_call(
        functools.partial(_odd_proj_kernel, prompt_tiles=lay.pt, slot=slot),
        grid=(lay.tiles,),
        in_specs=in_specs,
        out_specs=[row(n), row(n), row(n), slot.spec(tail, lay.p_block), slot.spec(tail, lay.p_block)],
        out_shape=[jax.ShapeDtypeStruct((lay.nt, n), BF16)] * 3 + [slot.shape(nb_p, tail)] * 2,
        input_output_aliases=aliases,
        name="odd_proj",
        compiler_params=_cparams(("arbitrary",)),
    )(*args)


def _diff_attn_kernel(*refs, nseg, has_ctx, lam_init):
    q_ref, dl_ref, gain_ref = refs[0], refs[1], refs[2]
    seg = [(refs[3 + 2 * s], refs[4 + 2 * s]) for s in range(nseg)]
    o_ref = refs[3 + 2 * nseg]
    if has_ctx:
        (kc_ref, vc_ref), (kc_scr, vc_scr) = seg[0], refs[4 + 2 * nseg:6 + 2 * nseg]

        @pl.when(pl.program_id(1) == 0)
        def _():
            for hh in range(H_C):
                rows = pl.ds(hh, kc_scr.shape[0], stride=H_C)
                kc_scr[:, hh * LANES:(hh + 1) * LANES] = kc_ref[0, 0, rows, :].astype(BF16)
                vc_scr[:, hh * LANES:(hh + 1) * LANES] = vc_ref[0, 0, rows, :].astype(BF16)

        seg[0] = (kc_scr, vc_scr)
    dl = dl_ref[0]
    lam = (jnp.exp(jnp.sum(dl[0:1] * dl[1:2], axis=-1, keepdims=True))
           - jnp.exp(jnp.sum(dl[2:3] * dl[3:4], axis=-1, keepdims=True)) + lam_init)
    tq = q_ref.shape[0]
    lane = lax.broadcasted_iota(jnp.int32, (1, LANES), 1)
    keep1 = jnp.where(lane < DH_C, 1.0, 0.0).astype(BF16)
    keep2 = jnp.where(lane < DH_C, 0.0, 1.0).astype(BF16)
    def qk(hh):
        q = q_ref[:, hh * LANES:(hh + 1) * LANES]
        qq = jnp.concatenate([q * keep1, q * keep2], axis=0)
        return [_dot_nt(qq, k_ref[:, hh * LANES:(hh + 1) * LANES]) for k_ref, _ in seg]

    nxt = qk(0)
    for hh in range(H_C):
        scores = nxt
        if hh + 1 < H_C:
            nxt = qk(hh + 1)
        mx = functools.reduce(jnp.maximum, [jnp.max(s, axis=-1, keepdims=True) for s in scores])
        ps = [jnp.exp2(s - mx) for s in scores]
        den = functools.reduce(lambda a, b: a + b, [jnp.sum(p, axis=-1, keepdims=True) for p in ps])
        acc = functools.reduce(lambda a, b: a + b, [
            _dot(p.astype(BF16), v_ref[:, hh * LANES:(hh + 1) * LANES]) for p, (_, v_ref) in zip(ps, seg)])
        acc = acc * (1.0 / den)
        o = acc[:tq] - lam * acc[tq:]
        o_ref[:, hh * LANES:(hh + 1) * LANES] = (_rms(o, gain_ref[0]) * (1.0 - lam_init)).astype(BF16)


def _diff_attn_call(q, dls, gains, slot, ctx, segs, row0, nb, t, lam_init):
    tq = min(t, TQ)
    assert t % tq == 0 and row0 % tq == 0
    q0 = row0 // tq
    per = t // tq
    n = q.shape[1]
    in_specs = [pl.BlockSpec((tq, n), lambda b, j: (q0 + b * per + j, 0)),
                pl.BlockSpec((1,) + dls.shape[1:], lambda b, j: (slot, 0, 0)),
                pl.BlockSpec((1, 1, gains.shape[2]), lambda b, j: (slot, 0, 0))]
    args = [q, dls, gains]
    scratch = []
    if ctx is not None:
        nb_c, n_l, past, heads, dh = ctx[0].shape
        for cache in ctx:
            in_specs.append(pl.BlockSpec((1, 1, past * heads, dh), lambda b, j: (b, slot, 0, 0)))
            args.append(cache.reshape(nb_c, n_l, past * heads, dh))
        scratch = [pltpu.VMEM((past, n), BF16), pltpu.VMEM((past, n), BF16)]
    for k, v, r0, s in segs:
        b0 = r0 // s
        in_specs.append(pl.BlockSpec((s, n), lambda b, j, b0=b0: (b0 + b, 0)))
        in_specs.append(pl.BlockSpec((s, n), lambda b, j, b0=b0: (b0 + b, 0)))
        args += [k, v]
    return pl.pallas_call(
        functools.partial(_diff_attn_kernel, nseg=len(segs) + (ctx is not None), has_ctx=ctx is not None,
                          lam_init=lam_init),
        grid=(nb, per),
        in_specs=in_specs,
        out_specs=pl.BlockSpec((tq, n), lambda b, j: (b * per + j, 0)),
        out_shape=jax.ShapeDtypeStruct((nb * t, n), BF16),
        scratch_shapes=scratch,
        name="diff_attn",
        compiler_params=_cparams(("arbitrary", "arbitrary")),
    )(*args)


def _halo_specs(rows, width, tm, tile):
    per = tm // HALO
    last = rows // HALO - 1
    return [pl.BlockSpec((HALO, width), lambda i: (jnp.maximum(tile(i) * per - 1, 0), 0)),
            pl.BlockSpec((tm, width), lambda i: (tile(i), 0)),
            pl.BlockSpec((HALO, width), lambda i: (jnp.minimum((tile(i) + 1) * per, last), 0))]


def _ffn_kernel(*refs, lay, d_ff, final, nx, widths):
    pos = 3 * nx + 6 * len(widths)
    trips = [refs[3 * t:3 * t + 3] for t in range(pos // 3)]
    mod_ref, gain_ref, wo_ref, wup_ref, cw_ref, cb_ref, wdn_ref, fg_ref = refs[pos:pos + 8]
    outs = refs[pos + 8:len(refs) - 2]
    u_ref, act_ref = refs[len(refs) - 2:]
    tm = lay.tm
    i = pl.program_id(0)
    row = i * tm
    is_prompt = i < lay.pt
    seq = jnp.where(is_prompt, lay.seq_p, lay.seq_s)
    first = (row & (seq - 1)) == 0
    last = ((row + tm) & (seq - 1)) == 0

    def ext(trip_p, trip_s):
        if trip_s is None:
            return jnp.concatenate([r[...] for r in trip_p], axis=0)
        return jnp.concatenate([jnp.where(is_prompt, a[...], b[...]) for a, b in zip(trip_p, trip_s)], axis=0)

    mod = mod_ref[0, 0]
    x_ext = ext(trips[0], trips[1] if nx == 2 else None)
    proj = None
    k0 = 0
    for p, w in enumerate(widths):
        part = _dot(ext(trips[nx + 2 * p], trips[nx + 2 * p + 1]), wo_ref[0, k0:k0 + w, :])
        k0 += w
        proj = part if proj is None else proj + part
    xm = x_ext + mod[2:3] * proj
    gain, shift, scale = gain_ref[0, 1:2], mod[3:4], mod[4:5]
    hh = _normmod(xm, gain, shift, scale)
    h = jnp.concatenate([jnp.where(first, 0.0, hh[HALO - SUBLANES:HALO]), hh[HALO:HALO + tm],
                         jnp.where(last, 0.0, hh[HALO + tm:HALO + tm + SUBLANES])], axis=0).astype(BF16)
    x = xm[HALO:HALO + tm]
    acc = jnp.zeros((tm, x.shape[1]), F32)
    n_f = d_ff // FFN_TF
    n_slots = u_ref.shape[0]
    sub = lax.broadcasted_iota(jnp.int32, (SUBLANES, FFN_TF), 0)
    cut_prev = is_prompt & (sub == 0)
    cut_next = is_prompt & (sub == SUBLANES - 1)
    seams = range(lay.seq_p, tm, lay.seq_p)

    def up(f):
        for half, base in enumerate((0, d_ff)):
            lo = base + f * FFN_TF
            u_ref[f % n_slots, :, half * FFN_TF:(half + 1) * FFN_TF] = _dot(h, wup_ref[0, :, lo:lo + FFN_TF])

    def conv(f, half):
        lo = half * d_ff + f * FFN_TF
        cols = slice(half * FFN_TF, (half + 1) * FFN_TF)
        cw = cw_ref[0, :, lo:lo + FFN_TF]
        prev = u_ref[f % n_slots, SUBLANES - 1:SUBLANES - 1 + tm, cols]
        here = u_ref[f % n_slots, SUBLANES:SUBLANES + tm, cols]
        nxt = u_ref[f % n_slots, SUBLANES + 1:SUBLANES + 1 + tm, cols]
        for r in seams:
            prev = jnp.concatenate([prev[:r], jnp.where(cut_prev, 0.0, prev[r:r + SUBLANES]), prev[r + SUBLANES:]], axis=0)
            nxt = jnp.concatenate([nxt[:r - SUBLANES], jnp.where(cut_next, 0.0, nxt[r - SUBLANES:r]), nxt[r:]], axis=0)
        return prev * cw[0:1] + here * cw[1:2] + nxt * cw[2:3] + cb_ref[0, :, lo:lo + FFN_TF]

    for f in range(min(FFN_AHEAD, n_f)):
        up(f)
    for f in range(n_f):
        act_ref[:, f * FFN_TF:(f + 1) * FFN_TF] = (_silu(conv(f, 0)) * conv(f, 1)).astype(BF16)
        if f + FFN_AHEAD < n_f:
            up(f + FFN_AHEAD)
        if (f + 1) % FFN_DOWN == 0 or f + 1 == n_f:
            k0 = (f // FFN_DOWN) * FFN_DOWN * FFN_TF
            acc = acc + _dot(act_ref[:, k0:(f + 1) * FFN_TF], wdn_ref[0, k0:(f + 1) * FFN_TF, :])
    y = x + mod[5:6] * acc
    if not final:
        outs[0][...] = y
    else:
        yn = _rms(y, fg_ref[...])

        @pl.when(i < lay.pt)
        def _():
            outs[0][...] = yn

        @pl.when(i >= lay.pt)
        def _():
            outs[1][...] = yn


def _ffn_call(lay, x, pairs, mods, l, gains, wo, slot, wup, cw, cb, wdn, fgain, final):
    d = wo.shape[2]
    d_ff = wdn.shape[1]
    assert d_ff % FFN_TF == 0
    tm = lay.tm
    in_specs, args = [], []
    if isinstance(x, tuple):
        streams = [(x[0], lay.p_block), (x[1], lay.s_block)]
    else:
        streams = [(x, lambda i: i)]
    for a_p, a_s in pairs:
        streams += [(a_p, lay.p_block), (a_s, lay.s_block)]
    for arr, tile in streams:
        in_specs += _halo_specs(arr.shape[0], arr.shape[1], tm, tile)
        args += [arr, arr, arr]
    in_specs += [_mod_spec(lay, l, d), _layer_spec(gains.shape, l), _layer_spec(wo.shape, slot), _layer_spec(wup.shape, l),
                 _layer_spec(cw.shape, l), _layer_spec(cb.shape, l), _layer_spec(wdn.shape, l), _resident((1, d))]
    args += [mods, gains, wo, wup, cw, cb, wdn, fgain]
    if final:
        out_specs = [pl.BlockSpec((tm, d), lambda i: (lay.p_block(i), 0)), pl.BlockSpec((tm, d), lambda i: (lay.s_block(i), 0))]
        out_shape = [jax.ShapeDtypeStruct((lay.np, d), F32), jax.ShapeDtypeStruct((lay.ns, d), F32)]
    else:
        out_specs = [pl.BlockSpec((tm, d), lambda i: (i, 0))]
        out_shape = [jax.ShapeDtypeStruct((lay.nt, d), F32)]
    return pl.pallas_call(
        functools.partial(_ffn_kernel, lay=lay, d_ff=d_ff, final=final, nx=2 if isinstance(x, tuple) else 1,
                          widths=[a_p.shape[1] for a_p, _ in pairs]),
        name="ffn",
        grid=(lay.tiles,),
        in_specs=in_specs,
        out_specs=out_specs,
        out_shape=out_shape,
        scratch_shapes=[pltpu.VMEM((FFN_AHEAD + 1, tm + 2 * SUBLANES, 2 * FFN_TF), F32), pltpu.VMEM((tm, d_ff), BF16)],
        compiler_params=_cparams(("arbitrary",)),
    )(*args)


def _rope_tables(seq_s, both_halves):
    t = jnp.arange(seq_s)
    rowp = (t // GRID_W).astype(F32)
    colp = (t % GRID_W).astype(F32)
    half = ROPE_B // 2
    inv = ROPE_BASE ** (-jnp.arange(0, half, 2, dtype=F32) / half)
    ar, ac = rowp[:, None] * inv, colp[:, None] * inv
    zero = jnp.zeros_like(ar)
    cos = jnp.concatenate([jnp.cos(ar), jnp.cos(ar), jnp.cos(ac), jnp.cos(ac)], -1)
    sin_dn = jnp.concatenate([zero, jnp.sin(ar), zero, jnp.sin(ac)], -1)
    sin_up = jnp.concatenate([-jnp.sin(ar), zero, -jnp.sin(ac), zero], -1)
    if both_halves:
        parts = [jnp.concatenate([p, p], -1) for p in (cos, sin_dn, sin_up)]
    else:
        one, zz = jnp.ones_like(cos), jnp.zeros_like(cos)
        parts = [jnp.concatenate([cos, one], -1), jnp.concatenate([sin_dn, zz], -1), jnp.concatenate([sin_up, zz], -1)]
    ident = [jnp.ones((seq_s, LANES), F32), jnp.zeros((seq_s, LANES), F32), jnp.zeros((seq_s, LANES), F32)]
    return jnp.stack([jnp.concatenate([i_, p], 0) for i_, p in zip(ident, parts)], 0)


def _even_weights(w_in, w_dec, b_dec, w_uq):
    d = w_in.shape[0]
    sizes = (H_A * DK_A, H_A * DK_A, H_A * DV_A, H_A * DV_A, GLA_LR, GLA_LR, Q_RANK, KV_RANK, ROPE_B)
    offs = np.concatenate([[0], np.cumsum(sizes)])
    qa, ka, va, ga, lrf, lrb, cq, ckv, kr = [w_in[:, offs[j]:offs[j + 1]] for j in range(9)]
    qa = qa * (DK_A ** -0.5)
    qk = jnp.concatenate([qa.reshape(d, H_A, DK_A), ka.reshape(d, H_A, DK_A)], -1).reshape(d, 2 * H_A * DK_A)
    pad = jnp.zeros((d, LANES - ROPE_B - 2 * GLA_LR), w_in.dtype)
    w = jnp.concatenate([qk, va, ga, cq, ckv, kr, lrf, lrb, pad], -1).astype(BF16)
    wd = jnp.zeros((LANES, H_A, 2, DK_A), F32)
    wd = wd.at[ROPE_B:ROPE_B + GLA_LR, :, 0, :].set(w_dec[0].reshape(GLA_LR, H_A, DK_A))
    wd = wd.at[ROPE_B + GLA_LR:ROPE_B + 2 * GLA_LR, :, 1, :].set(w_dec[1].reshape(GLA_LR, H_A, DK_A))
    wd = wd.reshape(LANES, 2 * H_A * DK_A).astype(BF16)
    bd = jnp.stack([b_dec[0].reshape(H_A, DK_A), b_dec[1].reshape(H_A, DK_A)], 1).reshape(1, 2 * H_A * DK_A)
    wq = w_uq.reshape(Q_RANK, H_B, NOPE_B + ROPE_B)
    wq = jnp.concatenate([wq, jnp.zeros((Q_RANK, H_B, 2 * LANES - NOPE_B - ROPE_B), w_uq.dtype)], -1)
    wq = wq.reshape(Q_RANK, H_B * 2 * LANES).astype(BF16)
    return w, wd, bd, wq


def kernel(x_prompt, x_sample, state_gla, cache_mla_ckv, cache_mla_krope, cache_diff_k, cache_diff_v, c, c_ctx, w_ada, b_ada, norm_gain, final_gain, w_in_even, w_out_even, gla_w_decay, gla_b_decay, gla_norm, mla_q_norm, mla_kv_norm, mla_w_uq, mla_w_ukv, w_in_odd, w_out_odd, diff_lambda, diff_norm, ffn_w_up, ffn_conv_w, ffn_conv_b, ffn_w_down):
    nb_p, seq_p, d = x_prompt.shape
    nb_s, seq_s, _ = x_sample.shape
    past = cache_mla_ckv.shape[2]
    depth = w_ada.shape[0]
    n_even, n_odd = w_in_even.shape[0], w_in_odd.shape[0]
    n_p, n_s = nb_p * seq_p, nb_s * seq_s
    lay = _Layout(n_p, n_s, seq_p, seq_s, TM)
    lay_ffn = _Layout(n_p, n_s, seq_p, seq_s, FFN_TM)
    per_tile = TM // seq_p
    assert seq_s % GRID_W == 0 and 1 + nb_s <= ADA_ROWS
    assert n_p % seq_s == 0 and seq_p & (seq_p - 1) == 0 and seq_s & (seq_s - 1) == 0
    assert seq_p % GLA_C == 0 and seq_s % GLA_C == 0

    cond = jnp.concatenate([c_ctx[None, :], c, jnp.zeros((ADA_ROWS - 1 - nb_s, d), F32)], 0)
    mods = _ada_call(cond, w_ada, b_ada).reshape(depth, ADA_ROWS, 6, d)

    tab_mla = _rope_tables(seq_s, both_halves=False)
    tab_diff = _rope_tables(seq_s, both_halves=True)
    gla_consts = _gla_consts(GLA_C)
    wup_all = ffn_w_up.astype(BF16)
    wdn_all = ffn_w_down.astype(BF16)
    cb_all = ffn_conv_b.reshape(depth, 1, -1)
    wo_even = w_out_even.astype(BF16)
    wo_odd = w_out_odd.astype(BF16)
    fgain = final_gain.reshape(1, d)
    dgains = diff_norm.reshape(n_odd, 1, 2 * DH_C)

    x = (x_prompt.reshape(n_p, d), x_sample.reshape(n_s, d))
    st_new = caches_even = caches_odd = None
    for l in range(depth):
        i = l // 2
        if l % 2 == 0:
            slot = _Slot(caches_even, i, n_even, per_tile)
            w, wd, bd, wq = _even_weights(w_in_even[i], gla_w_decay[i], gla_b_decay[i], mla_w_uq[i])
            wukv = mla_w_ukv[i].astype(BF16)
            qk, la, v, ga, qb, kra, kv, ckv_new, kr_new = _even_proj_call(
                lay, x, mods, l, norm_gain, w, wd, bd, mla_q_norm[i].reshape(1, -1), mla_kv_norm[i].reshape(1, -1),
                wq, wukv, tab_mla, slot, caches_even)
            caches_even = (ckv_new, kr_new)
            ggain = gla_norm[i].reshape(1, DV_A)
            og_p, st_new = _gla_call(qk, la, v, ga, ggain, gla_consts, None, 0, 0, nb_p, seq_p,
                                     _Slot(st_new, i, n_even), st_new)
            og_s, = _gla_call(qk, la, v, ga, ggain, gla_consts, state_gla, i, n_p, nb_s, seq_s, None, None)
            kv_ctx, kr_ctx = _kv_expand_call(cache_mla_ckv, cache_mla_krope, i, wukv)
            om_p = _mla_attn_call(qb, [(kv, kra, 0, seq_p)], 0, nb_p, seq_p)
            om_s = _mla_attn_call(qb, [(kv_ctx, kr_ctx, 0, past), (kv, kra, n_p, seq_s)], n_p, nb_s, seq_s)
            pairs, wo = [(og_p, og_s), (om_p, om_s)], wo_even
        else:
            slot = _Slot(caches_odd, i, n_odd, per_tile)
            lam_init = 0.8 - 0.6 * math.exp(-0.3 * l)
            n = H_C * 2 * DH_C
            col_scale = np.concatenate([np.full(n, DIFF_SCALE * LOG2E, np.float32), np.ones(2 * n, np.float32)])
            wi = (w_in_odd[i] * col_scale).astype(BF16)
            q, kb, vb, kf_new, vf_new = _odd_proj_call(lay, x, mods, l, norm_gain, wi, tab_diff, slot, caches_odd)
            caches_odd = (kf_new, vf_new)
            od_p = _diff_attn_call(q, diff_lambda, dgains, i, None, [(kb, vb, 0, seq_p)], 0, nb_p, seq_p, lam_init)
            od_s = _diff_attn_call(q, diff_lambda, dgains, i, (cache_diff_k, cache_diff_v), [(kb, vb, n_p, seq_s)],
                                   n_p, nb_s, seq_s, lam_init)
            pairs, wo = [(od_p, od_s)], wo_odd
        x = _ffn_call(lay_ffn, x, pairs, mods, l, norm_gain, wo, i, wup_all, ffn_conv_w, cb_all, wdn_all, fgain,
                      final=(l == depth - 1))
        if l < depth - 1:
            x = x[0]
    y_prompt = x[0].reshape(nb_p, seq_p, d)
    y_sample = x[1].reshape(nb_s, seq_s, d)
    diff_shape = (nb_p, n_odd, seq_p, H_C, 2 * DH_C)
    return (y_prompt, y_sample, st_new, caches_even[0], caches_even[1],
            caches_odd[0].reshape(diff_shape), caches_odd[1].reshape(diff_shape))
```

```python
import functools
import math

import numpy as np
import jax
import jax.numpy as jnp
from jax import lax
from jax.experimental import pallas as pl
from jax.experimental.pallas import tpu as pltpu

GRID_W = 64
ROPE_BASE = 10000.0
EPS = 1e-6
H_A, DK_A, DV_A = 4, 64, 128
GLA_LR = 16
GLA_TAU = 16.0
H_B, Q_RANK, KV_RANK, NOPE_B, ROPE_B, V_B = 4, 256, 256, 128, 64, 128
MLA_SCALE = (NOPE_B + ROPE_B) ** -0.5
H_C, DH_C = 8, 64
DIFF_SCALE = DH_C ** -0.5
LOG2E = math.log2(math.e)

LANES = 128
SUBLANES = 8
HALO = 16
VMEM_LIMIT = 56 * 1024 * 1024

TM = 512
TQ = 256
MLA_TQ = 512
ATTN_GROUP = 4
GLA_C = 64
GLA_STEPS = 1
GLA_HEADS = 4
GLA_UNROLL = 4
ADA_TN = 1536
FFN_TM = 512
FFN_TF = 256
FFN_AHEAD = 2
FFN_DOWN = 4
ADA_ROWS = 16

F32 = jnp.float32
BF16 = jnp.bfloat16


def _dot(a, b):
    return jnp.dot(a, b, preferred_element_type=F32)


def _dot_nt(a, b):
    return lax.dot_general(a, b, (((1,), (1,)), ((), ())), preferred_element_type=F32)


def _silu(x):
    return x * (1.0 / (1.0 + jnp.exp(-x)))


def _rms(x, gain):
    return x * lax.rsqrt(jnp.mean(x * x, axis=-1, keepdims=True) + EPS) * gain


def _normmod(x, gain, shift, scale):
    return _rms(x, gain) * (1.0 + scale) + shift


def _rope(x, cos, sin_dn, sin_up):
    return (x * cos + pltpu.roll(x, 16, axis=1) * sin_dn + pltpu.roll(x, LANES - 16, axis=1) * sin_up)


def _cparams(sem, vmem=VMEM_LIMIT):
    return pltpu.CompilerParams(dimension_semantics=sem, vmem_limit_bytes=vmem)


def _resident(shape):
    nd = len(shape)
    return pl.BlockSpec(shape, lambda *_: (0,) * nd, pipeline_mode=pl.Buffered(1))


def _layer_spec(shape, l):
    nd = len(shape)
    return pl.BlockSpec((1,) + tuple(shape[1:]), lambda *_: (l,) + (0,) * (nd - 1), pipeline_mode=pl.Buffered(1))


def _ada_kernel(c_ref, w_ref, b_ref, o_ref):
    s = _silu(c_ref[...]).astype(BF16)
    o_ref[0] = _dot(s, w_ref[0].astype(BF16)) + b_ref[0]


def _ada_call(cond, w_ada, b_ada):
    depth, d, n = w_ada.shape
    return pl.pallas_call(
        _ada_kernel,
        grid=(depth, n // ADA_TN),
        in_specs=[
            pl.BlockSpec((ADA_ROWS, d), lambda l, j: (0, 0)),
            pl.BlockSpec((1, d, ADA_TN), lambda l, j: (l, 0, j)),
            pl.BlockSpec((1, 1, ADA_TN), lambda l, j: (l, 0, j)),
        ],
        out_specs=pl.BlockSpec((1, ADA_ROWS, ADA_TN), lambda l, j: (l, 0, j)),
        out_shape=jax.ShapeDtypeStruct((depth, ADA_ROWS, n), F32),
        name="ada",
        compiler_params=_cparams(("arbitrary", "arbitrary")),
    )(cond, w_ada, b_ada.reshape(depth, 1, n))


class _Layout:
    def __init__(self, n_prompt, n_sample, seq_p, seq_s, tm):
        assert tm % seq_p == 0 and n_prompt % tm == 0 and n_sample % tm == 0 and seq_s % tm == 0
        self.tm = tm
        self.np, self.ns, self.nt = n_prompt, n_sample, n_prompt + n_sample
        self.seq_p, self.seq_s = seq_p, seq_s
        self.pt = n_prompt // tm
        self.st = n_sample // tm
        self.tiles = self.pt + self.st
        self.tiles_per_seq_s = seq_s // tm

    def group(self, i):
        return jnp.where(i < self.pt, 0, 1 + (i - self.pt) // self.tiles_per_seq_s)

    def pos_block(self, i):
        return jnp.where(i < self.pt, 0, self.tiles_per_seq_s + (i - self.pt) % self.tiles_per_seq_s)

    def p_block(self, i):
        return jnp.minimum(i, self.pt - 1)

    def s_block(self, i):
        return jnp.maximum(i - self.pt, 0)


def _mod_spec(lay, l, d):
    return pl.BlockSpec((1, 1, 6, d), lambda i: (l, lay.group(i), 0, 0))


def _tab_spec(lay):
    return pl.BlockSpec((3, lay.tm, LANES), lambda i: (0, lay.pos_block(i), 0))


def _x_specs(lay, x):
    if isinstance(x, tuple):
        d = x[0].shape[1]
        return [pl.BlockSpec((lay.tm, d), lambda i: (lay.p_block(i), 0)),
                pl.BlockSpec((lay.tm, d), lambda i: (lay.s_block(i), 0))], list(x)
    return [pl.BlockSpec((lay.tm, x.shape[1]), lambda i: (i, 0))], [x]


def _read_x(x_refs, prompt_tiles):
    if len(x_refs) == 1:
        return x_refs[0][...]
    return jnp.where(pl.program_id(0) < prompt_tiles, x_refs[0][...], x_refs[1][...])


class _Slot:
    def __init__(self, prev, slot, n_slots, per_block=1):
        self.prev, self.slot, self.n_slots, self.per_block = prev, slot, n_slots, per_block

    @property
    def first(self):
        return self.prev is None

    def spec(self, tail, index):
        zeros = (0,) * len(tail)
        if self.first:
            return pl.BlockSpec((self.per_block, self.n_slots) + tail, lambda *g: (index(*g), 0) + zeros)
        s = self.slot
        return pl.BlockSpec((self.per_block, 1) + tail, lambda *g: (index(*g), s) + zeros)

    def shape(self, nb, tail):
        return jax.ShapeDtypeStruct((nb, self.n_slots) + tail, F32)

    def at(self):
        return self.slot if self.first else 0

    def zero_others(self, ref):
        if self.first:
            for b in range(self.per_block):
                for s in range(self.n_slots):
                    if s != self.slot:
                        ref[b, s] = jnp.zeros(ref.shape[2:], F32)


def _even_proj_kernel(*refs, nx, prompt_tiles, slot):
    x_refs = refs[:nx]
    (mod_ref, gain_ref, w_ref, wd_ref, bd_ref, qn_ref, kvn_ref, wuq_ref, wukv_ref, tab_ref) = refs[nx:nx + 10]
    outs = refs[len(refs) - 9:]
    qk_ref, la_ref, v_ref, ga_ref, qb_ref, kra_ref, kv_ref, ckv_ref, kr_ref = outs
    mod = mod_ref[0, 0]
    seq = ckv_ref.shape[2]
    parts = [slice(b * seq, (b + 1) * seq) for b in range(slot.per_block)]
    x = _read_x(x_refs, prompt_tiles)
    ys = [_dot(_normmod(x[r], gain_ref[0, 0:1], mod[0:1], mod[1:2]).astype(BF16), w_ref[...]) for r in parts]
    ckvns, gs = [], []
    for r, y in zip(parts, ys):
        qk_ref[r, :] = y[:, 0:512]
        v_ref[r, :] = y[:, 512:1024].astype(BF16)
        ga_ref[r, :] = y[:, 1024:1536]
        cq, ckv, g = y[:, 1536:1792], y[:, 1792:2048], y[:, 2048:2176]
        pre = _dot(g.astype(BF16), wd_ref[...]) + bd_ref[...]
        la_ref[r, :] = (jnp.minimum(pre, 0.0) - jnp.log(1.0 + jnp.exp(-jnp.abs(pre)))) * (LOG2E / GLA_TAU)
        cos, sin_dn, sin_up = tab_ref[0, r, :], tab_ref[1, r, :], tab_ref[2, r, :]
        qb = _dot(_rms(cq, qn_ref[...]).astype(BF16), wuq_ref[...])
        for hh in range(H_B):
            lo = hh * 2 * LANES
            qb_ref[r, lo:lo + LANES] = qb[:, lo:lo + LANES].astype(BF16)
            qb_ref[r, lo + LANES:lo + 2 * LANES] = _rope(qb[:, lo + LANES:lo + 2 * LANES], cos, sin_dn, sin_up).astype(BF16)
        ckvn = _rms(ckv, kvn_ref[...])
        kv_ref[r, :] = _dot(ckvn.astype(BF16), wukv_ref[...]).astype(BF16)
        lane = lax.broadcasted_iota(jnp.int32, g.shape, 1)
        kra_ref[r, :] = jnp.where(lane < ROPE_B, _rope(g, cos, sin_dn, sin_up), 0.0).astype(BF16)
        ckvns.append(ckvn)
        gs.append(g)

    @pl.when(pl.program_id(0) < prompt_tiles)
    def _():
        for b in range(slot.per_block):
            ckv_ref[b, slot.at()] = ckvns[b]
            kr_ref[b, slot.at()] = gs[b][:, :ROPE_B]
        slot.zero_others(ckv_ref)
        slot.zero_others(kr_ref)


def _even_proj_call(lay, x, mods, l, gains, w, wd, bd, qn, kvn, wuq, wukv, tab, slot, prev):
    d = w.shape[0]
    x_specs, x_args = _x_specs(lay, x)
    row = lambda n: pl.BlockSpec((TM, n), lambda i: (i, 0))
    outs = [(512, F32), (512, F32), (512, BF16), (512, F32), (1024, BF16), (LANES, BF16), (1024, BF16)]
    in_specs = x_specs + [_mod_spec(lay, l, d), _layer_spec(gains.shape, l), _resident(w.shape), _resident(wd.shape),
                          _resident(bd.shape), _resident(qn.shape), _resident(kvn.shape), _resident(wuq.shape),
                          _resident(wukv.shape), _tab_spec(lay)]
    args = x_args + [mods, gains, w, wd, bd, qn, kvn, wuq, wukv, tab]
    aliases = {}
    if not slot.first:
        aliases = {len(args): len(outs), len(args) + 1: len(outs) + 1}
        in_specs += [pl.BlockSpec(memory_space=pl.ANY), pl.BlockSpec(memory_space=pl.ANY)]
        args += list(prev)
    nb_p = lay.np // lay.seq_p
    return pl.pallas_call(
        functools.partial(_even_proj_kernel, nx=len(x_args), prompt_tiles=lay.pt, slot=slot),
        grid=(lay.tiles,),
        in_specs=in_specs,
        out_specs=[row(n) for n, _ in outs]
        + [slot.spec((lay.seq_p, KV_RANK), lay.p_block), slot.spec((lay.seq_p, ROPE_B), lay.p_block)],
        out_shape=[jax.ShapeDtypeStruct((lay.nt, n), dt) for n, dt in outs]
        + [slot.shape(nb_p, (lay.seq_p, KV_RANK)), slot.shape(nb_p, (lay.seq_p, ROPE_B))],
        input_output_aliases=aliases,
        name="even_proj",
        compiler_params=_cparams(("arbitrary",)),
    )(*args)


def _gla_consts(c):
    levels = int(math.log2(c))
    idx = np.arange(c)
    cum = idx[None, :] <= idx[:, None]
    rows, sgn, masks = [cum], [], []
    for lev in range(levels):
        half = 1 << lev
        node = idx // (2 * half)
        mid = node * 2 * half + half
        upper = idx >= mid
        rows.append(cum[mid - 1])
        sgn.append(np.repeat(np.where(upper, 1.0, -1.0)[:, None], LANES, 1))
        same = node[:, None] == node[None, :]
        masks.append(np.concatenate([same & upper[:, None] & ~upper[None, :],
                                     same & ~upper[:, None] & upper[None, :]], 1))
    eye = np.eye(c, dtype=bool)
    masks.append(np.concatenate([eye, eye], 1))
    m = np.concatenate(rows, 0)
    return (jnp.asarray(np.concatenate([m, m, m], 1), BF16), jnp.asarray(np.stack(sgn, 0), F32),
            jnp.asarray(np.stack(masks, 0), F32), levels)


def _gla_kernel(*refs, t, has_s0, slot, levels):
    qk_ref, la_ref, v_ref, ga_ref, gain_ref, m3_ref, sg_ref, mask_ref = refs[:8]
    s0_ref = refs[8] if has_s0 else None
    n_out = 1 if slot is None else 2
    o_ref = refs[len(refs) - 3 - n_out]
    st_ref = None if slot is None else refs[len(refs) - 4]
    of_ref, ob_ref, st_scr = refs[len(refs) - 3:]
    c = GLA_C
    n = t // c
    heads = range(H_A)
    lane = lax.broadcasted_iota(jnp.int32, (1, LANES), 1)
    is_fwd = lane < DK_A
    keep_f = jnp.where(is_fwd, 1.0, 0.0).astype(BF16)
    keep_b = jnp.where(is_fwd, 0.0, 1.0).astype(BF16)

    def split(x):
        return jnp.concatenate([x * keep_f, x * keep_b], axis=0)

    for h in heads:
        if has_s0:
            st_scr[h] = jnp.concatenate([jnp.transpose(s0_ref[0, 0, 0, h]), jnp.transpose(s0_ref[0, 0, 1, h])], axis=1)
        else:
            st_scr[h] = jnp.zeros((DV_A, LANES), F32)

    steps = GLA_STEPS if n % GLA_STEPS == 0 else 1

    def body(j, carry):
        rf = [pl.multiple_of((j * steps + s) * c, c) for s in range(steps)]
        rb = [pl.multiple_of((n - 1 - j * steps - s) * c, c) for s in range(steps)]
        sl = [slice(h * LANES, (h + 1) * LANES) for h in heads]
        for g in range(0, H_A, GLA_HEADS):
            process([(s, h) for s in range(steps) for h in heads[g:g + GLA_HEADS]], rf, rb, sl)
        return carry

    def process(units, rf, rb, sl):
        qq, kk, ll, split3 = [], [], [], []
        for s, h in units:
            qkf, qkb = qk_ref[pl.ds(rf[s], c), sl[h]], qk_ref[pl.ds(rb[s], c), sl[h]]
            qq.append(jnp.where(is_fwd, qkf, pltpu.roll(qkb, DK_A, axis=1)))
            kk.append(jnp.where(is_fwd, pltpu.roll(qkf, DK_A, axis=1), qkb))
            ll.append(jnp.where(is_fwd, la_ref[pl.ds(rf[s], c), sl[h]], la_ref[pl.ds(rb[s], c), sl[h]]))
            hi = ll[-1].astype(BF16)
            r1 = ll[-1] - hi.astype(F32)
            mid = r1.astype(BF16)
            lo = (r1 - mid.astype(F32)).astype(BF16)
            split3.append(jnp.concatenate([hi, mid, lo], axis=0))
        e_all = _dot(m3_ref[...], jnp.concatenate(split3, axis=1))
        e = [e_all[:, k * LANES:(k + 1) * LANES] for k in range(len(units))]
        cx = [jnp.where(is_fwd, e[k][0:c], e[k][0:c] - ll[k]) for k in range(len(units))]
        a = [_dot_nt(qq[k].astype(BF16), split(kk[k].astype(BF16))) * mask_ref[levels] for k in range(len(units))]
        for lev in range(levels):
            for k in range(len(units)):
                ex = jnp.exp2(sg_ref[lev] * (cx[k] - e[k][(lev + 1) * c:(lev + 2) * c]))
                a[k] = a[k] + _dot_nt((qq[k] * ex).astype(BF16), split((kk[k] * ex).astype(BF16))) * mask_ref[lev]
        for k, (s, h) in enumerate(units):
            tot = e[k][c - 1:c]
            u = jnp.exp2(cx[k])
            w = jnp.exp2(tot - cx[k])
            qd = (qq[k] * jnp.where(is_fwd, u, w)).astype(BF16)
            kd = (kk[k] * jnp.where(is_fwd, w, u)).astype(BF16)
            vs = jnp.concatenate([v_ref[pl.ds(rf[s], c), sl[h]], v_ref[pl.ds(rb[s], c), sl[h]]], axis=0)
            st = st_scr[h]
            o = _dot(split(a[k].astype(BF16)), vs) + _dot_nt(split(qd), st.astype(BF16))
            of_ref[pl.ds(rf[s], c), sl[h]] = o[:c]
            ob_ref[pl.ds(rb[s], c), sl[h]] = o[c:]
            vt = jnp.transpose(vs.astype(F32)).astype(BF16)
            st_scr[h] = st * jnp.exp2(tot) + _dot(vt, split(kd))

    lax.fori_loop(0, n // steps, body, 0, unroll=GLA_UNROLL)
    for h in heads:
        sl = slice(h * LANES, (h + 1) * LANES)
        o = _rms(of_ref[:, sl] + ob_ref[:, sl], gain_ref[...]) * _silu(ga_ref[:, sl])
        o_ref[:, sl] = o.astype(BF16)
        if slot is not None:
            st = st_scr[h]
            st_ref[0, slot.at(), 0, h] = jnp.transpose(st[:, :DK_A])
            st_ref[0, slot.at(), 1, h] = jnp.transpose(st[:, DK_A:])
    if slot is not None:
        slot.zero_others(st_ref)


def _gla_call(qk, la, v, ga, gain, consts, s0, s0_slot, row0, nb, t, slot, prev):
    m3, sg, mask, levels = consts
    blk0 = row0 // t
    width = H_A * LANES
    tok = pl.BlockSpec((t, width), lambda b: (blk0 + b, 0))
    in_specs = [tok, tok, tok, tok, _resident(gain.shape), _resident(m3.shape), _resident(sg.shape), _resident(mask.shape)]
    args = [qk, la, v, ga, gain, m3, sg, mask]
    tail = (2, H_A, DK_A, DV_A)
    if s0 is not None:
        in_specs.append(pl.BlockSpec((1, 1) + tail, lambda b: (b, s0_slot, 0, 0, 0, 0)))
        args.append(s0)
    out_specs = [pl.BlockSpec((t, width), lambda b: (b, 0))]
    out_shape = [jax.ShapeDtypeStruct((nb * t, width), BF16)]
    aliases = {}
    if slot is not None:
        out_specs.append(slot.spec(tail, lambda b: b))
        out_shape.append(slot.shape(nb, tail))
        if not slot.first:
            aliases = {len(args): 1}
            in_specs.append(pl.BlockSpec(memory_space=pl.ANY))
            args.append(prev)
    return pl.pallas_call(
        functools.partial(_gla_kernel, t=t, has_s0=s0 is not None, slot=slot, levels=levels),
        grid=(nb,),
        in_specs=in_specs,
        out_specs=out_specs,
        out_shape=out_shape,
        scratch_shapes=[pltpu.VMEM((t, width), F32), pltpu.VMEM((t, width), F32), pltpu.VMEM((H_A, DV_A, LANES), F32)],
        input_output_aliases=aliases,
        name="gla",
        compiler_params=_cparams(("arbitrary",)),
    )(*args)


def _kv_expand_kernel(c_ref, r_ref, w_ref, o_ref, ro_ref):
    o_ref[...] = _dot(c_ref[0, 0].astype(BF16), w_ref[...]).astype(BF16)
    kr = r_ref[0, 0]
    ro_ref[...] = jnp.concatenate([kr, jnp.zeros((kr.shape[0], LANES - ROPE_B), F32)], axis=1).astype(BF16)


def _kv_expand_call(cache_ckv, cache_kr, slot, wukv):
    nb, _, past, r = cache_ckv.shape
    return pl.pallas_call(
        _kv_expand_kernel,
        grid=(nb,),
        in_specs=[pl.BlockSpec((1, 1, past, r), lambda b: (b, slot, 0, 0)),
                  pl.BlockSpec((1, 1, past, ROPE_B), lambda b: (b, slot, 0, 0)), _resident(wukv.shape)],
        out_specs=[pl.BlockSpec((past, wukv.shape[1]), lambda b: (b, 0)), pl.BlockSpec((past, LANES), lambda b: (b, 0))],
        out_shape=[jax.ShapeDtypeStruct((nb * past, wukv.shape[1]), BF16), jax.ShapeDtypeStruct((nb * past, LANES), BF16)],
        name="kv_expand",
        compiler_params=_cparams(("arbitrary",)),
    )(cache_ckv, cache_kr, wukv)


def _attn_group(nb, per, row0, t, seg_blocks):
    g = ATTN_GROUP
    if per == 1 and nb % g == 0 and (row0 // t) % g == 0 and all(b0 % g == 0 for b0 in seg_blocks):
        return g
    return 1


def _mla_attn_kernel(*refs, nseg, group):
    q_ref = refs[0]
    seg = [(refs[1 + 2 * s], refs[2 + 2 * s]) for s in range(nseg)]
    o_ref = refs[1 + 2 * nseg]
    tq = q_ref.shape[0] // group
    units = [(g, hh) for g in range(group) for hh in range(H_B)]

    def keys(ref, g):
        s = ref.shape[0] // group
        return slice(g * s, (g + 1) * s)

    def qk(u):
        g, hh = units[u]
        q = q_ref[g * tq:(g + 1) * tq, hh * 2 * LANES:(hh + 1) * 2 * LANES]
        scores = []
        for kv_ref, kr_ref in seg:
            kcat = jnp.concatenate([kv_ref[keys(kv_ref, g), hh * 2 * LANES:hh * 2 * LANES + LANES],
                                    kr_ref[keys(kr_ref, g), :]], axis=1)
            scores.append(_dot_nt(q, kcat) * (MLA_SCALE * LOG2E))
        return scores

    nxt = qk(0)
    for u, (g, hh) in enumerate(units):
        scores = nxt
        if u + 1 < len(units):
            nxt = qk(u + 1)
        mx = functools.reduce(jnp.maximum, [jnp.max(s, axis=-1, keepdims=True) for s in scores])
        ps = [jnp.exp2(s - mx) for s in scores]
        den = functools.reduce(lambda a, b: a + b, [jnp.sum(p, axis=-1, keepdims=True) for p in ps])
        acc = functools.reduce(lambda a, b: a + b, [
            _dot(p.astype(BF16), kv_ref[keys(kv_ref, g), hh * 2 * LANES + LANES:(hh + 1) * 2 * LANES])
            for p, (kv_ref, _) in zip(ps, seg)])
        o_ref[g * tq:(g + 1) * tq, hh * LANES:(hh + 1) * LANES] = (acc * (1.0 / den)).astype(BF16)


def _mla_attn_call(qb, segs, row0, nb, t):
    tq = min(t, MLA_TQ)
    assert t % tq == 0 and row0 % tq == 0
    per = t // tq
    group = _attn_group(nb, per, row0, t, [r0 // s for _, _, r0, s in segs])
    tq, nb = tq * group, nb // group
    q0 = row0 // tq
    in_specs = [pl.BlockSpec((tq, qb.shape[1]), lambda b, j: (q0 + b * per + j, 0))]
    args = [qb]
    for kv, kr, r0, s in segs:
        b0 = r0 // (s * group)
        in_specs.append(pl.BlockSpec((s * group, kv.shape[1]), lambda b, j, b0=b0: (b0 + b, 0)))
        in_specs.append(pl.BlockSpec((s * group, LANES), lambda b, j, b0=b0: (b0 + b, 0)))
        args += [kv, kr]
    return pl.pallas_call(
        functools.partial(_mla_attn_kernel, nseg=len(segs), group=group),
        grid=(nb, per),
        in_specs=in_specs,
        out_specs=pl.BlockSpec((tq, H_B * V_B), lambda b, j: (b * per + j, 0)),
        out_shape=jax.ShapeDtypeStruct((nb * per * tq, H_B * V_B), BF16),
        name="mla_attn",
        compiler_params=_cparams(("arbitrary", "arbitrary")),
    )(*args)


def _odd_proj_kernel(*refs, prompt_tiles, slot):
    x_ref, mod_ref, gain_ref, w_ref, tab_ref = refs[:5]
    q_ref, kb_ref, vb_ref, kf_ref, vf_ref = refs[len(refs) - 5:]
    mod = mod_ref[0, 0]
    n = q_ref.shape[1]
    heads = n // LANES
    seq = kf_ref.shape[2] // heads
    parts = [slice(b * seq, (b + 1) * seq) for b in range(slot.per_block)]
    x = x_ref[...]
    ys = [_dot(_normmod(x[r], gain_ref[0, 0:1], mod[0:1], mod[1:2]).astype(BF16), w_ref[...]) for r in parts]
    for r, y in zip(parts, ys):
        cos, sin_dn, sin_up = tab_ref[0, r, :], tab_ref[1, r, :], tab_ref[2, r, :]
        vb_ref[r, :] = y[:, 2 * n:3 * n].astype(BF16)
        for g in range(heads):
            lo = g * LANES
            q_ref[r, lo:lo + LANES] = _rope(y[:, lo:lo + LANES], cos, sin_dn, sin_up).astype(BF16)
            kb_ref[r, lo:lo + LANES] = _rope(y[:, n + lo:n + lo + LANES], cos, sin_dn, sin_up).astype(BF16)

    @pl.when(pl.program_id(0) < prompt_tiles)
    def _():
        for b in range(slot.per_block):
            for g in range(heads):
                lo = g * LANES
                kf_ref[b, slot.at(), pl.ds(g, seq, stride=heads), :] = ys[b][:, n + lo:n + lo + LANES]
                vf_ref[b, slot.at(), pl.ds(g, seq, stride=heads), :] = ys[b][:, 2 * n + lo:2 * n + lo + LANES]
        slot.zero_others(kf_ref)
        slot.zero_others(vf_ref)


def _odd_proj_call(lay, x, mods, l, gains, w, tab, slot, prev):
    d = x.shape[1]
    n = w.shape[1] // 3
    row = lambda m: pl.BlockSpec((TM, m), lambda i: (i, 0))
    in_specs = [row(d), _mod_spec(lay, l, d), _layer_spec(gains.shape, l), _resident(w.shape), _tab_spec(lay)]
    args = [x, mods, gains, w, tab]
    aliases = {}
    if not slot.first:
        aliases = {len(args): 3, len(args) + 1: 4}
        in_specs += [pl.BlockSpec(memory_space=pl.ANY), pl.BlockSpec(memory_space=pl.ANY)]
        args += list(prev)
    tail = (lay.seq_p * H_C, 2 * DH_C)
    nb_p = lay.np // lay.seq_p
    return pl.pallas_call(
        functools.partial(_odd_proj_kernel, prompt_tiles=lay.pt, slot=slot),
        grid=(lay.tiles,),
        in_specs=in_specs,
        out_specs=[row(n), row(n), row(n), slot.spec(tail, lay.p_block), slot.spec(tail, lay.p_block)],
        out_shape=[jax.ShapeDtypeStruct((lay.nt, n), BF16)] * 3 + [slot.shape(nb_p, tail)] * 2,
        input_output_aliases=aliases,
        name="odd_proj",
        compiler_params=_cparams(("arbitrary",)),
    )(*args)


def _diff_attn_kernel(*refs, nseg, has_ctx, lam_init):
    q_ref, dl_ref, gain_ref = refs[0], refs[1], refs[2]
    seg = [(refs[3 + 2 * s], refs[4 + 2 * s]) for s in range(nseg)]
    o_ref = refs[3 + 2 * nseg]
    if has_ctx:
        (kc_ref, vc_ref), (kc_scr, vc_scr) = seg[0], refs[4 + 2 * nseg:6 + 2 * nseg]

        @pl.when(pl.program_id(1) == 0)
        def _():
            for hh in range(H_C):
                rows = pl.ds(hh, kc_scr.shape[0], stride=H_C)
                kc_scr[:, hh * LANES:(hh + 1) * LANES] = kc_ref[0, 0, rows, :].astype(BF16)
                vc_scr[:, hh * LANES:(hh + 1) * LANES] = vc_ref[0, 0, rows, :].astype(BF16)

        seg[0] = (kc_scr, vc_scr)
    dl = dl_ref[0]
    lam = (jnp.exp(jnp.sum(dl[0:1] * dl[1:2], axis=-1, keepdims=True))
           - jnp.exp(jnp.sum(dl[2:3] * dl[3:4], axis=-1, keepdims=True)) + lam_init)
    tq = q_ref.shape[0]
    lane = lax.broadcasted_iota(jnp.int32, (1, LANES), 1)
    keep1 = jnp.where(lane < DH_C, 1.0, 0.0).astype(BF16)
    keep2 = jnp.where(lane < DH_C, 0.0, 1.0).astype(BF16)
    def qk(hh):
        q = q_ref[:, hh * LANES:(hh + 1) * LANES]
        qq = jnp.concatenate([q * keep1, q * keep2], axis=0)
        return [_dot_nt(qq, k_ref[:, hh * LANES:(hh + 1) * LANES]) for k_ref, _ in seg]

    nxt = qk(0)
    for hh in range(H_C):
        scores = nxt
        if hh + 1 < H_C:
            nxt = qk(hh + 1)
        mx = functools.reduce(jnp.maximum, [jnp.max(s, axis=-1, keepdims=True) for s in scores])
        ps = [jnp.exp2(s - mx) for s in scores]
        den = functools.reduce(lambda a, b: a + b, [jnp.sum(p, axis=-1, keepdims=True) for p in ps])
        acc = functools.reduce(lambda a, b: a + b, [
            _dot(p.astype(BF16), v_ref[:, hh * LANES:(hh + 1) * LANES]) for p, (_, v_ref) in zip(ps, seg)])
        acc = acc * (1.0 / den)
        o = acc[:tq] - lam * acc[tq:]
        o_ref[:, hh * LANES:(hh + 1) * LANES] = (_rms(o, gain_ref[0]) * (1.0 - lam_init)).astype(BF16)


def _diff_attn_call(q, dls, gains, slot, ctx, segs, row0, nb, t, lam_init):
    tq = min(t, TQ)
    assert t % tq == 0 and row0 % tq == 0
    q0 = row0 // tq
    per = t // tq
    n = q.shape[1]
    in_specs = [pl.BlockSpec((tq, n), lambda b, j: (q0 + b * per + j, 0)),
                pl.BlockSpec((1,) + dls.shape[1:], lambda b, j: (slot, 0, 0)),
                pl.BlockSpec((1, 1, gains.shape[2]), lambda b, j: (slot, 0, 0))]
    args = [q, dls, gains]
    scratch = []
    if ctx is not None:
        nb_c, n_l, past, heads, dh = ctx[0].shape
        for cache in ctx:
            in_specs.append(pl.BlockSpec((1, 1, past * heads, dh), lambda b, j: (b, slot, 0, 0)))
            args.append(cache.reshape(nb_c, n_l, past * heads, dh))
        scratch = [pltpu.VMEM((past, n), BF16), pltpu.VMEM((past, n), BF16)]
    for k, v, r0, s in segs:
        b0 = r0 // s
        in_specs.append(pl.BlockSpec((s, n), lambda b, j, b0=b0: (b0 + b, 0)))
        in_specs.append(pl.BlockSpec((s, n), lambda b, j, b0=b0: (b0 + b, 0)))
        args += [k, v]
    return pl.pallas_call(
        functools.partial(_diff_attn_kernel, nseg=len(segs) + (ctx is not None), has_ctx=ctx is not None,
                          lam_init=lam_init),
        grid=(nb, per),
        in_specs=in_specs,
        out_specs=pl.BlockSpec((tq, n), lambda b, j: (b * per + j, 0)),
        out_shape=jax.ShapeDtypeStruct((nb * t, n), BF16),
        scratch_shapes=scratch,
        name="diff_attn",
        compiler_params=_cparams(("arbitrary", "arbitrary")),
    )(*args)


def _halo_specs(rows, width, tm, tile):
    per = tm // HALO
    last = rows // HALO - 1
    return [pl.BlockSpec((HALO, width), lambda i: (jnp.maximum(tile(i) * per - 1, 0), 0)),
            pl.BlockSpec((tm, width), lambda i: (tile(i), 0)),
            pl.BlockSpec((HALO, width), lambda i: (jnp.minimum((tile(i) + 1) * per, last), 0))]


def _ffn_kernel(*refs, lay, d_ff, final, nx, widths):
    pos = 3 * nx + 6 * len(widths)
    trips = [refs[3 * t:3 * t + 3] for t in range(pos // 3)]
    mod_ref, gain_ref, wo_ref, wup_ref, cw_ref, cb_ref, wdn_ref, fg_ref = refs[pos:pos + 8]
    outs = refs[pos + 8:len(refs) - 2]
    u_ref, act_ref = refs[len(refs) - 2:]
    tm = lay.tm
    i = pl.program_id(0)
    row = i * tm
    is_prompt = i < lay.pt
    seq = jnp.where(is_prompt, lay.seq_p, lay.seq_s)
    first = (row & (seq - 1)) == 0
    last = ((row + tm) & (seq - 1)) == 0

    def ext(trip_p, trip_s):
        if trip_s is None:
            return jnp.concatenate([r[...] for r in trip_p], axis=0)
        return jnp.concatenate([jnp.where(is_prompt, a[...], b[...]) for a, b in zip(trip_p, trip_s)], axis=0)

    mod = mod_ref[0, 0]
    x_ext = ext(trips[0], trips[1] if nx == 2 else None)
    proj = None
    k0 = 0
    for p, w in enumerate(widths):
        part = _dot(ext(trips[nx + 2 * p], trips[nx + 2 * p + 1]), wo_ref[0, k0:k0 + w, :])
        k0 += w
        proj = part if proj is None else proj + part
    xm = x_ext + mod[2:3] * proj
    gain, shift, scale = gain_ref[0, 1:2], mod[3:4], mod[4:5]
    hh = _normmod(xm, gain, shift, scale)
    h = jnp.concatenate([jnp.where(first, 0.0, hh[HALO - SUBLANES:HALO]), hh[HALO:HALO + tm],
                         jnp.where(last, 0.0, hh[HALO + tm:HALO + tm + SUBLANES])], axis=0).astype(BF16)
    x = xm[HALO:HALO + tm]
    acc = jnp.zeros((tm, x.shape[1]), F32)
    n_f = d_ff // FFN_TF
    n_slots = u_ref.shape[0]
    sub = lax.broadcasted_iota(jnp.int32, (SUBLANES, FFN_TF), 0)
    cut_prev = is_prompt & (sub == 0)
    cut_next = is_prompt & (sub == SUBLANES - 1)
    seams = range(lay.seq_p, tm, lay.seq_p)

    def up(f):
        for half, base in enumerate((0, d_ff)):
            lo = base + f * FFN_TF
            u_ref[f % n_slots, :, half * FFN_TF:(half + 1) * FFN_TF] = _dot(h, wup_ref[0, :, lo:lo + FFN_TF])

    def conv(f, half):
        lo = half * d_ff + f * FFN_TF
        cols = slice(half * FFN_TF, (half + 1) * FFN_TF)
        cw = cw_ref[0, :, lo:lo + FFN_TF]
        prev = u_ref[f % n_slots, SUBLANES - 1:SUBLANES - 1 + tm, cols]
        here = u_ref[f % n_slots, SUBLANES:SUBLANES + tm, cols]
        nxt = u_ref[f % n_slots, SUBLANES + 1:SUBLANES + 1 + tm, cols]
        for r in seams:
            prev = jnp.concatenate([prev[:r], jnp.where(cut_prev, 0.0, prev[r:r + SUBLANES]), prev[r + SUBLANES:]], axis=0)
            nxt = jnp.concatenate([nxt[:r - SUBLANES], jnp.where(cut_next, 0.0, nxt[r - SUBLANES:r]), nxt[r:]], axis=0)
        return prev * cw[0:1] + here * cw[1:2] + nxt * cw[2:3] + cb_ref[0, :, lo:lo + FFN_TF]

    for f in range(min(FFN_AHEAD, n_f)):
        up(f)
    for f in range(n_f):
        act_ref[:, f * FFN_TF:(f + 1) * FFN_TF] = (_silu(conv(f, 0)) * conv(f, 1)).astype(BF16)
        if f + FFN_AHEAD < n_f:
            up(f + FFN_AHEAD)
        if (f + 1) % FFN_DOWN == 0 or f + 1 == n_f:
            k0 = (f // FFN_DOWN) * FFN_DOWN * FFN_TF
            acc = acc + _dot(act_ref[:, k0:(f + 1) * FFN_TF], wdn_ref[0, k0:(f + 1) * FFN_TF, :])
    y = x + mod[5:6] * acc
    if not final:
        outs[0][...] = y
    else:
        yn = _rms(y, fg_ref[...])

        @pl.when(i < lay.pt)
        def _():
            outs[0][...] = yn

        @pl.when(i >= lay.pt)
        def _():
            outs[1][...] = yn


def _ffn_call(lay, x, pairs, mods, l, gains, wo, slot, wup, cw, cb, wdn, fgain, final):
    d = wo.shape[2]
    d_ff = wdn.shape[1]
    assert d_ff % FFN_TF == 0
    tm = lay.tm
    in_specs, args = [], []
    if isinstance(x, tuple):
        streams = [(x[0], lay.p_block), (x[1], lay.s_block)]
    else:
        streams = [(x, lambda i: i)]
    for a_p, a_s in pairs:
        streams += [(a_p, lay.p_block), (a_s, lay.s_block)]
    for arr, tile in streams:
        in_specs += _halo_specs(arr.shape[0], arr.shape[1], tm, tile)
        args += [arr, arr, arr]
    in_specs += [_mod_spec(lay, l, d), _layer_spec(gains.shape, l), _layer_spec(wo.shape, slot), _layer_spec(wup.shape, l),
                 _layer_spec(cw.shape, l), _layer_spec(cb.shape, l), _layer_spec(wdn.shape, l), _resident((1, d))]
    args += [mods, gains, wo, wup, cw, cb, wdn, fgain]
    if final:
        out_specs = [pl.BlockSpec((tm, d), lambda i: (lay.p_block(i), 0)), pl.BlockSpec((tm, d), lambda i: (lay.s_block(i), 0))]
        out_shape = [jax.ShapeDtypeStruct((lay.np, d), F32), jax.ShapeDtypeStruct((lay.ns, d), F32)]
    else:
        out_specs = [pl.BlockSpec((tm, d), lambda i: (i, 0))]
        out_shape = [jax.ShapeDtypeStruct((lay.nt, d), F32)]
    return pl.pallas_call(
        functools.partial(_ffn_kernel, lay=lay, d_ff=d_ff, final=final, nx=2 if isinstance(x, tuple) else 1,
                          widths=[a_p.shape[1] for a_p, _ in pairs]),
        name="ffn",
        grid=(lay.tiles,),
        in_specs=in_specs,
        out_specs=out_specs,
        out_shape=out_shape,
        scratch_shapes=[pltpu.VMEM((FFN_AHEAD + 1, tm + 2 * SUBLANES, 2 * FFN_TF), F32), pltpu.VMEM((tm, d_ff), BF16)],
        compiler_params=_cparams(("arbitrary",)),
    )(*args)


def _rope_tables(seq_s, both_halves):
    t = jnp.arange(seq_s)
    rowp = (t // GRID_W).astype(F32)
    colp = (t % GRID_W).astype(F32)
    half = ROPE_B // 2
    inv = ROPE_BASE ** (-jnp.arange(0, half, 2, dtype=F32) / half)
    ar, ac = rowp[:, None] * inv, colp[:, None] * inv
    zero = jnp.zeros_like(ar)
    cos = jnp.concatenate([jnp.cos(ar), jnp.cos(ar), jnp.cos(ac), jnp.cos(ac)], -1)
    sin_dn = jnp.concatenate([zero, jnp.sin(ar), zero, jnp.sin(ac)], -1)
    sin_up = jnp.concatenate([-jnp.sin(ar), zero, -jnp.sin(ac), zero], -1)
    if both_halves:
        parts = [jnp.concatenate([p, p], -1) for p in (cos, sin_dn, sin_up)]
    else:
        one, zz = jnp.ones_like(cos), jnp.zeros_like(cos)
        parts = [jnp.concatenate([cos, one], -1), jnp.concatenate([sin_dn, zz], -1), jnp.concatenate([sin_up, zz], -1)]
    ident = [jnp.ones((seq_s, LANES), F32), jnp.zeros((seq_s, LANES), F32), jnp.zeros((seq_s, LANES), F32)]
    return jnp.stack([jnp.concatenate([i_, p], 0) for i_, p in zip(ident, parts)], 0)


def _even_weights(w_in, w_dec, b_dec, w_uq):
    d = w_in.shape[0]
    sizes = (H_A * DK_A, H_A * DK_A, H_A * DV_A, H_A * DV_A, GLA_LR, GLA_LR, Q_RANK, KV_RANK, ROPE_B)
    offs = np.concatenate([[0], np.cumsum(sizes)])
    qa, ka, va, ga, lrf, lrb, cq, ckv, kr = [w_in[:, offs[j]:offs[j + 1]] for j in range(9)]
    qa = qa * (DK_A ** -0.5)
    qk = jnp.concatenate([qa.reshape(d, H_A, DK_A), ka.reshape(d, H_A, DK_A)], -1).reshape(d, 2 * H_A * DK_A)
    pad = jnp.zeros((d, LANES - ROPE_B - 2 * GLA_LR), w_in.dtype)
    w = jnp.concatenate([qk, va, ga, cq, ckv, kr, lrf, lrb, pad], -1).astype(BF16)
    wd = jnp.zeros((LANES, H_A, 2, DK_A), F32)
    wd = wd.at[ROPE_B:ROPE_B + GLA_LR, :, 0, :].set(w_dec[0].reshape(GLA_LR, H_A, DK_A))
    wd = wd.at[ROPE_B + GLA_LR:ROPE_B + 2 * GLA_LR, :, 1, :].set(w_dec[1].reshape(GLA_LR, H_A, DK_A))
    wd = wd.reshape(LANES, 2 * H_A * DK_A).astype(BF16)
    bd = jnp.stack([b_dec[0].reshape(H_A, DK_A), b_dec[1].reshape(H_A, DK_A)], 1).reshape(1, 2 * H_A * DK_A)
    wq = w_uq.reshape(Q_RANK, H_B, NOPE_B + ROPE_B)
    wq = jnp.concatenate([wq, jnp.zeros((Q_RANK, H_B, 2 * LANES - NOPE_B - ROPE_B), w_uq.dtype)], -1)
    wq = wq.reshape(Q_RANK, H_B * 2 * LANES).astype(BF16)
    return w, wd, bd, wq


def kernel(x_prompt, x_sample, state_gla, cache_mla_ckv, cache_mla_krope, cache_diff_k, cache_diff_v, c, c_ctx, w_ada, b_ada, norm_gain, final_gain, w_in_even, w_out_even, gla_w_decay, gla_b_decay, gla_norm, mla_q_norm, mla_kv_norm, mla_w_uq, mla_w_ukv, w_in_odd, w_out_odd, diff_lambda, diff_norm, ffn_w_up, ffn_conv_w, ffn_conv_b, ffn_w_down):
    nb_p, seq_p, d = x_prompt.shape
    nb_s, seq_s, _ = x_sample.shape
    past = cache_mla_ckv.shape[2]
    depth = w_ada.shape[0]
    n_even, n_odd = w_in_even.shape[0], w_in_odd.shape[0]
    n_p, n_s = nb_p * seq_p, nb_s * seq_s
    lay = _Layout(n_p, n_s, seq_p, seq_s, TM)
    lay_ffn = _Layout(n_p, n_s, seq_p, seq_s, FFN_TM)
    per_tile = TM // seq_p
    assert seq_s % GRID_W == 0 and 1 + nb_s <= ADA_ROWS
    assert n_p % seq_s == 0 and seq_p & (seq_p - 1) == 0 and seq_s & (seq_s - 1) == 0
    assert seq_p % GLA_C == 0 and seq_s % GLA_C == 0

    cond = jnp.concatenate([c_ctx[None, :], c, jnp.zeros((ADA_ROWS - 1 - nb_s, d), F32)], 0)
    mods = _ada_call(cond, w_ada, b_ada).reshape(depth, ADA_ROWS, 6, d)

    tab_mla = _rope_tables(seq_s, both_halves=False)
    tab_diff = _rope_tables(seq_s, both_halves=True)
    gla_consts = _gla_consts(GLA_C)
    wup_all = ffn_w_up.astype(BF16)
    wdn_all = ffn_w_down.astype(BF16)
    cb_all = ffn_conv_b.reshape(depth, 1, -1)
    wo_even = w_out_even.astype(BF16)
    wo_odd = w_out_odd.astype(BF16)
    fgain = final_gain.reshape(1, d)
    dgains = diff_norm.reshape(n_odd, 1, 2 * DH_C)

    x = (x_prompt.reshape(n_p, d), x_sample.reshape(n_s, d))
    st_new = caches_even = caches_odd = None
    for l in range(depth):
        i = l // 2
        if l % 2 == 0:
            slot = _Slot(caches_even, i, n_even, per_tile)
            w, wd, bd, wq = _even_weights(w_in_even[i], gla_w_decay[i], gla_b_decay[i], mla_w_uq[i])
            wukv = mla_w_ukv[i].astype(BF16)
            qk, la, v, ga, qb, kra, kv, ckv_new, kr_new = _even_proj_call(
                lay, x, mods, l, norm_gain, w, wd, bd, mla_q_norm[i].reshape(1, -1), mla_kv_norm[i].reshape(1, -1),
                wq, wukv, tab_mla, slot, caches_even)
            caches_even = (ckv_new, kr_new)
            ggain = gla_norm[i].reshape(1, DV_A)
            og_p, st_new = _gla_call(qk, la, v, ga, ggain, gla_consts, None, 0, 0, nb_p, seq_p,
                                     _Slot(st_new, i, n_even), st_new)
            og_s, = _gla_call(qk, la, v, ga, ggain, gla_consts, state_gla, i, n_p, nb_s, seq_s, None, None)
            kv_ctx, kr_ctx = _kv_expand_call(cache_mla_ckv, cache_mla_krope, i, wukv)
            om_p = _mla_attn_call(qb, [(kv, kra, 0, seq_p)], 0, nb_p, seq_p)
            om_s = _mla_attn_call(qb, [(kv_ctx, kr_ctx, 0, past), (kv, kra, n_p, seq_s)], n_p, nb_s, seq_s)
            pairs, wo = [(og_p, og_s), (om_p, om_s)], wo_even
        else:
            slot = _Slot(caches_odd, i, n_odd, per_tile)
            lam_init = 0.8 - 0.6 * math.exp(-0.3 * l)
            n = H_C * 2 * DH_C
            col_scale = np.concatenate([np.full(n, DIFF_SCALE * LOG2E, np.float32), np.ones(2 * n, np.float32)])
            wi = (w_in_odd[i] * col_scale).astype(BF16)
            q, kb, vb, kf_new, vf_new = _odd_proj_call(lay, x, mods, l, norm_gain, wi, tab_diff, slot, caches_odd)
            caches_odd = (kf_new, vf_new)
            od_p = _diff_attn_call(q, diff_lambda, dgains, i, None, [(kb, vb, 0, seq_p)], 0, nb_p, seq_p, lam_init)
            od_s = _diff_attn_call(q, diff_lambda, dgains, i, (cache_diff_k, cache_diff_v), [(kb, vb, n_p, seq_s)],
                                   n_p, nb_s, seq_s, lam_init)
            pairs, wo = [(od_p, od_s)], wo_odd
        x = _ffn_call(lay_ffn, x, pairs, mods, l, norm_gain, wo, i, wup_all, ffn_conv_w, cb_all, wdn_all, fgain,
                      final=(l == depth - 1))
        if l < depth - 1:
            x = x[0]
    y_prompt = x[0].reshape(nb_p, seq_p, d)
    y_sample = x[1].reshape(nb_s, seq_s, d)
    diff_shape = (nb_p, n_odd, seq_p, H_C, 2 * DH_C)
    return (y_prompt, y_sample, st_new, caches_even[0], caches_even[1],
            caches_odd[0].reshape(diff_shape), caches_odd[1].reshape(diff_shape))
```

```python
import functools
import math

import numpy as np
import jax
import jax.numpy as jnp
from jax import lax
from jax.experimental import pallas as pl
from jax.experimental.pallas import tpu as pltpu

GRID_W = 64
ROPE_BASE = 10000.0
EPS = 1e-6
H_A, DK_A, DV_A = 4, 64, 128
GLA_LR = 16
GLA_TAU = 16.0
H_B, Q_RANK, KV_RANK, NOPE_B, ROPE_B, V_B = 4, 256, 256, 128, 64, 128
MLA_SCALE = (NOPE_B + ROPE_B) ** -0.5
H_C, DH_C = 8, 64
DIFF_SCALE = DH_C ** -0.5
LOG2E = math.log2(math.e)

LANES = 128
SUBLANES = 8
HALO = 16
VMEM_LIMIT = 56 * 1024 * 1024

TM = 512
TQ = 256
MLA_TQ = 512
ATTN_GROUP = 8
GLA_C = 64
GLA_STEPS = 1
GLA_HEADS = 4
GLA_UNROLL = 8
ADA_TN = 1536
FFN_TM = 512
FFN_TF = 256
FFN_AHEAD = 2
FFN_DOWN = 4
ADA_ROWS = 16

F32 = jnp.float32
BF16 = jnp.bfloat16


def _dot(a, b):
    return jnp.dot(a, b, preferred_element_type=F32)


def _dot_nt(a, b):
    return lax.dot_general(a, b, (((1,), (1,)), ((), ())), preferred_element_type=F32)


def _silu(x):
    return x * (1.0 / (1.0 + jnp.exp(-x)))


def _rms(x, gain):
    return x * lax.rsqrt(jnp.mean(x * x, axis=-1, keepdims=True) + EPS) * gain


def _normmod(x, gain, shift, scale):
    return _rms(x, gain) * (1.0 + scale) + shift


def _rope(x, cos, sin_dn, sin_up):
    return (x * cos + pltpu.roll(x, 16, axis=1) * sin_dn + pltpu.roll(x, LANES - 16, axis=1) * sin_up)


def _cparams(sem, vmem=VMEM_LIMIT):
    return pltpu.CompilerParams(dimension_semantics=sem, vmem_limit_bytes=vmem)


def _resident(shape):
    nd = len(shape)
    return pl.BlockSpec(shape, lambda *_: (0,) * nd, pipeline_mode=pl.Buffered(1))


def _layer_spec(shape, l):
    nd = len(shape)
    return pl.BlockSpec((1,) + tuple(shape[1:]), lambda *_: (l,) + (0,) * (nd - 1), pipeline_mode=pl.Buffered(1))


def _ada_kernel(c_ref, w_ref, b_ref, o_ref):
    s = _silu(c_ref[...]).astype(BF16)
    o_ref[0] = _dot(s, w_ref[0].astype(BF16)) + b_ref[0]


def _ada_call(cond, w_ada, b_ada):
    depth, d, n = w_ada.shape
    return pl.pallas_call(
        _ada_kernel,
        grid=(depth, n // ADA_TN),
        in_specs=[
            pl.BlockSpec((ADA_ROWS, d), lambda l, j: (0, 0)),
            pl.BlockSpec((1, d, ADA_TN), lambda l, j: (l, 0, j)),
            pl.BlockSpec((1, 1, ADA_TN), lambda l, j: (l, 0, j)),
        ],
        out_specs=pl.BlockSpec((1, ADA_ROWS, ADA_TN), lambda l, j: (l, 0, j)),
        out_shape=jax.ShapeDtypeStruct((depth, ADA_ROWS, n), F32),
        name="ada",
        compiler_params=_cparams(("arbitrary", "arbitrary")),
    )(cond, w_ada, b_ada.reshape(depth, 1, n))


class _Layout:
    def __init__(self, n_prompt, n_sample, seq_p, seq_s, tm):
        assert tm % seq_p == 0 and n_prompt % tm == 0 and n_sample % tm == 0 and seq_s % tm == 0
        self.tm = tm
        self.np, self.ns, self.nt = n_prompt, n_sample, n_prompt + n_sample
        self.seq_p, self.seq_s = seq_p, seq_s
        self.pt = n_prompt // tm
        self.st = n_sample // tm
        self.tiles = self.pt + self.st
        self.tiles_per_seq_s = seq_s // tm

    def group(self, i):
        return jnp.where(i < self.pt, 0, 1 + (i - self.pt) // self.tiles_per_seq_s)

    def pos_block(self, i):
        return jnp.where(i < self.pt, 0, self.tiles_per_seq_s + (i - self.pt) % self.tiles_per_seq_s)

    def p_block(self, i):
        return jnp.minimum(i, self.pt - 1)

    def s_block(self, i):
        return jnp.maximum(i - self.pt, 0)


def _mod_spec(lay, l, d):
    return pl.BlockSpec((1, 1, 6, d), lambda i: (l, lay.group(i), 0, 0))


def _tab_spec(lay):
    return pl.BlockSpec((3, lay.tm, LANES), lambda i: (0, lay.pos_block(i), 0))


def _x_specs(lay, x):
    if isinstance(x, tuple):
        d = x[0].shape[1]
        return [pl.BlockSpec((lay.tm, d), lambda i: (lay.p_block(i), 0)),
                pl.BlockSpec((lay.tm, d), lambda i: (lay.s_block(i), 0))], list(x)
    return [pl.BlockSpec((lay.tm, x.shape[1]), lambda i: (i, 0))], [x]


def _read_x(x_refs, prompt_tiles):
    if len(x_refs) == 1:
        return x_refs[0][...]
    return jnp.where(pl.program_id(0) < prompt_tiles, x_refs[0][...], x_refs[1][...])


class _Slot:
    def __init__(self, prev, slot, n_slots, per_block=1):
        self.prev, self.slot, self.n_slots, self.per_block = prev, slot, n_slots, per_block

    @property
    def first(self):
        return self.prev is None

    def spec(self, tail, index):
        zeros = (0,) * len(tail)
        if self.first:
            return pl.BlockSpec((self.per_block, self.n_slots) + tail, lambda *g: (index(*g), 0) + zeros)
        s = self.slot
        return pl.BlockSpec((self.per_block, 1) + tail, lambda *g: (index(*g), s) + zeros)

    def shape(self, nb, tail):
        return jax.ShapeDtypeStruct((nb, self.n_slots) + tail, F32)

    def at(self):
        return self.slot if self.first else 0

    def zero_others(self, ref):
        if self.first:
            for b in range(self.per_block):
                for s in range(self.n_slots):
                    if s != self.slot:
                        ref[b, s] = jnp.zeros(ref.shape[2:], F32)


def _even_proj_kernel(*refs, nx, prompt_tiles, slot):
    x_refs = refs[:nx]
    (mod_ref, gain_ref, w_ref, wd_ref, bd_ref, qn_ref, kvn_ref, wuq_ref, wukv_ref, tab_ref) = refs[nx:nx + 10]
    outs = refs[len(refs) - 9:]
    qk_ref, la_ref, v_ref, ga_ref, qb_ref, kra_ref, kv_ref, ckv_ref, kr_ref = outs
    mod = mod_ref[0, 0]
    seq = ckv_ref.shape[2]
    parts = [slice(b * seq, (b + 1) * seq) for b in range(slot.per_block)]
    x = _read_x(x_refs, prompt_tiles)
    ys = [_dot(_normmod(x[r], gain_ref[0, 0:1], mod[0:1], mod[1:2]).astype(BF16), w_ref[...]) for r in parts]
    ckvns, gs = [], []
    for r, y in zip(parts, ys):
        qk_ref[r, :] = y[:, 0:512]
        v_ref[r, :] = y[:, 512:1024].astype(BF16)
        ga_ref[r, :] = y[:, 1024:1536]
        cq, ckv, g = y[:, 1536:1792], y[:, 1792:2048], y[:, 2048:2176]
        pre = _dot(g.astype(BF16), wd_ref[...]) + bd_ref[...]
        la_ref[r, :] = (jnp.minimum(pre, 0.0) - jnp.log(1.0 + jnp.exp(-jnp.abs(pre)))) * (LOG2E / GLA_TAU)
        cos, sin_dn, sin_up = tab_ref[0, r, :], tab_ref[1, r, :], tab_ref[2, r, :]
        qb = _dot(_rms(cq, qn_ref[...]).astype(BF16), wuq_ref[...])
        for hh in range(H_B):
            lo = hh * 2 * LANES
            qb_ref[r, lo:lo + LANES] = qb[:, lo:lo + LANES].astype(BF16)
            qb_ref[r, lo + LANES:lo + 2 * LANES] = _rope(qb[:, lo + LANES:lo + 2 * LANES], cos, sin_dn, sin_up).astype(BF16)
        ckvn = _rms(ckv, kvn_ref[...])
        kv_ref[r, :] = _dot(ckvn.astype(BF16), wukv_ref[...]).astype(BF16)
        lane = lax.broadcasted_iota(jnp.int32, g.shape, 1)
        kra_ref[r, :] = jnp.where(lane < ROPE_B, _rope(g, cos, sin_dn, sin_up), 0.0).astype(BF16)
        ckvns.append(ckvn)
        gs.append(g)

    @pl.when(pl.program_id(0) < prompt_tiles)
    def _():
        for b in range(slot.per_block):
            ckv_ref[b, slot.at()] = ckvns[b]
            kr_ref[b, slot.at()] = gs[b][:, :ROPE_B]
        slot.zero_others(ckv_ref)
        slot.zero_others(kr_ref)


def _even_proj_call(lay, x, mods, l, gains, w, wd, bd, qn, kvn, wuq, wukv, tab, slot, prev):
    d = w.shape[0]
    x_specs, x_args = _x_specs(lay, x)
    row = lambda n: pl.BlockSpec((TM, n), lambda i: (i, 0))
    outs = [(512, F32), (512, F32), (512, BF16), (512, F32), (1024, BF16), (LANES, BF16), (1024, BF16)]
    in_specs = x_specs + [_mod_spec(lay, l, d), _layer_spec(gains.shape, l), _resident(w.shape), _resident(wd.shape),
                          _resident(bd.shape), _resident(qn.shape), _resident(kvn.shape), _resident(wuq.shape),
                          _resident(wukv.shape), _tab_spec(lay)]
    args = x_args + [mods, gains, w, wd, bd, qn, kvn, wuq, wukv, tab]
    aliases = {}
    if not slot.first:
        aliases = {len(args): len(outs), len(args) + 1: len(outs) + 1}
        in_specs += [pl.BlockSpec(memory_space=pl.ANY), pl.BlockSpec(memory_space=pl.ANY)]
        args += list(prev)
    nb_p = lay.np // lay.seq_p
    return pl.pallas_call(
        functools.partial(_even_proj_kernel, nx=len(x_args), prompt_tiles=lay.pt, slot=slot),
        grid=(lay.tiles,),
        in_specs=in_specs,
        out_specs=[row(n) for n, _ in outs]
        + [slot.spec((lay.seq_p, KV_RANK), lay.p_block), slot.spec((lay.seq_p, ROPE_B), lay.p_block)],
        out_shape=[jax.ShapeDtypeStruct((lay.nt, n), dt) for n, dt in outs]
        + [slot.shape(nb_p, (lay.seq_p, KV_RANK)), slot.shape(nb_p, (lay.seq_p, ROPE_B))],
        input_output_aliases=aliases,
        name="even_proj",
        compiler_params=_cparams(("arbitrary",)),
    )(*args)


def _gla_consts(c):
    levels = int(math.log2(c))
    idx = np.arange(c)
    cum = idx[None, :] <= idx[:, None]
    rows, sgn, masks = [cum], [], []
    for lev in range(levels):
        half = 1 << lev
        node = idx // (2 * half)
        mid = node * 2 * half + half
        upper = idx >= mid
        rows.append(cum[mid - 1])
        sgn.append(np.repeat(np.where(upper, 1.0, -1.0)[:, None], LANES, 1))
        same = node[:, None] == node[None, :]
        masks.append(np.concatenate([same & upper[:, None] & ~upper[None, :],
                                     same & ~upper[:, None] & upper[None, :]], 1))
    eye = np.eye(c, dtype=bool)
    masks.append(np.concatenate([eye, eye], 1))
    m = np.concatenate(rows, 0)
    return (jnp.asarray(np.concatenate([m, m, m], 1), BF16), jnp.asarray(np.stack(sgn, 0), F32),
            jnp.asarray(np.stack(masks, 0), F32), levels)


def _gla_kernel(*refs, t, has_s0, slot, levels):
    qk_ref, la_ref, v_ref, ga_ref, gain_ref, m3_ref, sg_ref, mask_ref = refs[:8]
    s0_ref = refs[8] if has_s0 else None
    n_out = 1 if slot is None else 2
    o_ref = refs[len(refs) - 3 - n_out]
    st_ref = None if slot is None else refs[len(refs) - 4]
    of_ref, ob_ref, st_scr = refs[len(refs) - 3:]
    c = GLA_C
    n = t // c
    heads = range(H_A)
    lane = lax.broadcasted_iota(jnp.int32, (1, LANES), 1)
    is_fwd = lane < DK_A
    keep_f = jnp.where(is_fwd, 1.0, 0.0).astype(BF16)
    keep_b = jnp.where(is_fwd, 0.0, 1.0).astype(BF16)

    def split(x):
        return jnp.concatenate([x * keep_f, x * keep_b], axis=0)

    for h in heads:
        if has_s0:
            st_scr[h] = jnp.concatenate([jnp.transpose(s0_ref[0, 0, 0, h]), jnp.transpose(s0_ref[0, 0, 1, h])], axis=1)
        else:
            st_scr[h] = jnp.zeros((DV_A, LANES), F32)

    steps = GLA_STEPS if n % GLA_STEPS == 0 else 1

    def body(j, carry):
        rf = [pl.multiple_of((j * steps + s) * c, c) for s in range(steps)]
        rb = [pl.multiple_of((n - 1 - j * steps - s) * c, c) for s in range(steps)]
        sl = [slice(h * LANES, (h + 1) * LANES) for h in heads]
        for g in range(0, H_A, GLA_HEADS):
            process([(s, h) for s in range(steps) for h in heads[g:g + GLA_HEADS]], rf, rb, sl)
        return carry

    def process(units, rf, rb, sl):
        qq, kk, ll, split3 = [], [], [], []
        for s, h in units:
            qkf, qkb = qk_ref[pl.ds(rf[s], c), sl[h]], qk_ref[pl.ds(rb[s], c), sl[h]]
            qq.append(jnp.where(is_fwd, qkf, pltpu.roll(qkb, DK_A, axis=1)))
            kk.append(jnp.where(is_fwd, pltpu.roll(qkf, DK_A, axis=1), qkb))
            ll.append(jnp.where(is_fwd, la_ref[pl.ds(rf[s], c), sl[h]], la_ref[pl.ds(rb[s], c), sl[h]]))
            hi = ll[-1].astype(BF16)
            r1 = ll[-1] - hi.astype(F32)
            mid = r1.astype(BF16)
            lo = (r1 - mid.astype(F32)).astype(BF16)
            split3.append(jnp.concatenate([hi, mid, lo], axis=0))
        e_all = _dot(m3_ref[...], jnp.concatenate(split3, axis=1))
        e = [e_all[:, k * LANES:(k + 1) * LANES] for k in range(len(units))]
        cx = [jnp.where(is_fwd, e[k][0:c], e[k][0:c] - ll[k]) for k in range(len(units))]
        a = [_dot_nt(qq[k].astype(BF16), split(kk[k].astype(BF16))) * mask_ref[levels] for k in range(len(units))]
        for lev in range(levels):
            for k in range(len(units)):
                ex = jnp.exp2(sg_ref[lev] * (cx[k] - e[k][(lev + 1) * c:(lev + 2) * c]))
                a[k] = a[k] + _dot_nt((qq[k] * ex).astype(BF16), split((kk[k] * ex).astype(BF16))) * mask_ref[lev]
        for k, (s, h) in enumerate(units):
            tot = e[k][c - 1:c]
            u = jnp.exp2(cx[k])
            w = jnp.exp2(tot - cx[k])
            qd = (qq[k] * jnp.where(is_fwd, u, w)).astype(BF16)
            kd = (kk[k] * jnp.where(is_fwd, w, u)).astype(BF16)
            vs = jnp.concatenate([v_ref[pl.ds(rf[s], c), sl[h]], v_ref[pl.ds(rb[s], c), sl[h]]], axis=0)
            st = st_scr[h]
            o = _dot(split(a[k].astype(BF16)), vs) + _dot_nt(split(qd), st.astype(BF16))
            of_ref[pl.ds(rf[s], c), sl[h]] = o[:c]
            ob_ref[pl.ds(rb[s], c), sl[h]] = o[c:]
            vt = jnp.transpose(vs.astype(F32)).astype(BF16)
            st_scr[h] = st * jnp.exp2(tot) + _dot(vt, split(kd))

    lax.fori_loop(0, n // steps, body, 0, unroll=min(GLA_UNROLL, n // steps))
    for h in heads:
        sl = slice(h * LANES, (h + 1) * LANES)
        o = _rms(of_ref[:, sl] + ob_ref[:, sl], gain_ref[...]) * _silu(ga_ref[:, sl])
        o_ref[:, sl] = o.astype(BF16)
        if slot is not None:
            st = st_scr[h]
            st_ref[0, slot.at(), 0, h] = jnp.transpose(st[:, :DK_A])
            st_ref[0, slot.at(), 1, h] = jnp.transpose(st[:, DK_A:])
    if slot is not None:
        slot.zero_others(st_ref)


def _gla_call(qk, la, v, ga, gain, consts, s0, s0_slot, row0, nb, t, slot, prev):
    m3, sg, mask, levels = consts
    blk0 = row0 // t
    width = H_A * LANES
    tok = pl.BlockSpec((t, width), lambda b: (blk0 + b, 0))
    in_specs = [tok, tok, tok, tok, _resident(gain.shape), _resident(m3.shape), _resident(sg.shape), _resident(mask.shape)]
    args = [qk, la, v, ga, gain, m3, sg, mask]
    tail = (2, H_A, DK_A, DV_A)
    if s0 is not None:
        in_specs.append(pl.BlockSpec((1, 1) + tail, lambda b: (b, s0_slot, 0, 0, 0, 0)))
        args.append(s0)
    out_specs = [pl.BlockSpec((t, width), lambda b: (b, 0))]
    out_shape = [jax.ShapeDtypeStruct((nb * t, width), BF16)]
    aliases = {}
    if slot is not None:
        out_specs.append(slot.spec(tail, lambda b: b))
        out_shape.append(slot.shape(nb, tail))
        if not slot.first:
            aliases = {len(args): 1}
            in_specs.append(pl.BlockSpec(memory_space=pl.ANY))
            args.append(prev)
    return pl.pallas_call(
        functools.partial(_gla_kernel, t=t, has_s0=s0 is not None, slot=slot, levels=levels),
        grid=(nb,),
        in_specs=in_specs,
        out_specs=out_specs,
        out_shape=out_shape,
        scratch_shapes=[pltpu.VMEM((t, width), F32), pltpu.VMEM((t, width), F32), pltpu.VMEM((H_A, DV_A, LANES), F32)],
        input_output_aliases=aliases,
        name="gla",
        compiler_params=_cparams(("arbitrary",)),
    )(*args)


def _kv_expand_kernel(c_ref, r_ref, w_ref, o_ref, ro_ref):
    o_ref[...] = _dot(c_ref[0, 0].astype(BF16), w_ref[...]).astype(BF16)
    kr = r_ref[0, 0]
    ro_ref[...] = jnp.concatenate([kr, jnp.zeros((kr.shape[0], LANES - ROPE_B), F32)], axis=1).astype(BF16)


def _kv_expand_call(cache_ckv, cache_kr, slot, wukv):
    nb, _, past, r = cache_ckv.shape
    return pl.pallas_call(
        _kv_expand_kernel,
        grid=(nb,),
        in_specs=[pl.BlockSpec((1, 1, past, r), lambda b: (b, slot, 0, 0)),
                  pl.BlockSpec((1, 1, past, ROPE_B), lambda b: (b, slot, 0, 0)), _resident(wukv.shape)],
        out_specs=[pl.BlockSpec((past, wukv.shape[1]), lambda b: (b, 0)), pl.BlockSpec((past, LANES), lambda b: (b, 0))],
        out_shape=[jax.ShapeDtypeStruct((nb * past, wukv.shape[1]), BF16), jax.ShapeDtypeStruct((nb * past, LANES), BF16)],
        name="kv_expand",
        compiler_params=_cparams(("arbitrary",)),
    )(cache_ckv, cache_kr, wukv)


def _attn_group(nb, per, row0, t, seg_blocks):
    g = ATTN_GROUP
    if per == 1 and nb % g == 0 and (row0 // t) % g == 0 and all(b0 % g == 0 for b0 in seg_blocks):
        return g
    return 1


def _mla_attn_kernel(*refs, nseg, group):
    q_ref = refs[0]
    seg = [(refs[1 + 2 * s], refs[2 + 2 * s]) for s in range(nseg)]
    o_ref = refs[1 + 2 * nseg]
    tq = q_ref.shape[0] // group
    units = [(g, hh) for g in range(group) for hh in range(H_B)]

    def keys(ref, g):
        s = ref.shape[0] // group
        return slice(g * s, (g + 1) * s)

    def qk(u):
        g, hh = units[u]
        q = q_ref[g * tq:(g + 1) * tq, hh * 2 * LANES:(hh + 1) * 2 * LANES]
        scores = []
        for kv_ref, kr_ref in seg:
            kcat = jnp.concatenate([kv_ref[keys(kv_ref, g), hh * 2 * LANES:hh * 2 * LANES + LANES],
                                    kr_ref[keys(kr_ref, g), :]], axis=1)
            scores.append(_dot_nt(q, kcat) * (MLA_SCALE * LOG2E))
        return scores

    nxt = qk(0)
    for u, (g, hh) in enumerate(units):
        scores = nxt
        if u + 1 < len(units):
            nxt = qk(u + 1)
        mx = functools.reduce(jnp.maximum, [jnp.max(s, axis=-1, keepdims=True) for s in scores])
        ps = [jnp.exp2(s - mx) for s in scores]
        den = functools.reduce(lambda a, b: a + b, [jnp.sum(p, axis=-1, keepdims=True) for p in ps])
        acc = functools.reduce(lambda a, b: a + b, [
            _dot(p.astype(BF16), kv_ref[keys(kv_ref, g), hh * 2 * LANES + LANES:(hh + 1) * 2 * LANES])
            for p, (kv_ref, _) in zip(ps, seg)])
        o_ref[g * tq:(g + 1) * tq, hh * LANES:(hh + 1) * LANES] = (acc * (1.0 / den)).astype(BF16)


def _mla_attn_call(qb, segs, row0, nb, t):
    tq = min(t, MLA_TQ)
    assert t % tq == 0 and row0 % tq == 0
    per = t // tq
    group = _attn_group(nb, per, row0, t, [r0 // s for _, _, r0, s in segs])
    tq, nb = tq * group, nb // group
    q0 = row0 // tq
    in_specs = [pl.BlockSpec((tq, qb.shape[1]), lambda b, j: (q0 + b * per + j, 0))]
    args = [qb]
    for kv, kr, r0, s in segs:
        b0 = r0 // (s * group)
        in_specs.append(pl.BlockSpec((s * group, kv.shape[1]), lambda b, j, b0=b0: (b0 + b, 0)))
        in_specs.append(pl.BlockSpec((s * group, LANES), lambda b, j, b0=b0: (b0 + b, 0)))
        args += [kv, kr]
    return pl.pallas_call(
        functools.partial(_mla_attn_kernel, nseg=len(segs), group=group),
        grid=(nb, per),
        in_specs=in_specs,
        out_specs=pl.BlockSpec((tq, H_B * V_B), lambda b, j: (b * per + j, 0)),
        out_shape=jax.ShapeDtypeStruct((nb * per * tq, H_B * V_B), BF16),
        name="mla_attn",
        compiler_params=_cparams(("arbitrary", "arbitrary")),
    )(*args)


def _odd_proj_kernel(*refs, prompt_tiles, slot):
    x_ref, mod_ref, gain_ref, w_ref, tab_ref = refs[:5]
    q_ref, kb_ref, vb_ref, kf_ref, vf_ref = refs[len(refs) - 5:]
    mod = mod_ref[0, 0]
    n = q_ref.shape[1]
    heads = n // LANES
    seq = kf_ref.shape[2] // heads
    parts = [slice(b * seq, (b + 1) * seq) for b in range(slot.per_block)]
    x = x_ref[...]
    ys = [_dot(_normmod(x[r], gain_ref[0, 0:1], mod[0:1], mod[1:2]).astype(BF16), w_ref[...]) for r in parts]
    for r, y in zip(parts, ys):
        cos, sin_dn, sin_up = tab_ref[0, r, :], tab_ref[1, r, :], tab_ref[2, r, :]
        vb_ref[r, :] = y[:, 2 * n:3 * n].astype(BF16)
        for g in range(heads):
            lo = g * LANES
            q_ref[r, lo:lo + LANES] = _rope(y[:, lo:lo + LANES], cos, sin_dn, sin_up).astype(BF16)
            kb_ref[r, lo:lo + LANES] = _rope(y[:, n + lo:n + lo + LANES], cos, sin_dn, sin_up).astype(BF16)

    @pl.when(pl.program_id(0) < prompt_tiles)
    def _():
        for b in range(slot.per_block):
            for g in range(heads):
                lo = g * LANES
                kf_ref[b, slot.at(), pl.ds(g, seq, stride=heads), :] = ys[b][:, n + lo:n + lo + LANES]
                vf_ref[b, slot.at(), pl.ds(g, seq, stride=heads), :] = ys[b][:, 2 * n + lo:2 * n + lo + LANES]
        slot.zero_others(kf_ref)
        slot.zero_others(vf_ref)


def _odd_proj_call(lay, x, mods, l, gains, w, tab, slot, prev):
    d = x.shape[1]
    n = w.shape[1] // 3
    row = lambda m: pl.BlockSpec((TM, m), lambda i: (i, 0))
    in_specs = [row(d), _mod_spec(lay, l, d), _layer_spec(gains.shape, l), _resident(w.shape), _tab_spec(lay)]
    args = [x, mods, gains, w, tab]
    aliases = {}
    if not slot.first:
        aliases = {len(args): 3, len(args) + 1: 4}
        in_specs += [pl.BlockSpec(memory_space=pl.ANY), pl.BlockSpec(memory_space=pl.ANY)]
        args += list(prev)
    tail = (lay.seq_p * H_C, 2 * DH_C)
    nb_p = lay.np // lay.seq_p
    return pl.pallas_call(
        functools.partial(_odd_proj_kernel, prompt_tiles=lay.pt, slot=slot),
        grid=(lay.tiles,),
        in_specs=in_specs,
        out_specs=[row(n), row(n), row(n), slot.spec(tail, lay.p_block), slot.spec(tail, lay.p_block)],
        out_shape=[jax.ShapeDtypeStruct((lay.nt, n), BF16)] * 3 + [slot.shape(nb_p, tail)] * 2,
        input_output_aliases=aliases,
        name="odd_proj",
        compiler_params=_cparams(("arbitrary",)),
    )(*args)


def _diff_attn_kernel(*refs, nseg, has_ctx, lam_init):
    q_ref, dl_ref, gain_ref = refs[0], refs[1], refs[2]
    seg = [(refs[3 + 2 * s], refs[4 + 2 * s]) for s in range(nseg)]
    o_ref = refs[3 + 2 * nseg]
    if has_ctx:
        (kc_ref, vc_ref), (kc_scr, vc_scr) = seg[0], refs[4 + 2 * nseg:6 + 2 * nseg]

        @pl.when(pl.program_id(1) == 0)
        def _():
            for hh in range(H_C):
                rows = pl.ds(hh, kc_scr.shape[0], stride=H_C)
                kc_scr[:, hh * LANES:(hh + 1) * LANES] = kc_ref[0, 0, rows, :].astype(BF16)
                vc_scr[:, hh * LANES:(hh + 1) * LANES] = vc_ref[0, 0, rows, :].astype(BF16)

        seg[0] = (kc_scr, vc_scr)
    dl = dl_ref[0]
    lam = (jnp.exp(jnp.sum(dl[0:1] * dl[1:2], axis=-1, keepdims=True))
           - jnp.exp(jnp.sum(dl[2:3] * dl[3:4], axis=-1, keepdims=True)) + lam_init)
    tq = q_ref.shape[0]
    lane = lax.broadcasted_iota(jnp.int32, (1, LANES), 1)
    keep1 = jnp.where(lane < DH_C, 1.0, 0.0).astype(BF16)
    keep2 = jnp.where(lane < DH_C, 0.0, 1.0).astype(BF16)
    def qk(hh):
        q = q_ref[:, hh * LANES:(hh + 1) * LANES]
        qq = jnp.concatenate([q * keep1, q * keep2], axis=0)
        return [_dot_nt(qq, k_ref[:, hh * LANES:(hh + 1) * LANES]) for k_ref, _ in seg]

    nxt = qk(0)
    for hh in range(H_C):
        scores = nxt
        if hh + 1 < H_C:
            nxt = qk(hh + 1)
        mx = functools.reduce(jnp.maximum, [jnp.max(s, axis=-1, keepdims=True) for s in scores])
        ps = [jnp.exp2(s - mx) for s in scores]
        den = functools.reduce(lambda a, b: a + b, [jnp.sum(p, axis=-1, keepdims=True) for p in ps])
        acc = functools.reduce(lambda a, b: a + b, [
            _dot(p.astype(BF16), v_ref[:, hh * LANES:(hh + 1) * LANES]) for p, (_, v_ref) in zip(ps, seg)])
        acc = acc * (1.0 / den)
        o = acc[:tq] - lam * acc[tq:]
        o_ref[:, hh * LANES:(hh + 1) * LANES] = (_rms(o, gain_ref[0]) * (1.0 - lam_init)).astype(BF16)


def _diff_attn_call(q, dls, gains, slot, ctx, segs, row0, nb, t, lam_init):
    tq = min(t, TQ)
    assert t % tq == 0 and row0 % tq == 0
    q0 = row0 // tq
    per = t // tq
    n = q.shape[1]
    in_specs = [pl.BlockSpec((tq, n), lambda b, j: (q0 + b * per + j, 0)),
                pl.BlockSpec((1,) + dls.shape[1:], lambda b, j: (slot, 0, 0)),
                pl.BlockSpec((1, 1, gains.shape[2]), lambda b, j: (slot, 0, 0))]
    args = [q, dls, gains]
    scratch = []
    if ctx is not None:
        nb_c, n_l, past, heads, dh = ctx[0].shape
        for cache in ctx:
            in_specs.append(pl.BlockSpec((1, 1, past * heads, dh), lambda b, j: (b, slot, 0, 0)))
            args.append(cache.reshape(nb_c, n_l, past * heads, dh))
        scratch = [pltpu.VMEM((past, n), BF16), pltpu.VMEM((past, n), BF16)]
    for k, v, r0, s in segs:
        b0 = r0 // s
        in_specs.append(pl.BlockSpec((s, n), lambda b, j, b0=b0: (b0 + b, 0)))
        in_specs.append(pl.BlockSpec((s, n), lambda b, j, b0=b0: (b0 + b, 0)))
        args += [k, v]
    return pl.pallas_call(
        functools.partial(_diff_attn_kernel, nseg=len(segs) + (ctx is not None), has_ctx=ctx is not None,
                          lam_init=lam_init),
        grid=(nb, per),
        in_specs=in_specs,
        out_specs=pl.BlockSpec((tq, n), lambda b, j: (b * per + j, 0)),
        out_shape=jax.ShapeDtypeStruct((nb * t, n), BF16),
        scratch_shapes=scratch,
        name="diff_attn",
        compiler_params=_cparams(("arbitrary", "arbitrary")),
    )(*args)


def _halo_specs(rows, width, tm, tile):
    per = tm // HALO
    last = rows // HALO - 1
    return [pl.BlockSpec((HALO, width), lambda i: (jnp.maximum(tile(i) * per - 1, 0), 0)),
            pl.BlockSpec((tm, width), lambda i: (tile(i), 0)),
            pl.BlockSpec((HALO, width), lambda i: (jnp.minimum((tile(i) + 1) * per, last), 0))]


def _ffn_kernel(*refs, lay, d_ff, final, nx, widths):
    pos = 3 * nx + 6 * len(widths)
    trips = [refs[3 * t:3 * t + 3] for t in range(pos // 3)]
    mod_ref, gain_ref, wo_ref, wup_ref, cw_ref, cb_ref, wdn_ref, fg_ref = refs[pos:pos + 8]
    outs = refs[pos + 8:len(refs) - 2]
    u_ref, act_ref = refs[len(refs) - 2:]
    tm = lay.tm
    i = pl.program_id(0)
    row = i * tm
    is_prompt = i < lay.pt
    seq = jnp.where(is_prompt, lay.seq_p, lay.seq_s)
    first = (row & (seq - 1)) == 0
    last = ((row + tm) & (seq - 1)) == 0

    def ext(trip_p, trip_s):
        if trip_s is None:
            return jnp.concatenate([r[...] for r in trip_p], axis=0)
        return jnp.concatenate([jnp.where(is_prompt, a[...], b[...]) for a, b in zip(trip_p, trip_s)], axis=0)

    mod = mod_ref[0, 0]
    x_ext = ext(trips[0], trips[1] if nx == 2 else None)
    proj = None
    k0 = 0
    for p, w in enumerate(widths):
        part = _dot(ext(trips[nx + 2 * p], trips[nx + 2 * p + 1]), wo_ref[0, k0:k0 + w, :])
        k0 += w
        proj = part if proj is None else proj + part
    xm = x_ext + mod[2:3] * proj
    gain, shift, scale = gain_ref[0, 1:2], mod[3:4], mod[4:5]
    hh = _normmod(xm, gain, shift, scale)
    h = jnp.concatenate([jnp.where(first, 0.0, hh[HALO - SUBLANES:HALO]), hh[HALO:HALO + tm],
                         jnp.where(last, 0.0, hh[HALO + tm:HALO + tm + SUBLANES])], axis=0).astype(BF16)
    x = xm[HALO:HALO + tm]
    acc = jnp.zeros((tm, x.shape[1]), F32)
    n_f = d_ff // FFN_TF
    n_slots = u_ref.shape[0]
    sub = lax.broadcasted_iota(jnp.int32, (SUBLANES, FFN_TF), 0)
    cut_prev = is_prompt & (sub == 0)
    cut_next = is_prompt & (sub == SUBLANES - 1)
    seams = range(lay.seq_p, tm, lay.seq_p)

    def up(f):
        for half, base in enumerate((0, d_ff)):
            lo = base + f * FFN_TF
            u_ref[f % n_slots, :, half * FFN_TF:(half + 1) * FFN_TF] = _dot(h, wup_ref[0, :, lo:lo + FFN_TF])

    def conv(f, half):
        lo = half * d_ff + f * FFN_TF
        cols = slice(half * FFN_TF, (half + 1) * FFN_TF)
        cw = cw_ref[0, :, lo:lo + FFN_TF]
        prev = u_ref[f % n_slots, SUBLANES - 1:SUBLANES - 1 + tm, cols]
        here = u_ref[f % n_slots, SUBLANES:SUBLANES + tm, cols]
        nxt = u_ref[f % n_slots, SUBLANES + 1:SUBLANES + 1 + tm, cols]
        for r in seams:
            prev = jnp.concatenate([prev[:r], jnp.where(cut_prev, 0.0, prev[r:r + SUBLANES]), prev[r + SUBLANES:]], axis=0)
            nxt = jnp.concatenate([nxt[:r - SUBLANES], jnp.where(cut_next, 0.0, nxt[r - SUBLANES:r]), nxt[r:]], axis=0)
        return prev * cw[0:1] + here * cw[1:2] + nxt * cw[2:3] + cb_ref[0, :, lo:lo + FFN_TF]

    for f in range(min(FFN_AHEAD, n_f)):
        up(f)
    for f in range(n_f):
        act_ref[:, f * FFN_TF:(f + 1) * FFN_TF] = (_silu(conv(f, 0)) * conv(f, 1)).astype(BF16)
        if f + FFN_AHEAD < n_f:
            up(f + FFN_AHEAD)
        if (f + 1) % FFN_DOWN == 0 or f + 1 == n_f:
            k0 = (f // FFN_DOWN) * FFN_DOWN * FFN_TF
            acc = acc + _dot(act_ref[:, k0:(f + 1) * FFN_TF], wdn_ref[0, k0:(f + 1) * FFN_TF, :])
    y = x + mod[5:6] * acc
    if not final:
        outs[0][...] = y
    else:
        yn = _rms(y, fg_ref[...])

        @pl.when(i < lay.pt)
        def _():
            outs[0][...] = yn

        @pl.when(i >= lay.pt)
        def _():
            outs[1][...] = yn


def _ffn_call(lay, x, pairs, mods, l, gains, wo, slot, wup, cw, cb, wdn, fgain, final):
    d = wo.shape[2]
    d_ff = wdn.shape[1]
    assert d_ff % FFN_TF == 0
    tm = lay.tm
    in_specs, args = [], []
    if isinstance(x, tuple):
        streams = [(x[0], lay.p_block), (x[1], lay.s_block)]
    else:
        streams = [(x, lambda i: i)]
    for a_p, a_s in pairs:
        streams += [(a_p, lay.p_block), (a_s, lay.s_block)]
    for arr, tile in streams:
        in_specs += _halo_specs(arr.shape[0], arr.shape[1], tm, tile)
        args += [arr, arr, arr]
    in_specs += [_mod_spec(lay, l, d), _layer_spec(gains.shape, l), _layer_spec(wo.shape, slot), _layer_spec(wup.shape, l),
                 _layer_spec(cw.shape, l), _layer_spec(cb.shape, l), _layer_spec(wdn.shape, l), _resident((1, d))]
    args += [mods, gains, wo, wup, cw, cb, wdn, fgain]
    if final:
        out_specs = [pl.BlockSpec((tm, d), lambda i: (lay.p_block(i), 0)), pl.BlockSpec((tm, d), lambda i: (lay.s_block(i), 0))]
        out_shape = [jax.ShapeDtypeStruct((lay.np, d), F32), jax.ShapeDtypeStruct((lay.ns, d), F32)]
    else:
        out_specs = [pl.BlockSpec((tm, d), lambda i: (i, 0))]
        out_shape = [jax.ShapeDtypeStruct((lay.nt, d), F32)]
    return pl.pallas_call(
        functools.partial(_ffn_kernel, lay=lay, d_ff=d_ff, final=final, nx=2 if isinstance(x, tuple) else 1,
                          widths=[a_p.shape[1] for a_p, _ in pairs]),
        name="ffn",
        grid=(lay.tiles,),
        in_specs=in_specs,
        out_specs=out_specs,
        out_shape=out_shape,
        scratch_shapes=[pltpu.VMEM((FFN_AHEAD + 1, tm + 2 * SUBLANES, 2 * FFN_TF), F32), pltpu.VMEM((tm, d_ff), BF16)],
        compiler_params=_cparams(("arbitrary",)),
    )(*args)


def _rope_tables(seq_s, both_halves):
    t = jnp.arange(seq_s)
    rowp = (t // GRID_W).astype(F32)
    colp = (t % GRID_W).astype(F32)
    half = ROPE_B // 2
    inv = ROPE_BASE ** (-jnp.arange(0, half, 2, dtype=F32) / half)
    ar, ac = rowp[:, None] * inv, colp[:, None] * inv
    zero = jnp.zeros_like(ar)
    cos = jnp.concatenate([jnp.cos(ar), jnp.cos(ar), jnp.cos(ac), jnp.cos(ac)], -1)
    sin_dn = jnp.concatenate([zero, jnp.sin(ar), zero, jnp.sin(ac)], -1)
    sin_up = jnp.concatenate([-jnp.sin(ar), zero, -jnp.sin(ac), zero], -1)
    if both_halves:
        parts = [jnp.concatenate([p, p], -1) for p in (cos, sin_dn, sin_up)]
    else:
        one, zz = jnp.ones_like(cos), jnp.zeros_like(cos)
        parts = [jnp.concatenate([cos, one], -1), jnp.concatenate([sin_dn, zz], -1), jnp.concatenate([sin_up, zz], -1)]
    ident = [jnp.ones((seq_s, LANES), F32), jnp.zeros((seq_s, LANES), F32), jnp.zeros((seq_s, LANES), F32)]
    return jnp.stack([jnp.concatenate([i_, p], 0) for i_, p in zip(ident, parts)], 0)


def _even_weights(w_in, w_dec, b_dec, w_uq):
    d = w_in.shape[0]
    sizes = (H_A * DK_A, H_A * DK_A, H_A * DV_A, H_A * DV_A, GLA_LR, GLA_LR, Q_RANK, KV_RANK, ROPE_B)
    offs = np.concatenate([[0], np.cumsum(sizes)])
    qa, ka, va, ga, lrf, lrb, cq, ckv, kr = [w_in[:, offs[j]:offs[j + 1]] for j in range(9)]
    qa = qa * (DK_A ** -0.5)
    qk = jnp.concatenate([qa.reshape(d, H_A, DK_A), ka.reshape(d, H_A, DK_A)], -1).reshape(d, 2 * H_A * DK_A)
    pad = jnp.zeros((d, LANES - ROPE_B - 2 * GLA_LR), w_in.dtype)
    w = jnp.concatenate([qk, va, ga, cq, ckv, kr, lrf, lrb, pad], -1).astype(BF16)
    wd = jnp.zeros((LANES, H_A, 2, DK_A), F32)
    wd = wd.at[ROPE_B:ROPE_B + GLA_LR, :, 0, :].set(w_dec[0].reshape(GLA_LR, H_A, DK_A))
    wd = wd.at[ROPE_B + GLA_LR:ROPE_B + 2 * GLA_LR, :, 1, :].set(w_dec[1].reshape(GLA_LR, H_A, DK_A))
    wd = wd.reshape(LANES, 2 * H_A * DK_A).astype(BF16)
    bd = jnp.stack([b_dec[0].reshape(H_A, DK_A), b_dec[1].reshape(H_A, DK_A)], 1).reshape(1, 2 * H_A * DK_A)
    wq = w_uq.reshape(Q_RANK, H_B, NOPE_B + ROPE_B)
    wq = jnp.concatenate([wq, jnp.zeros((Q_RANK, H_B, 2 * LANES - NOPE_B - ROPE_B), w_uq.dtype)], -1)
    wq = wq.reshape(Q_RANK, H_B * 2 * LANES).astype(BF16)
    return w, wd, bd, wq


def kernel(x_prompt, x_sample, state_gla, cache_mla_ckv, cache_mla_krope, cache_diff_k, cache_diff_v, c, c_ctx, w_ada, b_ada, norm_gain, final_gain, w_in_even, w_out_even, gla_w_decay, gla_b_decay, gla_norm, mla_q_norm, mla_kv_norm, mla_w_uq, mla_w_ukv, w_in_odd, w_out_odd, diff_lambda, diff_norm, ffn_w_up, ffn_conv_w, ffn_conv_b, ffn_w_down):
    nb_p, seq_p, d = x_prompt.shape
    nb_s, seq_s, _ = x_sample.shape
    past = cache_mla_ckv.shape[2]
    depth = w_ada.shape[0]
    n_even, n_odd = w_in_even.shape[0], w_in_odd.shape[0]
    n_p, n_s = nb_p * seq_p, nb_s * seq_s
    lay = _Layout(n_p, n_s, seq_p, seq_s, TM)
    lay_ffn = _Layout(n_p, n_s, seq_p, seq_s, FFN_TM)
    per_tile = TM // seq_p
    assert seq_s % GRID_W == 0 and 1 + nb_s <= ADA_ROWS
    assert n_p % seq_s == 0 and seq_p & (seq_p - 1) == 0 and seq_s & (seq_s - 1) == 0
    assert seq_p % GLA_C == 0 and seq_s % GLA_C == 0

    cond = jnp.concatenate([c_ctx[None, :], c, jnp.zeros((ADA_ROWS - 1 - nb_s, d), F32)], 0)
    mods = _ada_call(cond, w_ada, b_ada).reshape(depth, ADA_ROWS, 6, d)

    tab_mla = _rope_tables(seq_s, both_halves=False)
    tab_diff = _rope_tables(seq_s, both_halves=True)
    gla_consts = _gla_consts(GLA_C)
    wup_all = ffn_w_up.astype(BF16)
    wdn_all = ffn_w_down.astype(BF16)
    cb_all = ffn_conv_b.reshape(depth, 1, -1)
    wo_even = w_out_even.astype(BF16)
    wo_odd = w_out_odd.astype(BF16)
    fgain = final_gain.reshape(1, d)
    dgains = diff_norm.reshape(n_odd, 1, 2 * DH_C)

    x = (x_prompt.reshape(n_p, d), x_sample.reshape(n_s, d))
    st_new = caches_even = caches_odd = None
    for l in range(depth):
        i = l // 2
        if l % 2 == 0:
            slot = _Slot(caches_even, i, n_even, per_tile)
            w, wd, bd, wq = _even_weights(w_in_even[i], gla_w_decay[i], gla_b_decay[i], mla_w_uq[i])
            wukv = mla_w_ukv[i].astype(BF16)
            qk, la, v, ga, qb, kra, kv, ckv_new, kr_new = _even_proj_call(
                lay, x, mods, l, norm_gain, w, wd, bd, mla_q_norm[i].reshape(1, -1), mla_kv_norm[i].reshape(1, -1),
                wq, wukv, tab_mla, slot, caches_even)
            caches_even = (ckv_new, kr_new)
            ggain = gla_norm[i].reshape(1, DV_A)
            og_p, st_new = _gla_call(qk, la, v, ga, ggain, gla_consts, None, 0, 0, nb_p, seq_p,
                                     _Slot(st_new, i, n_even), st_new)
            og_s, = _gla_call(qk, la, v, ga, ggain, gla_consts, state_gla, i, n_p, nb_s, seq_s, None, None)
            kv_ctx, kr_ctx = _kv_expand_call(cache_mla_ckv, cache_mla_krope, i, wukv)
            om_p = _mla_attn_call(qb, [(kv, kra, 0, seq_p)], 0, nb_p, seq_p)
            om_s = _mla_attn_call(qb, [(kv_ctx, kr_ctx, 0, past), (kv, kra, n_p, seq_s)], n_p, nb_s, seq_s)
            pairs, wo = [(og_p, og_s), (om_p, om_s)], wo_even
        else:
            slot = _Slot(caches_odd, i, n_odd, per_tile)
            lam_init = 0.8 - 0.6 * math.exp(-0.3 * l)
            n = H_C * 2 * DH_C
            col_scale = np.concatenate([np.full(n, DIFF_SCALE * LOG2E, np.float32), np.ones(2 * n, np.float32)])
            wi = (w_in_odd[i] * col_scale).astype(BF16)
            q, kb, vb, kf_new, vf_new = _odd_proj_call(lay, x, mods, l, norm_gain, wi, tab_diff, slot, caches_odd)
            caches_odd = (kf_new, vf_new)
            od_p = _diff_attn_call(q, diff_lambda, dgains, i, None, [(kb, vb, 0, seq_p)], 0, nb_p, seq_p, lam_init)
            od_s = _diff_attn_call(q, diff_lambda, dgains, i, (cache_diff_k, cache_diff_v), [(kb, vb, n_p, seq_s)],
                                   n_p, nb_s, seq_s, lam_init)
            pairs, wo = [(od_p, od_s)], wo_odd
        x = _ffn_call(lay_ffn, x, pairs, mods, l, norm_gain, wo, i, wup_all, ffn_conv_w, cb_all, wdn_all, fgain,
                      final=(l == depth - 1))
        if l < depth - 1:
            x = x[0]
    y_prompt = x[0].reshape(nb_p, seq_p, d)
    y_sample = x[1].reshape(nb_s, seq_s, d)
    diff_shape = (nb_p, n_odd, seq_p, H_C, 2 * DH_C)
    return (y_prompt, y_sample, st_new, caches_even[0], caches_even[1],
            caches_odd[0].reshape(diff_shape), caches_odd[1].reshape(diff_shape))
```

```python
import functools
import math

import numpy as np
import jax
import jax.numpy as jnp
from jax import lax
from jax.experimental import pallas as pl
from jax.experimental.pallas import tpu as pltpu

GRID_W = 64
ROPE_BASE = 10000.0
EPS = 1e-6
H_A, DK_A, DV_A = 4, 64, 128
GLA_LR = 16
GLA_TAU = 16.0
H_B, Q_RANK, KV_RANK, NOPE_B, ROPE_B, V_B = 4, 256, 256, 128, 64, 128
MLA_SCALE = (NOPE_B + ROPE_B) ** -0.5
H_C, DH_C = 8, 64
DIFF_SCALE = DH_C ** -0.5
LOG2E = math.log2(math.e)

LANES = 128
SUBLANES = 8
HALO = 16
VMEM_LIMIT = 56 * 1024 * 1024

TM = 512
TQ = 256
MLA_TQ = 512
ATTN_GROUP = 8
GLA_C = 64
GLA_STEPS = 1
GLA_HEADS = 4
GLA_UNROLL = 8
ADA_TN = 1536
FFN_TM = 512
FFN_TF = 256
FFN_AHEAD = 2
FFN_DOWN = 4
ADA_ROWS = 16

F32 = jnp.float32
BF16 = jnp.bfloat16


def _dot(a, b):
    return jnp.dot(a, b, preferred_element_type=F32)


def _dot_nt(a, b):
    return lax.dot_general(a, b, (((1,), (1,)), ((), ())), preferred_element_type=F32)


def _silu(x):
    return x * (1.0 / (1.0 + jnp.exp(-x)))


def _rms(x, gain):
    return x * lax.rsqrt(jnp.mean(x * x, axis=-1, keepdims=True) + EPS) * gain


def _normmod(x, gain, shift, scale):
    return _rms(x, gain) * (1.0 + scale) + shift


def _rope(x, cos, sin_dn, sin_up):
    return (x * cos + pltpu.roll(x, 16, axis=1) * sin_dn + pltpu.roll(x, LANES - 16, axis=1) * sin_up)


def _cparams(sem, vmem=VMEM_LIMIT):
    return pltpu.CompilerParams(dimension_semantics=sem, vmem_limit_bytes=vmem)


def _resident(shape):
    nd = len(shape)
    return pl.BlockSpec(shape, lambda *_: (0,) * nd, pipeline_mode=pl.Buffered(1))


def _layer_spec(shape, l):
    nd = len(shape)
    return pl.BlockSpec((1,) + tuple(shape[1:]), lambda *_: (l,) + (0,) * (nd - 1), pipeline_mode=pl.Buffered(1))


def _ada_kernel(c_ref, w_ref, b_ref, o_ref):
    s = _silu(c_ref[...]).astype(BF16)
    o_ref[0] = _dot(s, w_ref[0].astype(BF16)) + b_ref[0]


def _ada_call(cond, w_ada, b_ada):
    depth, d, n = w_ada.shape
    return pl.pallas_call(
        _ada_kernel,
        grid=(depth, n // ADA_TN),
        in_specs=[
            pl.BlockSpec((ADA_ROWS, d), lambda l, j: (0, 0)),
            pl.BlockSpec((1, d, ADA_TN), lambda l, j: (l, 0, j)),
            pl.BlockSpec((1, 1, ADA_TN), lambda l, j: (l, 0, j)),
        ],
        out_specs=pl.BlockSpec((1, ADA_ROWS, ADA_TN), lambda l, j: (l, 0, j)),
        out_shape=jax.ShapeDtypeStruct((depth, ADA_ROWS, n), F32),
        name="ada",
        compiler_params=_cparams(("arbitrary", "arbitrary")),
    )(cond, w_ada, b_ada.reshape(depth, 1, n))


class _Layout:
    def __init__(self, n_prompt, n_sample, seq_p, seq_s, tm):
        assert tm % seq_p == 0 and n_prompt % tm == 0 and n_sample % tm == 0 and seq_s % tm == 0
        self.tm = tm
        self.np, self.ns, self.nt = n_prompt, n_sample, n_prompt + n_sample
        self.seq_p, self.seq_s = seq_p, seq_s
        self.pt = n_prompt // tm
        self.st = n_sample // tm
        self.tiles = self.pt + self.st
        self.tiles_per_seq_s = seq_s // tm

    def group(self, i):
        return jnp.where(i < self.pt, 0, 1 + (i - self.pt) // self.tiles_per_seq_s)

    def pos_block(self, i):
        return jnp.where(i < self.pt, 0, self.tiles_per_seq_s + (i - self.pt) % self.tiles_per_seq_s)

    def p_block(self, i):
        return jnp.minimum(i, self.pt - 1)

    def s_block(self, i):
        return jnp.maximum(i - self.pt, 0)


def _mod_spec(lay, l, d):
    return pl.BlockSpec((1, 1, 6, d), lambda i: (l, lay.group(i), 0, 0))


def _tab_spec(lay):
    return pl.BlockSpec((3, lay.tm, LANES), lambda i: (0, lay.pos_block(i), 0))


def _x_specs(lay, x):
    if isinstance(x, tuple):
        d = x[0].shape[1]
        return [pl.BlockSpec((lay.tm, d), lambda i: (lay.p_block(i), 0)),
                pl.BlockSpec((lay.tm, d), lambda i: (lay.s_block(i), 0))], list(x)
    return [pl.BlockSpec((lay.tm, x.shape[1]), lambda i: (i, 0))], [x]


def _read_x(x_refs, prompt_tiles):
    if len(x_refs) == 1:
        return x_refs[0][...]
    return jnp.where(pl.program_id(0) < prompt_tiles, x_refs[0][...], x_refs[1][...])


class _Slot:
    def __init__(self, prev, slot, n_slots, per_block=1):
        self.prev, self.slot, self.n_slots, self.per_block = prev, slot, n_slots, per_block

    @property
    def first(self):
        return self.prev is None

    def spec(self, tail, index):
        zeros = (0,) * len(tail)
        if self.first:
            return pl.BlockSpec((self.per_block, self.n_slots) + tail, lambda *g: (index(*g), 0) + zeros)
        s = self.slot
        return pl.BlockSpec((self.per_block, 1) + tail, lambda *g: (index(*g), s) + zeros)

    def shape(self, nb, tail):
        return jax.ShapeDtypeStruct((nb, self.n_slots) + tail, F32)

    def at(self):
        return self.slot if self.first else 0

    def zero_others(self, ref):
        if self.first:
            for b in range(self.per_block):
                for s in range(self.n_slots):
                    if s != self.slot:
                        ref[b, s] = jnp.zeros(ref.shape[2:], F32)


def _even_proj_kernel(*refs, nx, prompt_tiles, slot):
    x_refs = refs[:nx]
    (mod_ref, gain_ref, w_ref, wd_ref, bd_ref, qn_ref, kvn_ref, wuq_ref, wukv_ref, tab_ref) = refs[nx:nx + 10]
    outs = refs[len(refs) - 9:]
    qk_ref, la_ref, v_ref, ga_ref, qb_ref, kra_ref, kv_ref, ckv_ref, kr_ref = outs
    mod = mod_ref[0, 0]
    seq = ckv_ref.shape[2]
    parts = [slice(b * seq, (b + 1) * seq) for b in range(slot.per_block)]
    x = _read_x(x_refs, prompt_tiles)
    ys = [_dot(_normmod(x[r], gain_ref[0, 0:1], mod[0:1], mod[1:2]).astype(BF16), w_ref[...]) for r in parts]
    ckvns, gs = [], []
    for r, y in zip(parts, ys):
        qk_ref[r, :] = y[:, 0:512]
        v_ref[r, :] = y[:, 512:1024].astype(BF16)
        ga_ref[r, :] = y[:, 1024:1536]
        cq, ckv, g = y[:, 1536:1792], y[:, 1792:2048], y[:, 2048:2176]
        pre = _dot(g.astype(BF16), wd_ref[...]) + bd_ref[...]
        la_ref[r, :] = (jnp.minimum(pre, 0.0) - jnp.log(1.0 + jnp.exp(-jnp.abs(pre)))) * (LOG2E / GLA_TAU)
        cos, sin_dn, sin_up = tab_ref[0, r, :], tab_ref[1, r, :], tab_ref[2, r, :]
        qb = _dot(_rms(cq, qn_ref[...]).astype(BF16), wuq_ref[...])
        for hh in range(H_B):
            lo = hh * 2 * LANES
            qb_ref[r, lo:lo + LANES] = qb[:, lo:lo + LANES].astype(BF16)
            qb_ref[r, lo + LANES:lo + 2 * LANES] = _rope(qb[:, lo + LANES:lo + 2 * LANES], cos, sin_dn, sin_up).astype(BF16)
        ckvn = _rms(ckv, kvn_ref[...])
        kv_ref[r, :] = _dot(ckvn.astype(BF16), wukv_ref[...]).astype(BF16)
        lane = lax.broadcasted_iota(jnp.int32, g.shape, 1)
        kra_ref[r, :] = jnp.where(lane < ROPE_B, _rope(g, cos, sin_dn, sin_up), 0.0).astype(BF16)
        ckvns.append(ckvn)
        gs.append(g)

    @pl.when(pl.program_id(0) < prompt_tiles)
    def _():
        for b in range(slot.per_block):
            ckv_ref[b, slot.at()] = ckvns[b]
            kr_ref[b, slot.at()] = gs[b][:, :ROPE_B]
        slot.zero_others(ckv_ref)
        slot.zero_others(kr_ref)


def _even_proj_call(lay, x, mods, l, gains, w, wd, bd, qn, kvn, wuq, wukv, tab, slot, prev):
    d = w.shape[0]
    x_specs, x_args = _x_specs(lay, x)
    row = lambda n: pl.BlockSpec((TM, n), lambda i: (i, 0))
    outs = [(512, F32), (512, F32), (512, BF16), (512, F32), (1024, BF16), (LANES, BF16), (1024, BF16)]
    in_specs = x_specs + [_mod_spec(lay, l, d), _layer_spec(gains.shape, l), _resident(w.shape), _resident(wd.shape),
                          _resident(bd.shape), _resident(qn.shape), _resident(kvn.shape), _resident(wuq.shape),
                          _resident(wukv.shape), _tab_spec(lay)]
    args = x_args + [mods, gains, w, wd, bd, qn, kvn, wuq, wukv, tab]
    aliases = {}
    if not slot.first:
        aliases = {len(args): len(outs), len(args) + 1: len(outs) + 1}
        in_specs += [pl.BlockSpec(memory_space=pl.ANY), pl.BlockSpec(memory_space=pl.ANY)]
        args += list(prev)
    nb_p = lay.np // lay.seq_p
    return pl.pallas_call(
        functools.partial(_even_proj_kernel, nx=len(x_args), prompt_tiles=lay.pt, slot=slot),
        grid=(lay.tiles,),
        in_specs=in_specs,
        out_specs=[row(n) for n, _ in outs]
        + [slot.spec((lay.seq_p, KV_RANK), lay.p_block), slot.spec((lay.seq_p, ROPE_B), lay.p_block)],
        out_shape=[jax.ShapeDtypeStruct((lay.nt, n), dt) for n, dt in outs]
        + [slot.shape(nb_p, (lay.seq_p, KV_RANK)), slot.shape(nb_p, (lay.seq_p, ROPE_B))],
        input_output_aliases=aliases,
        name="even_proj",
        compiler_params=_cparams(("arbitrary",)),
    )(*args)


def _gla_consts(c):
    levels = int(math.log2(c))
    idx = np.arange(c)
    cum = idx[None, :] <= idx[:, None]
    rows, sgn, masks = [cum], [], []
    for lev in range(levels):
        half = 1 << lev
        node = idx // (2 * half)
        mid = node * 2 * half + half
        upper = idx >= mid
        rows.append(cum[mid - 1])
        sgn.append(np.repeat(np.where(upper, 1.0, -1.0)[:, None], LANES, 1))
        same = node[:, None] == node[None, :]
        masks.append(np.concatenate([same & upper[:, None] & ~upper[None, :],
                                     same & ~upper[:, None] & upper[None, :]], 1))
    eye = np.eye(c, dtype=bool)
    masks.append(np.concatenate([eye, eye], 1))
    m = np.concatenate(rows, 0)
    return (jnp.asarray(np.concatenate([m, m, m], 1), BF16), jnp.asarray(np.stack(sgn, 0), F32),
            jnp.asarray(np.stack(masks, 0), F32), levels)


def _gla_kernel(*refs, t, has_s0, slot, levels):
    qk_ref, la_ref, v_ref, ga_ref, gain_ref, m3_ref, sg_ref, mask_ref = refs[:8]
    s0_ref = refs[8] if has_s0 else None
    n_out = 1 if slot is None else 2
    o_ref = refs[len(refs) - 3 - n_out]
    st_ref = None if slot is None else refs[len(refs) - 4]
    of_ref, ob_ref, st_scr = refs[len(refs) - 3:]
    c = GLA_C
    n = t // c
    heads = range(H_A)
    lane = lax.broadcasted_iota(jnp.int32, (1, LANES), 1)
    is_fwd = lane < DK_A
    keep_f = jnp.where(is_fwd, 1.0, 0.0).astype(BF16)
    keep_b = jnp.where(is_fwd, 0.0, 1.0).astype(BF16)

    def split(x):
        return jnp.concatenate([x * keep_f, x * keep_b], axis=0)

    for h in heads:
        if has_s0:
            st_scr[h] = jnp.concatenate([jnp.transpose(s0_ref[0, 0, 0, h]), jnp.transpose(s0_ref[0, 0, 1, h])], axis=1)
        else:
            st_scr[h] = jnp.zeros((DV_A, LANES), F32)

    steps = GLA_STEPS if n % GLA_STEPS == 0 else 1

    def body(j, carry):
        rf = [pl.multiple_of((j * steps + s) * c, c) for s in range(steps)]
        rb = [pl.multiple_of((n - 1 - j * steps - s) * c, c) for s in range(steps)]
        sl = [slice(h * LANES, (h + 1) * LANES) for h in heads]
        for g in range(0, H_A, GLA_HEADS):
            process([(s, h) for s in range(steps) for h in heads[g:g + GLA_HEADS]], rf, rb, sl)
        return carry

    def process(units, rf, rb, sl):
        qq, kk, ll, split3 = [], [], [], []
        for s, h in units:
            qkf, qkb = qk_ref[pl.ds(rf[s], c), sl[h]], qk_ref[pl.ds(rb[s], c), sl[h]]
            qq.append(jnp.where(is_fwd, qkf, pltpu.roll(qkb, DK_A, axis=1)))
            kk.append(jnp.where(is_fwd, pltpu.roll(qkf, DK_A, axis=1), qkb))
            ll.append(jnp.where(is_fwd, la_ref[pl.ds(rf[s], c), sl[h]], la_ref[pl.ds(rb[s], c), sl[h]]))
            hi = ll[-1].astype(BF16)
            r1 = ll[-1] - hi.astype(F32)
            mid = r1.astype(BF16)
            lo = (r1 - mid.astype(F32)).astype(BF16)
            split3.append(jnp.concatenate([hi, mid, lo], axis=0))
        e_all = _dot(m3_ref[...], jnp.concatenate(split3, axis=1))
        e = [e_all[:, k * LANES:(k + 1) * LANES] for k in range(len(units))]
        cx = [jnp.where(is_fwd, e[k][0:c], e[k][0:c] - ll[k]) for k in range(len(units))]
        a = [_dot_nt(qq[k].astype(BF16), split(kk[k].astype(BF16))) * mask_ref[levels] for k in range(len(units))]
        for lev in range(levels):
            for k in range(len(units)):
                ex = jnp.exp2(sg_ref[lev] * (cx[k] - e[k][(lev + 1) * c:(lev + 2) * c]))
                a[k] = a[k] + _dot_nt((qq[k] * ex).astype(BF16), split((kk[k] * ex).astype(BF16))) * mask_ref[lev]
        for k, (s, h) in enumerate(units):
            tot = e[k][c - 1:c]
            u = jnp.exp2(cx[k])
            w = jnp.exp2(tot - cx[k])
            qd = (qq[k] * jnp.where(is_fwd, u, w)).astype(BF16)
            kd = (kk[k] * jnp.where(is_fwd, w, u)).astype(BF16)
            vs = jnp.concatenate([v_ref[pl.ds(rf[s], c), sl[h]], v_ref[pl.ds(rb[s], c), sl[h]]], axis=0)
            st = st_scr[h]
            o = _dot(split(a[k].astype(BF16)), vs) + _dot_nt(split(qd), st.astype(BF16))
            of_ref[pl.ds(rf[s], c), sl[h]] = o[:c]
            ob_ref[pl.ds(rb[s], c), sl[h]] = o[c:]
            vt = jnp.transpose(vs.astype(F32)).astype(BF16)
            st_scr[h] = st * jnp.exp2(tot) + _dot(vt, split(kd))

    lax.fori_loop(0, n // steps, body, 0, unroll=min(GLA_UNROLL, n // steps))
    for h in heads:
        sl = slice(h * LANES, (h + 1) * LANES)
        o = _rms(of_ref[:, sl] + ob_ref[:, sl], gain_ref[...]) * _silu(ga_ref[:, sl])
        o_ref[:, sl] = o.astype(BF16)
        if slot is not None:
            st = st_scr[h]
            st_ref[0, slot.at(), 0, h] = jnp.transpose(st[:, :DK_A])
            st_ref[0, slot.at(), 1, h] = jnp.transpose(st[:, DK_A:])
    if slot is not None:
        slot.zero_others(st_ref)


def _gla_call(qk, la, v, ga, gain, consts, s0, s0_slot, row0, nb, t, slot, prev):
    m3, sg, mask, levels = consts
    blk0 = row0 // t
    width = H_A * LANES
    tok = pl.BlockSpec((t, width), lambda b: (blk0 + b, 0))
    in_specs = [tok, tok, tok, tok, _resident(gain.shape), _resident(m3.shape), _resident(sg.shape), _resident(mask.shape)]
    args = [qk, la, v, ga, gain, m3, sg, mask]
    tail = (2, H_A, DK_A, DV_A)
    if s0 is not None:
        in_specs.append(pl.BlockSpec((1, 1) + tail, lambda b: (b, s0_slot, 0, 0, 0, 0)))
        args.append(s0)
    out_specs = [pl.BlockSpec((t, width), lambda b: (b, 0))]
    out_shape = [jax.ShapeDtypeStruct((nb * t, width), BF16)]
    aliases = {}
    if slot is not None:
        out_specs.append(slot.spec(tail, lambda b: b))
        out_shape.append(slot.shape(nb, tail))
        if not slot.first:
            aliases = {len(args): 1}
            in_specs.append(pl.BlockSpec(memory_space=pl.ANY))
            args.append(prev)
    return pl.pallas_call(
        functools.partial(_gla_kernel, t=t, has_s0=s0 is not None, slot=slot, levels=levels),
        grid=(nb,),
        in_specs=in_specs,
        out_specs=out_specs,
        out_shape=out_shape,
        scratch_shapes=[pltpu.VMEM((t, width), F32), pltpu.VMEM((t, width), F32), pltpu.VMEM((H_A, DV_A, LANES), F32)],
        input_output_aliases=aliases,
        name="gla",
        compiler_params=_cparams(("arbitrary",)),
    )(*args)


def _kv_expand_kernel(c_ref, r_ref, w_ref, o_ref, ro_ref):
    o_ref[...] = _dot(c_ref[0, 0].astype(BF16), w_ref[...]).astype(BF16)
    kr = r_ref[0, 0]
    ro_ref[...] = jnp.concatenate([kr, jnp.zeros((kr.shape[0], LANES - ROPE_B), F32)], axis=1).astype(BF16)


def _kv_expand_call(cache_ckv, cache_kr, slot, wukv):
    nb, _, past, r = cache_ckv.shape
    return pl.pallas_call(
        _kv_expand_kernel,
        grid=(nb,),
        in_specs=[pl.BlockSpec((1, 1, past, r), lambda b: (b, slot, 0, 0)),
                  pl.BlockSpec((1, 1, past, ROPE_B), lambda b: (b, slot, 0, 0)), _resident(wukv.shape)],
        out_specs=[pl.BlockSpec((past, wukv.shape[1]), lambda b: (b, 0)), pl.BlockSpec((past, LANES), lambda b: (b, 0))],
        out_shape=[jax.ShapeDtypeStruct((nb * past, wukv.shape[1]), BF16), jax.ShapeDtypeStruct((nb * past, LANES), BF16)],
        name="kv_expand",
        compiler_params=_cparams(("arbitrary",)),
    )(cache_ckv, cache_kr, wukv)


def _attn_group(nb, per, row0, t, seg_blocks):
    g = ATTN_GROUP
    if per == 1 and nb % g == 0 and (row0 // t) % g == 0 and all(b0 % g == 0 for b0 in seg_blocks):
        return g
    return 1


def _mla_attn_kernel(*refs, nseg, group):
    q_ref = refs[0]
    seg = [(refs[1 + 2 * s], refs[2 + 2 * s]) for s in range(nseg)]
    o_ref = refs[1 + 2 * nseg]
    tq = q_ref.shape[0] // group
    units = [(g, hh) for g in range(group) for hh in range(H_B)]

    def keys(ref, g):
        s = ref.shape[0] // group
        return slice(g * s, (g + 1) * s)

    def qk(u):
        g, hh = units[u]
        q = q_ref[g * tq:(g + 1) * tq, hh * 2 * LANES:(hh + 1) * 2 * LANES]
        scores = []
        for kv_ref, kr_ref in seg:
            kcat = jnp.concatenate([kv_ref[keys(kv_ref, g), hh * 2 * LANES:hh * 2 * LANES + LANES],
                                    kr_ref[keys(kr_ref, g), :]], axis=1)
            scores.append(_dot_nt(q, kcat) * (MLA_SCALE * LOG2E))
        return scores

    nxt = qk(0)
    for u, (g, hh) in enumerate(units):
        scores = nxt
        if u + 1 < len(units):
            nxt = qk(u + 1)
        mx = functools.reduce(jnp.maximum, [jnp.max(s, axis=-1, keepdims=True) for s in scores])
        ps = [jnp.exp2(s - mx) for s in scores]
        den = functools.reduce(lambda a, b: a + b, [jnp.sum(p, axis=-1, keepdims=True) for p in ps])
        acc = functools.reduce(lambda a, b: a + b, [
            _dot(p.astype(BF16), kv_ref[keys(kv_ref, g), hh * 2 * LANES + LANES:(hh + 1) * 2 * LANES])
            for p, (kv_ref, _) in zip(ps, seg)])
        o_ref[g * tq:(g + 1) * tq, hh * LANES:(hh + 1) * LANES] = (acc * (1.0 / den)).astype(BF16)


def _mla_attn_call(qb, segs, row0, nb, t):
    tq = min(t, MLA_TQ)
    assert t % tq == 0 and row0 % tq == 0
    per = t // tq
    group = _attn_group(nb, per, row0, t, [r0 // s for _, _, r0, s in segs])
    tq, nb = tq * group, nb // group
    q0 = row0 // tq
    in_specs = [pl.BlockSpec((tq, qb.shape[1]), lambda b, j: (q0 + b * per + j, 0))]
    args = [qb]
    for kv, kr, r0, s in segs:
        b0 = r0 // (s * group)
        in_specs.append(pl.BlockSpec((s * group, kv.shape[1]), lambda b, j, b0=b0: (b0 + b, 0)))
        in_specs.append(pl.BlockSpec((s * group, LANES), lambda b, j, b0=b0: (b0 + b, 0)))
        args += [kv, kr]
    return pl.pallas_call(
        functools.partial(_mla_attn_kernel, nseg=len(segs), group=group),
        grid=(nb, per),
        in_specs=in_specs,
        out_specs=pl.BlockSpec((tq, H_B * V_B), lambda b, j: (b * per + j, 0)),
        out_shape=jax.ShapeDtypeStruct((nb * per * tq, H_B * V_B), BF16),
        name="mla_attn",
        compiler_params=_cparams(("arbitrary", "arbitrary")),
    )(*args)


def _odd_proj_kernel(*refs, prompt_tiles, slot):
    x_ref, mod_ref, gain_ref, w_ref, tab_ref = refs[:5]
    q_ref, kb_ref, vb_ref, kf_ref, vf_ref = refs[len(refs) - 5:]
    mod = mod_ref[0, 0]
    n = q_ref.shape[1]
    heads = n // LANES
    seq = kf_ref.shape[2] // heads
    parts = [slice(b * seq, (b + 1) * seq) for b in range(slot.per_block)]
    x = x_ref[...]
    ys = [_dot(_normmod(x[r], gain_ref[0, 0:1], mod[0:1], mod[1:2]).astype(BF16), w_ref[...]) for r in parts]
    for r, y in zip(parts, ys):
        cos, sin_dn, sin_up = tab_ref[0, r, :], tab_ref[1, r, :], tab_ref[2, r, :]
        vb_ref[r, :] = y[:, 2 * n:3 * n].astype(BF16)
        for g in range(heads):
            lo = g * LANES
            q_ref[r, lo:lo + LANES] = _rope(y[:, lo:lo + LANES], cos, sin_dn, sin_up).astype(BF16)
            kb_ref[r, lo:lo + LANES] = _rope(y[:, n + lo:n + lo + LANES], cos, sin_dn, sin_up).astype(BF16)

    @pl.when(pl.program_id(0) < prompt_tiles)
    def _():
        for b in range(slot.per_block):
            for g in range(heads):
                lo = g * LANES
                kf_ref[b, slot.at(), pl.ds(g, seq, stride=heads), :] = ys[b][:, n + lo:n + lo + LANES]
                vf_ref[b, slot.at(), pl.ds(g, seq, stride=heads), :] = ys[b][:, 2 * n + lo:2 * n + lo + LANES]
        slot.zero_others(kf_ref)
        slot.zero_others(vf_ref)


def _odd_proj_call(lay, x, mods, l, gains, w, tab, slot, prev):
    d = x.shape[1]
    n = w.shape[1] // 3
    row = lambda m: pl.BlockSpec((TM, m), lambda i: (i, 0))
    in_specs = [row(d), _mod_spec(lay, l, d), _layer_spec(gains.shape, l), _resident(w.shape), _tab_spec(lay)]
    args = [x, mods, gains, w, tab]
    aliases = {}
    if not slot.first:
        aliases = {len(args): 3, len(args) + 1: 4}
        in_specs += [pl.BlockSpec(memory_space=pl.ANY), pl.BlockSpec(memory_space=pl.ANY)]
        args += list(prev)
    tail = (lay.seq_p * H_C, 2 * DH_C)
    nb_p = lay.np // lay.seq_p
    return pl.pallas_call(
        functools.partial(_odd_proj_kernel, prompt_tiles=lay.pt, slot=slot),
        grid=(lay.tiles,),
        in_specs=in_specs,
        out_specs=[row(n), row(n), row(n), slot.spec(tail, lay.p_block), slot.spec(tail, lay.p_block)],
        out_shape=[jax.ShapeDtypeStruct((lay.nt, n), BF16)] * 3 + [slot.shape(nb_p, tail)] * 2,
        input_output_aliases=aliases,
        name="odd_proj",
        compiler_params=_cparams(("arbitrary",)),
    )(*args)


def _diff_attn_kernel(*refs, nseg, has_ctx, lam_init):
    q_ref, dl_ref, gain_ref = refs[0], refs[1], refs[2]
    seg = [(refs[3 + 2 * s], refs[4 + 2 * s]) for s in range(nseg)]
    o_ref = refs[3 + 2 * nseg]
    if has_ctx:
        (kc_ref, vc_ref), (kc_scr, vc_scr) = seg[0], refs[4 + 2 * nseg:6 + 2 * nseg]

        @pl.when(pl.program_id(1) == 0)
        def _():
            for hh in range(H_C):
                rows = pl.ds(hh, kc_scr.shape[0], stride=H_C)
                kc_scr[:, hh * LANES:(hh + 1) * LANES] = kc_ref[0, 0, rows, :].astype(BF16)
                vc_scr[:, hh * LANES:(hh + 1) * LANES] = vc_ref[0, 0, rows, :].astype(BF16)

        seg[0] = (kc_scr, vc_scr)
    dl = dl_ref[0]
    lam = (jnp.exp(jnp.sum(dl[0:1] * dl[1:2], axis=-1, keepdims=True))
           - jnp.exp(jnp.sum(dl[2:3] * dl[3:4], axis=-1, keepdims=True)) + lam_init)
    tq = q_ref.shape[0]
    lane = lax.broadcasted_iota(jnp.int32, (1, LANES), 1)
    keep1 = jnp.where(lane < DH_C, 1.0, 0.0).astype(BF16)
    keep2 = jnp.where(lane < DH_C, 0.0, 1.0).astype(BF16)
    def qk(hh):
        q = q_ref[:, hh * LANES:(hh + 1) * LANES]
        qq = jnp.concatenate([q * keep1, q * keep2], axis=0)
        return [_dot_nt(qq, k_ref[:, hh * LANES:(hh + 1) * LANES]) for k_ref, _ in seg]

    nxt = qk(0)
    for hh in range(H_C):
        scores = nxt
        if hh + 1 < H_C:
            nxt = qk(hh + 1)
        mx = functools.reduce(jnp.maximum, [jnp.max(s, axis=-1, keepdims=True) for s in scores])
        ps = [jnp.exp2(s - mx) for s in scores]
        both = functools.reduce(lambda a, b: a + b, [
            _dot(p.astype(BF16), jnp.concatenate([v_ref[:, hh * LANES:(hh + 1) * LANES],
                                                  jnp.ones((v_ref.shape[0], LANES), BF16)], axis=1))
            for p, (_, v_ref) in zip(ps, seg)])
        acc = both[:, :LANES] * (1.0 / both[:, LANES:])
        o = acc[:tq] - lam * acc[tq:]
        o_ref[:, hh * LANES:(hh + 1) * LANES] = (_rms(o, gain_ref[0]) * (1.0 - lam_init)).astype(BF16)


def _diff_attn_call(q, dls, gains, slot, ctx, segs, row0, nb, t, lam_init):
    tq = min(t, TQ)
    assert t % tq == 0 and row0 % tq == 0
    q0 = row0 // tq
    per = t // tq
    n = q.shape[1]
    in_specs = [pl.BlockSpec((tq, n), lambda b, j: (q0 + b * per + j, 0)),
                pl.BlockSpec((1,) + dls.shape[1:], lambda b, j: (slot, 0, 0)),
                pl.BlockSpec((1, 1, gains.shape[2]), lambda b, j: (slot, 0, 0))]
    args = [q, dls, gains]
    scratch = []
    if ctx is not None:
        nb_c, n_l, past, heads, dh = ctx[0].shape
        for cache in ctx:
            in_specs.append(pl.BlockSpec((1, 1, past * heads, dh), lambda b, j: (b, slot, 0, 0)))
            args.append(cache.reshape(nb_c, n_l, past * heads, dh))
        scratch = [pltpu.VMEM((past, n), BF16), pltpu.VMEM((past, n), BF16)]
    for k, v, r0, s in segs:
        b0 = r0 // s
        in_specs.append(pl.BlockSpec((s, n), lambda b, j, b0=b0: (b0 + b, 0)))
        in_specs.append(pl.BlockSpec((s, n), lambda b, j, b0=b0: (b0 + b, 0)))
        args += [k, v]
    return pl.pallas_call(
        functools.partial(_diff_attn_kernel, nseg=len(segs) + (ctx is not None), has_ctx=ctx is not None,
                          lam_init=lam_init),
        grid=(nb, per),
        in_specs=in_specs,
        out_specs=pl.BlockSpec((tq, n), lambda b, j: (b * per + j, 0)),
        out_shape=jax.ShapeDtypeStruct((nb * t, n), BF16),
        scratch_shapes=scratch,
        name="diff_attn",
        compiler_params=_cparams(("arbitrary", "arbitrary")),
    )(*args)


def _halo_specs(rows, width, tm, tile):
    per = tm // HALO
    last = rows // HALO - 1
    return [pl.BlockSpec((HALO, width), lambda i: (jnp.maximum(tile(i) * per - 1, 0), 0)),
            pl.BlockSpec((tm, width), lambda i: (tile(i), 0)),
            pl.BlockSpec((HALO, width), lambda i: (jnp.minimum((tile(i) + 1) * per, last), 0))]


def _ffn_kernel(*refs, lay, d_ff, final, nx, widths):
    pos = 3 * nx + 6 * len(widths)
    trips = [refs[3 * t:3 * t + 3] for t in range(pos // 3)]
    mod_ref, gain_ref, wo_ref, wup_ref, cw_ref, cb_ref, wdn_ref, fg_ref = refs[pos:pos + 8]
    outs = refs[pos + 8:len(refs) - 2]
    u_ref, act_ref = refs[len(refs) - 2:]
    tm = lay.tm
    i = pl.program_id(0)
    row = i * tm
    is_prompt = i < lay.pt
    seq = jnp.where(is_prompt, lay.seq_p, lay.seq_s)
    first = (row & (seq - 1)) == 0
    last = ((row + tm) & (seq - 1)) == 0

    def ext(trip_p, trip_s):
        if trip_s is None:
            return jnp.concatenate([r[...] for r in trip_p], axis=0)
        return jnp.concatenate([jnp.where(is_prompt, a[...], b[...]) for a, b in zip(trip_p, trip_s)], axis=0)

    mod = mod_ref[0, 0]
    x_ext = ext(trips[0], trips[1] if nx == 2 else None)
    proj = None
    k0 = 0
    for p, w in enumerate(widths):
        part = _dot(ext(trips[nx + 2 * p], trips[nx + 2 * p + 1]), wo_ref[0, k0:k0 + w, :])
        k0 += w
        proj = part if proj is None else proj + part
    xm = x_ext + mod[2:3] * proj
    gain, shift, scale = gain_ref[0, 1:2], mod[3:4], mod[4:5]
    hh = _normmod(xm, gain, shift, scale)
    h = jnp.concatenate([jnp.where(first, 0.0, hh[HALO - SUBLANES:HALO]), hh[HALO:HALO + tm],
                         jnp.where(last, 0.0, hh[HALO + tm:HALO + tm + SUBLANES])], axis=0).astype(BF16)
    x = xm[HALO:HALO + tm]
    acc = jnp.zeros((tm, x.shape[1]), F32)
    n_f = d_ff // FFN_TF
    n_slots = u_ref.shape[0]
    sub = lax.broadcasted_iota(jnp.int32, (SUBLANES, FFN_TF), 0)
    cut_prev = is_prompt & (sub == 0)
    cut_next = is_prompt & (sub == SUBLANES - 1)
    seams = range(lay.seq_p, tm, lay.seq_p)

    def up(f):
        for half, base in enumerate((0, d_ff)):
            lo = base + f * FFN_TF
            u_ref[f % n_slots, :, half * FFN_TF:(half + 1) * FFN_TF] = _dot(h, wup_ref[0, :, lo:lo + FFN_TF])

    def conv(f, half):
        lo = half * d_ff + f * FFN_TF
        cols = slice(half * FFN_TF, (half + 1) * FFN_TF)
        cw = cw_ref[0, :, lo:lo + FFN_TF]
        prev = u_ref[f % n_slots, SUBLANES - 1:SUBLANES - 1 + tm, cols]
        here = u_ref[f % n_slots, SUBLANES:SUBLANES + tm, cols]
        nxt = u_ref[f % n_slots, SUBLANES + 1:SUBLANES + 1 + tm, cols]
        for r in seams:
            prev = jnp.concatenate([prev[:r], jnp.where(cut_prev, 0.0, prev[r:r + SUBLANES]), prev[r + SUBLANES:]], axis=0)
            nxt = jnp.concatenate([nxt[:r - SUBLANES], jnp.where(cut_next, 0.0, nxt[r - SUBLANES:r]), nxt[r:]], axis=0)
        return prev * cw[0:1] + here * cw[1:2] + nxt * cw[2:3] + cb_ref[0, :, lo:lo + FFN_TF]

    for f in range(min(FFN_AHEAD, n_f)):
        up(f)
    for f in range(n_f):
        act_ref[:, f * FFN_TF:(f + 1) * FFN_TF] = (_silu(conv(f, 0)) * conv(f, 1)).astype(BF16)
        if f + FFN_AHEAD < n_f:
            up(f + FFN_AHEAD)
        if (f + 1) % FFN_DOWN == 0 or f + 1 == n_f:
            k0 = (f // FFN_DOWN) * FFN_DOWN * FFN_TF
            acc = acc + _dot(act_ref[:, k0:(f + 1) * FFN_TF], wdn_ref[0, k0:(f + 1) * FFN_TF, :])
    y = x + mod[5:6] * acc
    if not final:
        outs[0][...] = y
    else:
        yn = _rms(y, fg_ref[...])

        @pl.when(i < lay.pt)
        def _():
            outs[0][...] = yn

        @pl.when(i >= lay.pt)
        def _():
            outs[1][...] = yn


def _ffn_call(lay, x, pairs, mods, l, gains, wo, slot, wup, cw, cb, wdn, fgain, final):
    d = wo.shape[2]
    d_ff = wdn.shape[1]
    assert d_ff % FFN_TF == 0
    tm = lay.tm
    in_specs, args = [], []
    if isinstance(x, tuple):
        streams = [(x[0], lay.p_block), (x[1], lay.s_block)]
    else:
        streams = [(x, lambda i: i)]
    for a_p, a_s in pairs:
        streams += [(a_p, lay.p_block), (a_s, lay.s_block)]
    for arr, tile in streams:
        in_specs += _halo_specs(arr.shape[0], arr.shape[1], tm, tile)
        args += [arr, arr, arr]
    in_specs += [_mod_spec(lay, l, d), _layer_spec(gains.shape, l), _layer_spec(wo.shape, slot), _layer_spec(wup.shape, l),
                 _layer_spec(cw.shape, l), _layer_spec(cb.shape, l), _layer_spec(wdn.shape, l), _resident((1, d))]
    args += [mods, gains, wo, wup, cw, cb, wdn, fgain]
    if final:
        out_specs = [pl.BlockSpec((tm, d), lambda i: (lay.p_block(i), 0)), pl.BlockSpec((tm, d), lambda i: (lay.s_block(i), 0))]
        out_shape = [jax.ShapeDtypeStruct((lay.np, d), F32), jax.ShapeDtypeStruct((lay.ns, d), F32)]
    else:
        out_specs = [pl.BlockSpec((tm, d), lambda i: (i, 0))]
        out_shape = [jax.ShapeDtypeStruct((lay.nt, d), F32)]
    return pl.pallas_call(
        functools.partial(_ffn_kernel, lay=lay, d_ff=d_ff, final=final, nx=2 if isinstance(x, tuple) else 1,
                          widths=[a_p.shape[1] for a_p, _ in pairs]),
        name="ffn",
        grid=(lay.tiles,),
        in_specs=in_specs,
        out_specs=out_specs,
        out_shape=out_shape,
        scratch_shapes=[pltpu.VMEM((FFN_AHEAD + 1, tm + 2 * SUBLANES, 2 * FFN_TF), F32), pltpu.VMEM((tm, d_ff), BF16)],
        compiler_params=_cparams(("arbitrary",)),
    )(*args)


def _rope_tables(seq_s, both_halves):
    t = jnp.arange(seq_s)
    rowp = (t // GRID_W).astype(F32)
    colp = (t % GRID_W).astype(F32)
    half = ROPE_B // 2
    inv = ROPE_BASE ** (-jnp.arange(0, half, 2, dtype=F32) / half)
    ar, ac = rowp[:, None] * inv, colp[:, None] * inv
    zero = jnp.zeros_like(ar)
    cos = jnp.concatenate([jnp.cos(ar), jnp.cos(ar), jnp.cos(ac), jnp.cos(ac)], -1)
    sin_dn = jnp.concatenate([zero, jnp.sin(ar), zero, jnp.sin(ac)], -1)
    sin_up = jnp.concatenate([-jnp.sin(ar), zero, -jnp.sin(ac), zero], -1)
    if both_halves:
        parts = [jnp.concatenate([p, p], -1) for p in (cos, sin_dn, sin_up)]
    else:
        one, zz = jnp.ones_like(cos), jnp.zeros_like(cos)
        parts = [jnp.concatenate([cos, one], -1), jnp.concatenate([sin_dn, zz], -1), jnp.concatenate([sin_up, zz], -1)]
    ident = [jnp.ones((seq_s, LANES), F32), jnp.zeros((seq_s, LANES), F32), jnp.zeros((seq_s, LANES), F32)]
    return jnp.stack([jnp.concatenate([i_, p], 0) for i_, p in zip(ident, parts)], 0)


def _even_weights(w_in, w_dec, b_dec, w_uq):
    d = w_in.shape[0]
    sizes = (H_A * DK_A, H_A * DK_A, H_A * DV_A, H_A * DV_A, GLA_LR, GLA_LR, Q_RANK, KV_RANK, ROPE_B)
    offs = np.concatenate([[0], np.cumsum(sizes)])
    qa, ka, va, ga, lrf, lrb, cq, ckv, kr = [w_in[:, offs[j]:offs[j + 1]] for j in range(9)]
    qa = qa * (DK_A ** -0.5)
    qk = jnp.concatenate([qa.reshape(d, H_A, DK_A), ka.reshape(d, H_A, DK_A)], -1).reshape(d, 2 * H_A * DK_A)
    pad = jnp.zeros((d, LANES - ROPE_B - 2 * GLA_LR), w_in.dtype)
    w = jnp.concatenate([qk, va, ga, cq, ckv, kr, lrf, lrb, pad], -1).astype(BF16)
    wd = jnp.zeros((LANES, H_A, 2, DK_A), F32)
    wd = wd.at[ROPE_B:ROPE_B + GLA_LR, :, 0, :].set(w_dec[0].reshape(GLA_LR, H_A, DK_A))
    wd = wd.at[ROPE_B + GLA_LR:ROPE_B + 2 * GLA_LR, :, 1, :].set(w_dec[1].reshape(GLA_LR, H_A, DK_A))
    wd = wd.reshape(LANES, 2 * H_A * DK_A).astype(BF16)
    bd = jnp.stack([b_dec[0].reshape(H_A, DK_A), b_dec[1].reshape(H_A, DK_A)], 1).reshape(1, 2 * H_A * DK_A)
    wq = w_uq.reshape(Q_RANK, H_B, NOPE_B + ROPE_B)
    wq = jnp.concatenate([wq, jnp.zeros((Q_RANK, H_B, 2 * LANES - NOPE_B - ROPE_B), w_uq.dtype)], -1)
    wq = wq.reshape(Q_RANK, H_B * 2 * LANES).astype(BF16)
    return w, wd, bd, wq


def kernel(x_prompt, x_sample, state_gla, cache_mla_ckv, cache_mla_krope, cache_diff_k, cache_diff_v, c, c_ctx, w_ada, b_ada, norm_gain, final_gain, w_in_even, w_out_even, gla_w_decay, gla_b_decay, gla_norm, mla_q_norm, mla_kv_norm, mla_w_uq, mla_w_ukv, w_in_odd, w_out_odd, diff_lambda, diff_norm, ffn_w_up, ffn_conv_w, ffn_conv_b, ffn_w_down):
    nb_p, seq_p, d = x_prompt.shape
    nb_s, seq_s, _ = x_sample.shape
    past = cache_mla_ckv.shape[2]
    depth = w_ada.shape[0]
    n_even, n_odd = w_in_even.shape[0], w_in_odd.shape[0]
    n_p, n_s = nb_p * seq_p, nb_s * seq_s
    lay = _Layout(n_p, n_s, seq_p, seq_s, TM)
    lay_ffn = _Layout(n_p, n_s, seq_p, seq_s, FFN_TM)
    per_tile = TM // seq_p
    assert seq_s % GRID_W == 0 and 1 + nb_s <= ADA_ROWS
    assert n_p % seq_s == 0 and seq_p & (seq_p - 1) == 0 and seq_s & (seq_s - 1) == 0
    assert seq_p % GLA_C == 0 and seq_s % GLA_C == 0

    cond = jnp.concatenate([c_ctx[None, :], c, jnp.zeros((ADA_ROWS - 1 - nb_s, d), F32)], 0)
    mods = _ada_call(cond, w_ada, b_ada).reshape(depth, ADA_ROWS, 6, d)

    tab_mla = _rope_tables(seq_s, both_halves=False)
    tab_diff = _rope_tables(seq_s, both_halves=True)
    gla_consts = _gla_consts(GLA_C)
    wup_all = ffn_w_up.astype(BF16)
    wdn_all = ffn_w_down.astype(BF16)
    cb_all = ffn_conv_b.reshape(depth, 1, -1)
    wo_even = w_out_even.astype(BF16)
    wo_odd = w_out_odd.astype(BF16)
    fgain = final_gain.reshape(1, d)
    dgains = diff_norm.reshape(n_odd, 1, 2 * DH_C)

    x = (x_prompt.reshape(n_p, d), x_sample.reshape(n_s, d))
    st_new = caches_even = caches_odd = None
    for l in range(depth):
        i = l // 2
        if l % 2 == 0:
            slot = _Slot(caches_even, i, n_even, per_tile)
            w, wd, bd, wq = _even_weights(w_in_even[i], gla_w_decay[i], gla_b_decay[i], mla_w_uq[i])
            wukv = mla_w_ukv[i].astype(BF16)
            qk, la, v, ga, qb, kra, kv, ckv_new, kr_new = _even_proj_call(
                lay, x, mods, l, norm_gain, w, wd, bd, mla_q_norm[i].reshape(1, -1), mla_kv_norm[i].reshape(1, -1),
                wq, wukv, tab_mla, slot, caches_even)
            caches_even = (ckv_new, kr_new)
            ggain = gla_norm[i].reshape(1, DV_A)
            og_p, st_new = _gla_call(qk, la, v, ga, ggain, gla_consts, None, 0, 0, nb_p, seq_p,
                                     _Slot(st_new, i, n_even), st_new)
            og_s, = _gla_call(qk, la, v, ga, ggain, gla_consts, state_gla, i, n_p, nb_s, seq_s, None, None)
            kv_ctx, kr_ctx = _kv_expand_call(cache_mla_ckv, cache_mla_krope, i, wukv)
            om_p = _mla_attn_call(qb, [(kv, kra, 0, seq_p)], 0, nb_p, seq_p)
            om_s = _mla_attn_call(qb, [(kv_ctx, kr_ctx, 0, past), (kv, kra, n_p, seq_s)], n_p, nb_s, seq_s)
            pairs, wo = [(og_p, og_s), (om_p, om_s)], wo_even
        else:
            slot = _Slot(caches_odd, i, n_odd, per_tile)
            lam_init = 0.8 - 0.6 * math.exp(-0.3 * l)
            n = H_C * 2 * DH_C
            col_scale = np.concatenate([np.full(n, DIFF_SCALE * LOG2E, np.float32), np.ones(2 * n, np.float32)])
            wi = (w_in_odd[i] * col_scale).astype(BF16)
            q, kb, vb, kf_new, vf_new = _odd_proj_call(lay, x, mods, l, norm_gain, wi, tab_diff, slot, caches_odd)
            caches_odd = (kf_new, vf_new)
            od_p = _diff_attn_call(q, diff_lambda, dgains, i, None, [(kb, vb, 0, seq_p)], 0, nb_p, seq_p, lam_init)
            od_s = _diff_attn_call(q, diff_lambda, dgains, i, (cache_diff_k, cache_diff_v), [(kb, vb, n_p, seq_s)],
                                   n_p, nb_s, seq_s, lam_init)
            pairs, wo = [(od_p, od_s)], wo_odd
        x = _ffn_call(lay_ffn, x, pairs, mods, l, norm_gain, wo, i, wup_all, ffn_conv_w, cb_all, wdn_all, fgain,
                      final=(l == depth - 1))
        if l < depth - 1:
            x = x[0]
    y_prompt = x[0].reshape(nb_p, seq_p, d)
    y_sample = x[1].reshape(nb_s, seq_s, d)
    diff_shape = (nb_p, n_odd, seq_p, H_C, 2 * DH_C)
    return (y_prompt, y_sample, st_new, caches_even[0], caches_even[1],
            caches_odd[0].reshape(diff_shape), caches_odd[1].reshape(diff_shape))
```
